```python
import jax
import jax.numpy as jnp
from jax import lax
import numpy as np

D_MODEL = 1024
BATCH = 8
SEQ = 8192
DEPTH = 4

GRID_W = 64
CTX_LEN = 256
N_MIXERS = 2
EPS = 1e-6
N_MOD = 6

M_HEADS = 8
M_DK = 64
M_DV = 128
M_CHUNK = 64
M_CONV = 3
M_F_BIAS = 3.5

H_HEADS = 8
H_DK = 128
H_DV = 128
H_CHUNK = 32

FFN_HIDDEN = 2816
N_EXPERTS = 8
TOP_K = 2
EXPERT_HIDDEN = 2816

F32 = jnp.float32

kernel_name = 'hybrid_mlstm_hgrn2_moe_dit'


def rmsnorm(x, g):
    xf = x.astype(F32)
    y = xf * lax.rsqrt(jnp.mean(xf * xf, axis=-1, keepdims=True) + EPS)
    return (y * g.astype(F32)).astype(x.dtype)


def short_conv(x, w):
    K, C = w.shape
    left = (K - 1) // 2
    return lax.conv_general_dilated(x, w[:, None, :].astype(x.dtype), window_strides=(1,),
                                    padding=[(left, K - 1 - left)],
                                    dimension_numbers=('NWC', 'WIO', 'NWC'),
                                    feature_group_count=C)


def to_heads(a, n_heads):
    B, T, C = a.shape
    return a.reshape(B, T, n_heads, C // n_heads).transpose(0, 2, 1, 3).astype(F32)


def to_chunks(a, L):
    B, H, T = a.shape[:3]
    return jnp.moveaxis(a.reshape(B, H, T // L, L, *a.shape[3:]), 2, 0)


def from_chunks(a):
    N, B, H, L = a.shape[:4]
    return jnp.moveaxis(a, 0, 2).reshape(B, H, N * L, *a.shape[4:])


def to_column_major(a, rows):
    B, T, C = a.shape
    return a.reshape(B, rows, GRID_W, C).transpose(0, 2, 1, 3).reshape(B, T, C)


def from_column_major(a, rows):
    B, T, C = a.shape
    return a.reshape(B, GRID_W, rows, C).transpose(0, 2, 1, 3).reshape(B, T, C)


def mlstm_scan(q, k, v, ig, lf, state):
    L = M_CHUNK
    causal = jnp.tril(jnp.ones((L, L), dtype=bool))

    def step(carry, xs):
        C, n, m = carry
        qc, kc, vc, ic, fc = xs
        b = jnp.cumsum(fc, axis=-1)
        d = jnp.where(causal, b[..., :, None] - b[..., None, :] + ic[..., None, :], -jnp.inf)
        inter = b + m[..., None]
        m_t = jnp.maximum(inter, jnp.max(d, axis=-1))
        s = jnp.einsum('bhtd,bhsd->bhts', qc, kc) * jnp.exp(d - m_t[..., None])
        w_inter = jnp.exp(inter - m_t)
        num = jnp.einsum('bhts,bhsv->bhtv', s, vc) + w_inter[..., None] * jnp.einsum('bhtd,bhdv->bhtv', qc, C)
        den = jnp.sum(s, axis=-1) + w_inter * jnp.einsum('bhtd,bhd->bht', qc, n)
        h = num / jnp.maximum(jnp.abs(den), jnp.exp(-m_t))[..., None]
        b_last = b[..., -1]
        g = b_last[..., None] - b + ic
        m_new = jnp.maximum(b_last + m, jnp.max(g, axis=-1))
        w_k = jnp.exp(g - m_new[..., None])
        decay = jnp.exp(b_last + m - m_new)
        C = decay[..., None, None] * C + jnp.einsum('bhs,bhsd,bhsv->bhdv', w_k, kc, vc)
        n = decay[..., None] * n + jnp.einsum('bhs,bhsd->bhd', w_k, kc)
        return (C, n, m_new), h

    state, h = lax.scan(step, state, tuple(to_chunks(a, L) for a in (q, k, v, ig, lf)))
    return from_chunks(h), state


def hgrn2_scan(q, k, v, lf, S):
    L = H_CHUNK
    causal = jnp.tril(jnp.ones((L, L), dtype=bool))[:, :, None]

    def step(S, xs):
        qc, kc, vc, fc = xs
        a = jnp.cumsum(fc, axis=-2)
        decay = jnp.exp(jnp.where(causal, a[..., :, None, :] - a[..., None, :, :], -jnp.inf))
        s = jnp.einsum('bhtd,bhsd,bhtsd->bhts', qc, kc, decay)
        o = jnp.einsum('bhts,bhsv->bhtv', s, vc) + jnp.einsum('bhtd,bhdv->bhtv', qc * jnp.exp(a), S)
        a_last = a[..., -1:, :]
        S = jnp.exp(a_last[..., 0, :])[..., None] * S + jnp.einsum('bhsd,bhsv->bhdv', kc * jnp.exp(a_last - a), vc)
        return S, o

    S, o = lax.scan(step, S, tuple(to_chunks(a, L) for a in (q, k, v, lf)))
    return from_chunks(o), S


def _flip_t(a):
    return jnp.flip(a, axis=2)


def bidirectional_scan(scan_fn, ctx_fwd, ctx_bwd, lat_fwd, lat_bwd, state0):
    hc_f, st_f = scan_fn(*ctx_fwd, state0)
    hc_b, st_b = scan_fn(*[_flip_t(a) for a in ctx_bwd], state0)
    hl_f, _ = scan_fn(*lat_fwd, st_f)
    hl_b, _ = scan_fn(*[_flip_t(a) for a in lat_bwd], st_b)
    return hc_f + _flip_t(hc_b), hl_f + _flip_t(hl_b)


def head_out(h, gate, g_head, w_out, dtype):
    B, H, T, dv = h.shape
    h = h.transpose(0, 2, 1, 3)
    h = h * lax.rsqrt(jnp.mean(h * h, axis=-1, keepdims=True) + EPS)
    h = h.reshape(B, T, H * dv) * g_head.astype(F32) * gate
    return h.astype(dtype) @ w_out


def mlstm_mixer(u_ctx, u_lat, w_in, b_gate, w_conv, g_head, w_out, with_ctx_out):
    qk_w = 2 * M_HEADS * M_DK
    v_w = M_HEADS * M_DV

    def streams(u):
        B, T, _ = u.shape
        qk, v, o, gates = jnp.split(u @ w_in, [qk_w, qk_w + v_w, qk_w + 2 * v_w], axis=-1)
        qk = jax.nn.silu(short_conv(qk, w_conv))
        q, k = jnp.split(qk, 2, axis=-1)
        q = to_heads(q, M_HEADS)
        k = to_heads(k, M_HEADS) * (M_DK ** -0.5)
        v = to_heads(v, M_HEADS)
        g = (gates + b_gate.reshape(-1)).astype(F32).reshape(B, T, 4, M_HEADS).transpose(2, 0, 3, 1)
        fwd = (q, k, v, g[0], jax.nn.log_sigmoid(g[1]))
        bwd = (q, k, v, g[2], jax.nn.log_sigmoid(g[3]))
        return fwd, bwd, jax.nn.sigmoid(o.astype(F32))

    c_fwd, c_bwd, c_gate = streams(u_ctx)
    l_fwd, l_bwd, l_gate = streams(u_lat)
    B = u_lat.shape[0]
    state0 = (jnp.zeros((B, M_HEADS, M_DK, M_DV), F32),
              jnp.zeros((B, M_HEADS, M_DK), F32),
              jnp.zeros((B, M_HEADS), F32))
    h_ctx, h_lat = bidirectional_scan(mlstm_scan, c_fwd, c_bwd, l_fwd, l_bwd, state0)
    y_lat = head_out(h_lat, l_gate, g_head, w_out, u_lat.dtype)
    y_ctx = head_out(h_ctx, c_gate, g_head, w_out, u_ctx.dtype) if with_ctx_out else None
    return y_ctx, y_lat


def hgrn2_mixer(u_ctx, u_lat, w_in, b_f, lb, g_head, w_out, rows, with_ctx_out):
    kw = H_HEADS * H_DK
    vw = H_HEADS * H_DV

    def streams(u):
        q, v, f_f, f_b, g = jnp.split(u @ w_in, [kw, kw + vw, 2 * kw + vw, 3 * kw + vw], axis=-1)

        def direction(f_pre, bias, lbd):
            f = lbd + (1.0 - lbd) * jax.nn.sigmoid(f_pre.astype(F32) + bias.astype(F32))
            return to_heads(1.0 - f, H_HEADS), to_heads(jnp.log(f), H_HEADS)

        k_f, lf_f = direction(f_f, b_f[0], lb[0])
        k_b, lf_b = direction(f_b, b_f[1], lb[1])
        q = to_heads(q, H_HEADS)
        v = to_heads(v, H_HEADS)
        return (q, k_f, v, lf_f), (q, k_b, v, lf_b), jax.nn.sigmoid(g.astype(F32))

    c_fwd, c_bwd, c_gate = streams(u_ctx)
    l_fwd, l_bwd, l_gate = streams(to_column_major(u_lat, rows))
    B = u_lat.shape[0]
    state0 = jnp.zeros((B, H_HEADS, H_DK, H_DV), F32)
    h_ctx, h_lat = bidirectional_scan(hgrn2_scan, c_fwd, c_bwd, l_fwd, l_bwd, state0)
    y_lat = from_column_major(head_out(h_lat, l_gate, g_head, w_out, u_lat.dtype), rows)
    y_ctx = head_out(h_ctx, c_gate, g_head, w_out, u_ctx.dtype) if with_ctx_out else None
    return y_ctx, y_lat


def swiglu(u, w_gu, w_down):
    g, up = jnp.split(u @ w_gu, 2, axis=-1)
    return (jax.nn.silu(g) * up) @ w_down


def moe_swiglu(u, w_router, b_router, w_gu, w_down):
    logits = (u @ w_router).astype(F32) + b_router.astype(F32)
    top_val, top_idx = lax.top_k(logits, TOP_K)
    weights = jax.nn.softmax(top_val, axis=-1)
    combine = jnp.einsum('...k,...ke->...e', weights, jax.nn.one_hot(top_idx, N_EXPERTS, dtype=F32))
    out = jnp.zeros(u.shape[:-1] + (w_down.shape[-1],), F32)
    for e in range(N_EXPERTS):
        out = out + combine[..., e:e + 1] * swiglu(u, w_gu[e], w_down[e]).astype(F32)
    return out.astype(u.dtype)


def setup_inputs(seed: int = 0) -> dict:
    key = jax.random.key(seed)
    keys = jax.random.split(key, 24)
    D = D_MODEL
    n_even = (DEPTH + 1) // 2
    n_odd = DEPTH // 2
    m_in = 2 * M_HEADS * M_DK + 2 * M_HEADS * M_DV + 4 * M_HEADS
    h_in = 3 * H_HEADS * H_DK + 2 * H_HEADS * H_DV

    def nrm(k, shape, scale):
        return jax.random.normal(k, shape, jnp.float32) * scale

    gate_offset = jnp.tile(jnp.array([0.0, M_F_BIAS, 0.0, M_F_BIAS], jnp.float32)[:, None], (1, M_HEADS))
    return {
        'x': nrm(keys[0], (BATCH, SEQ, D), 1.0),
        'c': nrm(keys[1], (BATCH, D), 1.0),
        'ctx': nrm(keys[2], (BATCH, CTX_LEN, D), 1.0),
        'c_ctx': nrm(keys[3], (D,), 1.0),
        'mod_w': nrm(keys[4], (DEPTH, D, N_MOD * D), 0.5 * D ** -0.5),
        'mod_b': nrm(keys[5], (DEPTH, N_MOD * D), 0.02),
        'norm_g': 1.0 + nrm(keys[6], (DEPTH, 4, D), 0.1),
        'm_w_in': nrm(keys[7], (n_even, D, m_in), D ** -0.5),
        'm_b_gate': gate_offset + nrm(keys[8], (n_even, 4, M_HEADS), 0.5),
        'm_w_conv': nrm(keys[9], (n_even, M_CONV, 2 * M_HEADS * M_DK), M_CONV ** -0.5),
        'm_g_head': 1.0 + nrm(keys[10], (n_even, M_HEADS * M_DV), 0.1),
        'm_w_out': nrm(keys[11], (n_even, M_HEADS * M_DV, D), (M_HEADS * M_DV) ** -0.5),
        'h_w_in': nrm(keys[12], (n_odd, D, h_in), D ** -0.5),
        'h_b_f': nrm(keys[13], (n_odd, 2, H_HEADS * H_DK), 0.1),
        'h_lb_raw': nrm(keys[14], (n_odd, 2, H_HEADS * H_DK), 1.0),
        'h_g_head': 1.0 + nrm(keys[15], (n_odd, H_HEADS * H_DV), 0.1),
        'h_w_out': nrm(keys[16], (n_odd, H_HEADS * H_DV, D), (H_HEADS * H_DV) ** -0.5),
        'f_w_gu': nrm(keys[17], (n_even, D, 2 * FFN_HIDDEN), D ** -0.5),
        'f_w_down': nrm(keys[18], (n_even, FFN_HIDDEN, D), FFN_HIDDEN ** -0.5),
        'e_w_router': nrm(keys[19], (n_odd, D, N_EXPERTS), D ** -0.5),
        'e_b_router': nrm(keys[20], (n_odd, N_EXPERTS), 0.01),
        'e_w_gu': nrm(keys[21], (n_odd, N_EXPERTS, D, 2 * EXPERT_HIDDEN), D ** -0.5),
        'e_w_down': nrm(keys[22], (n_odd, N_EXPERTS, EXPERT_HIDDEN, D), EXPERT_HIDDEN ** -0.5),
    }


def reference(x, c, ctx, c_ctx, mod_w, mod_b, norm_g, m_w_in, m_b_gate, m_w_conv, m_g_head, m_w_out,
              h_w_in, h_b_f, h_lb_raw, h_g_head, h_w_out, f_w_gu, f_w_down,
              e_w_router, e_b_router, e_w_gu, e_w_down):
    B = x.shape[0]
    rows = x.shape[1] // GRID_W
    silu_c = jax.nn.silu(c)
    silu_cc = jax.nn.silu(c_ctx)
    lb_all = jax.nn.softmax(h_lb_raw.astype(F32), axis=0)
    lb_all = jnp.cumsum(lb_all, axis=0) - lb_all[0]
    lat, cx = x, ctx
    for i in range(DEPTH):
        j = i // 2
        with_ctx = i < DEPTH - 1
        mod_l = (silu_c @ mod_w[i] + mod_b[i]).reshape(B, 1, N_MOD, D_MODEL)
        mod_c = (silu_cc @ mod_w[i] + mod_b[i]).reshape(1, N_MOD, D_MODEL)
        sh1, sc1, gt1, sh2, sc2, gt2 = [mod_l[:, :, s] for s in range(N_MOD)]
        csh1, csc1, cgt1, csh2, csc2, cgt2 = [mod_c[:, s] for s in range(N_MOD)]

        u_l = rmsnorm(lat, norm_g[i, 0]) * (1 + sc1) + sh1
        u_c = rmsnorm(cx, norm_g[i, 0]) * (1 + csc1) + csh1
        if i % N_MIXERS == 0:
            y_c, y_l = mlstm_mixer(u_c, u_l, m_w_in[j], m_b_gate[j], m_w_conv[j], m_g_head[j], m_w_out[j], with_ctx)
        else:
            y_c, y_l = hgrn2_mixer(u_c, u_l, h_w_in[j], h_b_f[j], lb_all[j], h_g_head[j], h_w_out[j], rows, with_ctx)
        lat = lat + gt1 * rmsnorm(y_l, norm_g[i, 1])

        if i % 2 == 0:
            ffn = lambda u: swiglu(u, f_w_gu[j], f_w_down[j])
        else:
            ffn = lambda u: moe_swiglu(u, e_w_router[j], e_b_router[j], e_w_gu[j], e_w_down[j])
        u_l = rmsnorm(lat, norm_g[i, 2]) * (1 + sc2) + sh2
        lat = lat + gt2 * rmsnorm(ffn(u_l), norm_g[i, 3])

        if with_ctx:
            cx = cx + cgt1 * rmsnorm(y_c, norm_g[i, 1])
            u_c = rmsnorm(cx, norm_g[i, 2]) * (1 + csc2) + csh2
            cx = cx + cgt2 * rmsnorm(ffn(u_c), norm_g[i, 3])
    return lat
```

```python
import functools

import jax
import jax.numpy as jnp
from jax import lax
from jax.experimental import pallas as pl
from jax.experimental.pallas import tpu as pltpu

F32 = jnp.float32
BF16 = jnp.bfloat16
EPS = 1e-6
NEG = -1e30

GRID_W = 64
N_MOD = 6
M_HEADS, M_DK, M_DV = 8, 64, 128
H_HEADS, H_DK, H_DV = 8, 128, 128
N_EXPERTS = 8
LANES = 128
SUBLANES = 8
CHUNK = 128
MOD_ROWS = 16
VMEM_LIMIT = 56 * 1024 * 1024

HI = lax.Precision.HIGHEST
NT_DIMS = (((1,), (1,)), ((), ()))
TN_DIMS = (((0,), (0,)), ((), ()))


def _cp(*sem):
    return pltpu.CompilerParams(dimension_semantics=sem, vmem_limit_bytes=VMEM_LIMIT)


def _full(a, nargs):
    zeros = (0,) * a.ndim
    return pl.BlockSpec(a.shape, lambda *_: zeros)


def _sigmoid(x):
    return 1.0 / (1.0 + jnp.exp(-x))


def _silu(x):
    return x * _sigmoid(x)


def _log_sigmoid(x):
    return jnp.minimum(x, 0.0) - jnp.log(1.0 + jnp.exp(-jnp.abs(x)))


def _rms(x):
    return x * lax.rsqrt(jnp.mean(x * x, axis=-1, keepdims=True) + EPS)


def _normmod(x, g, mod, s):
    return _rms(x) * (g * (1.0 + mod[s + 1:s + 2])) + mod[s:s + 1]


def _split2(x):
    hi = x.astype(BF16)
    lo = (x - hi.astype(F32)).astype(BF16)
    return hi, lo


def _cumsum_mat(n, reverse):
    r = lax.broadcasted_iota(jnp.int32, (n, n), 0)
    c = lax.broadcasted_iota(jnp.int32, (n, n), 1)
    return jnp.where((c >= r) if reverse else (c <= r), 1.0, 0.0).astype(BF16)


def _cumsum_time(x, tri):
    hi, lo = _split2(x)
    return (jnp.dot(tri, hi, preferred_element_type=F32)
            + jnp.dot(tri, lo, preferred_element_type=F32))


class _Seg:
    def __init__(self, batch, seqlen, is_ctx):
        self.batch, self.seqlen, self.is_ctx = batch, seqlen, is_ctx
        self.n = batch * seqlen

    def tile(self, pref):
        tm = min(pref, self.n)
        assert self.n % tm == 0 and (self.seqlen % tm == 0 or tm % self.seqlen == 0)
        return tm

    def mod_spec(self, tm, d, nargs):
        if self.is_ctx:
            row = lambda j: self.batch
        else:
            row = lambda j: (j * tm) // self.seqlen
        if nargs == 1:
            return pl.BlockSpec((1, N_MOD, d), lambda j: (row(j), 0, 0))
        if nargs == 2:
            return pl.BlockSpec((1, N_MOD, d), lambda j, k: (row(j), 0, 0))
        return pl.BlockSpec((1, N_MOD, d), lambda j, e, k: (row(j), 0, 0))


def _mod_kernel(c_ref, w_ref, b_ref, o_ref):
    c = c_ref[...]
    o_ref[0] = jnp.dot(_silu(c), w_ref[0], preferred_element_type=F32, precision=HI) + b_ref[0]


def _modulation(c_all, mod_w, mod_b):
    depth, d, n = mod_w.shape
    tn = n // 4
    return pl.pallas_call(
        _mod_kernel,
        out_shape=jax.ShapeDtypeStruct((depth, MOD_ROWS, n), F32),
        grid=(depth, n // tn),
        in_specs=[pl.BlockSpec((MOD_ROWS, d), lambda i, j: (0, 0)),
                  pl.BlockSpec((1, d, tn), lambda i, j: (i, 0, j)),
                  pl.BlockSpec((1, 1, tn), lambda i, j: (i, 0, j))],
        out_specs=pl.BlockSpec((1, MOD_ROWS, tn), lambda i, j: (i, 0, j)),
        compiler_params=_cp("arbitrary", "arbitrary"),
        name="modulation",
    )(c_all, mod_w, mod_b.reshape(depth, 1, n))


def _proj_m_kernel(xp_ref, x_ref, xn_ref, mod_ref, g_ref, wqk_ref, wv_ref, wo_ref, wg_ref,
                   cw_ref, gb_ref, gm_ref, q_ref, k_ref, v_ref, o_ref, gt_ref, *, tm, seqlen):
    j = pl.program_id(0)
    mod = mod_ref[0]
    g = g_ref[...]
    u = _normmod(x_ref[...], g, mod, 0).astype(BF16)
    uh = _normmod(jnp.concatenate([xp_ref[...], xn_ref[...]], axis=0), g, mod, 0).astype(BF16)
    wqk = wqk_ref[...]
    z = jnp.dot(u, wqk, preferred_element_type=F32)
    zh = jnp.dot(uh, wqk, preferred_element_type=F32)
    local = lax.broadcasted_iota(jnp.int32, (tm, 1), 0)
    pos = lax.rem(j * tm + local, seqlen)
    zp = jnp.where(local == 0, zh[SUBLANES - 1:SUBLANES], pltpu.roll(z, 1, 0))
    zn = jnp.where(local == tm - 1, zh[SUBLANES:SUBLANES + 1], pltpu.roll(z, tm - 1, 0))
    zp = jnp.where(pos == 0, 0.0, zp)
    zn = jnp.where(pos == seqlen - 1, 0.0, zn)
    cw = cw_ref[...]
    a = _silu(cw[0:1] * zp + cw[1:2] * z + cw[2:3] * zn)
    half = a.shape[1] // 2
    q_ref[...] = a[:, :half].astype(BF16)
    k_ref[...] = (a[:, half:] * (M_DK ** -0.5)).astype(BF16)
    v_ref[...] = jnp.dot(u, wv_ref[...], preferred_element_type=F32).astype(BF16)
    o_ref[...] = jnp.dot(u, wo_ref[...], preferred_element_type=F32).astype(BF16)
    zg = jnp.dot(u, wg_ref[...], preferred_element_type=F32) + gb_ref[...]
    gt_ref[...] = jnp.where(gm_ref[...] > 0.5, _log_sigmoid(zg), zg)


def _proj_m(x, seg, mod, g, wqk, wv, wo, wg, cw, gb, gm):
    n, d = x.shape
    tm = seg.tile(512)
    nb8 = n // SUBLANES
    r8 = tm // SUBLANES
    kern = functools.partial(_proj_m_kernel, tm=tm, seqlen=seg.seqlen)
    nqk = wqk.shape[1]
    consts = (g, wqk, wv, wo, wg, cw, gb, gm)
    return pl.pallas_call(
        kern,
        out_shape=[jax.ShapeDtypeStruct((n, nqk // 2), BF16),
                   jax.ShapeDtypeStruct((n, nqk // 2), BF16),
                   jax.ShapeDtypeStruct((n, wv.shape[1]), BF16),
                   jax.ShapeDtypeStruct((n, wo.shape[1]), BF16),
                   jax.ShapeDtypeStruct((n, LANES), F32)],
        grid=(n // tm,),
        in_specs=[pl.BlockSpec((SUBLANES, d), lambda j: (jnp.maximum(j * r8 - 1, 0), 0)),
                  pl.BlockSpec((tm, d), lambda j: (j, 0)),
                  pl.BlockSpec((SUBLANES, d), lambda j: (jnp.minimum((j + 1) * r8, nb8 - 1), 0)),
                  seg.mod_spec(tm, d, 1)] + [_full(a, 1) for a in consts],
        out_specs=[pl.BlockSpec((tm, nqk // 2), lambda j: (j, 0)),
                   pl.BlockSpec((tm, nqk // 2), lambda j: (j, 0)),
                   pl.BlockSpec((tm, wv.shape[1]), lambda j: (j, 0)),
                   pl.BlockSpec((tm, wo.shape[1]), lambda j: (j, 0)),
                   pl.BlockSpec((tm, LANES), lambda j: (j, 0))],
        compiler_params=_cp("arbitrary"),
        name="mlstm_proj",
    )(x, x, x, mod, *consts)


def _mlstm_dir(q_ref, k_ref, v_ref, g_ref, h_ref, s_scr, m_scr, reverse, d):
    L = CHUNK
    g = g_ref[...]
    tri = _cumsum_mat(L, reverse)
    b = _cumsum_time(g, tri)
    gT = g.T
    bT = b.T
    r = lax.broadcasted_iota(jnp.int32, (L, L), 0)
    c = lax.broadcasted_iota(jnp.int32, (L, L), 1)
    causal = (c >= r) if reverse else (c <= r)
    last = 0 if reverse else L - 1
    c_ig = 2 * M_HEADS * d
    c_lf = c_ig + M_HEADS
    lane = lax.broadcasted_iota(jnp.int32, (1, LANES), 1)
    srow = lax.broadcasted_iota(jnp.int32, (LANES, 1), 0)
    ones = jnp.ones((L, M_DV), BF16)
    for p in range(M_HEADS // 2):
        q_p = q_ref[:, p * LANES:(p + 1) * LANES]
        k_p = k_ref[:, p * LANES:(p + 1) * LANES]
        s_pair = s_scr[d, p]
        s_bf = s_pair.astype(BF16)
        dsum = None
        decs = []
        for e in range(2):
            h = 2 * p + e
            in_head = (lane // M_DK) == e
            k_e = jnp.where(in_head, k_p, jnp.zeros_like(k_p))
            q_e = jnp.where(in_head, q_p, jnp.zeros_like(q_p))
            ig_col = g[:, c_ig + h:c_ig + h + 1]
            b_col = b[:, c_lf + h:c_lf + h + 1]
            ig_row = gT[c_ig + h:c_ig + h + 1, :]
            b_row = bT[c_lf + h:c_lf + h + 1, :]
            m_prev = m_scr[d * M_HEADS + h:d * M_HEADS + h + 1, 0:1]
            b_last = b_col[last:last + 1, :]
            dmat = jnp.where(causal, b_col - b_row + ig_row, NEG)
            inter = b_col + m_prev
            m_t = jnp.maximum(inter, jnp.max(dmat, axis=1, keepdims=True))
            qk = lax.dot_general(q_p, k_e, NT_DIMS, preferred_element_type=F32)
            s = (qk * jnp.exp(dmat - m_t)).astype(BF16)
            vp = jnp.concatenate([v_ref[:, h * M_DV:(h + 1) * M_DV], ones], axis=1)
            tot = (jnp.dot(s, vp, preferred_element_type=F32)
                   + jnp.exp(inter - m_t) * jnp.dot(q_e, s_bf, preferred_element_type=F32))
            den = jnp.maximum(jnp.abs(tot[:, M_DV:]), jnp.exp(-m_t))
            h_ref[:, h * M_DV:(h + 1) * M_DV] = (tot[:, :M_DV] / den).astype(BF16)
            g_col = b_last - b_col + ig_col
            g_row = b_last - b_row + ig_row
            m_new = jnp.maximum(b_last + m_prev, jnp.max(g_row, axis=1, keepdims=True))
            kw = (k_e.astype(F32) * jnp.exp(g_col - m_new)).astype(BF16)
            ds = lax.dot_general(kw, vp, TN_DIMS, preferred_element_type=F32)
            dsum = ds if dsum is None else dsum + ds
            decs.append(jnp.exp(b_last + m_prev - m_new))
            m_scr[d * M_HEADS + h:d * M_HEADS + h + 1, :] = jnp.broadcast_to(m_new, (1, LANES))
        dec = jnp.where(srow < M_DK, decs[0], decs[1])
        s_scr[d, p] = dec * s_pair + dsum


def _mlstm_scan_kernel(qf, kf, vf, gf, qb, kb, vb, gb, s0_ref, m0_ref,
                       hf_ref, hb_ref, st_ref, mt_ref, s_scr, m_scr):
    n = pl.program_id(1)

    @pl.when(n == 0)
    def _():
        s_scr[...] = s0_ref[0]
        m_scr[...] = m0_ref[0]

    _mlstm_dir(qf, kf, vf, gf, hf_ref, s_scr, m_scr, False, 0)
    _mlstm_dir(qb, kb, vb, gb, hb_ref, s_scr, m_scr, True, 1)

    @pl.when(n == pl.num_programs(1) - 1)
    def _():
        st_ref[0] = s_scr[...]
        mt_ref[0] = m_scr[...]


def _mlstm_scan(q, k, v, gates, s0, m0, seg):
    L = CHUNK
    assert seg.seqlen % L == 0
    n = seg.seqlen // L
    fwd = lambda b, i: (b * n + i, 0)
    bwd = lambda b, i: (b * n + (n - 1 - i), 0)
    dq, dv = q.shape[1], v.shape[1]
    st = lambda b, i: (b, 0, 0, 0, 0)
    mt = lambda b, i: (b, 0, 0)
    return pl.pallas_call(
        _mlstm_scan_kernel,
        out_shape=[jax.ShapeDtypeStruct((seg.n, dv), BF16), jax.ShapeDtypeStruct((seg.n, dv), BF16),
                   jax.ShapeDtypeStruct(s0.shape, F32), jax.ShapeDtypeStruct(m0.shape, F32)],
        grid=(seg.batch, n),
        in_specs=[pl.BlockSpec((L, dq), fwd), pl.BlockSpec((L, dq), fwd), pl.BlockSpec((L, dv), fwd),
                  pl.BlockSpec((L, LANES), fwd),
                  pl.BlockSpec((L, dq), bwd), pl.BlockSpec((L, dq), bwd), pl.BlockSpec((L, dv), bwd),
                  pl.BlockSpec((L, LANES), bwd),
                  pl.BlockSpec((1,) + s0.shape[1:], st), pl.BlockSpec((1,) + m0.shape[1:], mt)],
        out_specs=[pl.BlockSpec((L, dv), fwd), pl.BlockSpec((L, dv), bwd),
                   pl.BlockSpec((1,) + s0.shape[1:], st), pl.BlockSpec((1,) + m0.shape[1:], mt)],
        scratch_shapes=[pltpu.VMEM(s0.shape[1:], F32), pltpu.VMEM(m0.shape[1:], F32)],
        compiler_params=_cp("arbitrary", "arbitrary"),
        name="mlstm_scan",
    )(q, k, v, gates, q, k, v, gates, s0, m0)


def _head_out(hf, hb, gate_pre, gh, w, n_heads):
    h = hf.astype(F32) + hb.astype(F32)
    dv = h.shape[1] // n_heads
    parts = [_rms(h[:, i * dv:(i + 1) * dv]) for i in range(n_heads)]
    hn = jnp.concatenate(parts, axis=1) * gh * _sigmoid(gate_pre.astype(F32))
    return jnp.dot(hn.astype(BF16), w, preferred_element_type=F32)


def _mix_out_kernel(hf_ref, hb_ref, o_ref, x_ref, mod_ref, gh_ref, w_ref, g1_ref, out_ref, *, n_heads):
    mod = mod_ref[0]
    y = _head_out(hf_ref[...], hb_ref[...], o_ref[...], gh_ref[...], w_ref[...], n_heads)
    out_ref[...] = x_ref[...] + mod[2:3] * (_rms(y) * g1_ref[...])


def _mix_out(hf, hb, o, x, seg, mod, gh, w, g1, n_heads):
    n, d = x.shape
    tm = seg.tile(512)
    rmap = lambda j: (j, 0)
    consts = (gh, w, g1)
    return pl.pallas_call(
        functools.partial(_mix_out_kernel, n_heads=n_heads),
        out_shape=jax.ShapeDtypeStruct(x.shape, F32),
        grid=(n // tm,),
        in_specs=[pl.BlockSpec((tm, hf.shape[1]), rmap), pl.BlockSpec((tm, hf.shape[1]), rmap),
                  pl.BlockSpec((tm, o.shape[1]), rmap), pl.BlockSpec((tm, d), rmap),
                  seg.mod_spec(tm, d, 1)] + [_full(a, 1) for a in consts],
        out_specs=pl.BlockSpec((tm, d), rmap),
        input_output_aliases={3: 0},
        compiler_params=_cp("arbitrary"),
        name="mix_out",
    )(hf, hb, o, x, mod, *consts)


def _ffn_kernel(x_ref, mod_ref, g2_ref, wg_ref, wu_ref, wd_ref, g3_ref, out_ref, u_scr, acc_scr):
    kk = pl.program_id(1)

    @pl.when(kk == 0)
    def _():
        u_scr[...] = _normmod(x_ref[...], g2_ref[...], mod_ref[0], 3).astype(BF16)
        acc_scr[...] = jnp.zeros_like(acc_scr)

    u = u_scr[...]
    gg = jnp.dot(u, wg_ref[...], preferred_element_type=F32)
    up = jnp.dot(u, wu_ref[...], preferred_element_type=F32)
    hh = (_silu(gg) * up).astype(BF16)
    acc_scr[...] += jnp.dot(hh, wd_ref[...], preferred_element_type=F32)

    @pl.when(kk == pl.num_programs(1) - 1)
    def _():
        mod = mod_ref[0]
        out_ref[...] = x_ref[...] + mod[5:6] * (_rms(acc_scr[...]) * g3_ref[...])


def _ffn(x, seg, mod, g2, wgu, wd, g3, th):
    n, d = x.shape
    tm = seg.tile(1024)
    hd = wd.shape[0]
    assert hd % th == 0
    nk = hd // th
    return pl.pallas_call(
        _ffn_kernel,
        out_shape=jax.ShapeDtypeStruct((n, d), F32),
        grid=(n // tm, nk),
        in_specs=[pl.BlockSpec((tm, d), lambda i, k: (i, 0)),
                  seg.mod_spec(tm, d, 2),
                  _full(g2, 2),
                  pl.BlockSpec((d, th), lambda i, k: (0, k)),
                  pl.BlockSpec((d, th), lambda i, k: (0, nk + k)),
                  pl.BlockSpec((th, d), lambda i, k: (k, 0)),
                  _full(g3, 2)],
        out_specs=pl.BlockSpec((tm, d), lambda i, k: (i, 0)),
        scratch_shapes=[pltpu.VMEM((tm, d), BF16), pltpu.VMEM((tm, d), F32)],
        input_output_aliases={0: 0},
        compiler_params=_cp("arbitrary", "arbitrary"),
        name="ffn",
    )(x, mod, g2, wgu, wgu, wd, g3)


def _router_kernel(x_ref, mod_ref, g2_ref, wr_ref, br_ref, u_ref, comb_ref):
    u = _normmod(x_ref[...], g2_ref[...], mod_ref[0], 3)
    u_ref[...] = u.astype(BF16)
    lane = lax.broadcasted_iota(jnp.int32, (1, LANES), 1)
    logits = jnp.dot(u, wr_ref[...], preferred_element_type=F32, precision=HI) + br_ref[...]
    logits = jnp.where(lane < N_EXPERTS, logits, NEG)
    m1 = jnp.max(logits, axis=1, keepdims=True)
    i1 = jnp.min(jnp.where(logits == m1, lane, LANES), axis=1, keepdims=True)
    rest = jnp.where(lane == i1, NEG, logits)
    m2 = jnp.max(rest, axis=1, keepdims=True)
    i2 = jnp.min(jnp.where(rest == m2, lane, LANES), axis=1, keepdims=True)
    e2 = jnp.exp(m2 - m1)
    w1 = 1.0 / (1.0 + e2)
    comb_ref[...] = jnp.where(lane == i1, w1, 0.0) + jnp.where(lane == i2, e2 * w1, 0.0)


def _router(x, seg, mod, g2, wr, br):
    n, d = x.shape
    tm = seg.tile(1024)
    consts = (g2, wr, br)
    return pl.pallas_call(
        _router_kernel,
        out_shape=[jax.ShapeDtypeStruct((n, d), BF16), jax.ShapeDtypeStruct((n, LANES), F32)],
        grid=(n // tm,),
        in_specs=[pl.BlockSpec((tm, d), lambda i: (i, 0)), seg.mod_spec(tm, d, 1)]
                 + [_full(a, 1) for a in consts],
        out_specs=[pl.BlockSpec((tm, d), lambda i: (i, 0)), pl.BlockSpec((tm, LANES), lambda i: (i, 0))],
        compiler_params=_cp("arbitrary"),
        name="router",
    )(x, mod, *consts)


def _moe_kernel(u_ref, comb_ref, x_ref, mod_ref, wg_ref, wu_ref, wd_ref, g3_ref, out_ref, acc_scr):
    e = pl.program_id(1)
    kk = pl.program_id(2)

    @pl.when((e == 0) & (kk == 0))
    def _():
        acc_scr[...] = jnp.zeros_like(acc_scr)

    u = u_ref[...]
    gg = jnp.dot(u, wg_ref[0], preferred_element_type=F32)
    up = jnp.dot(u, wu_ref[0], preferred_element_type=F32)
    hh = (_silu(gg) * up).astype(BF16)
    lane = lax.broadcasted_iota(jnp.int32, (1, LANES), 1)
    cw = jnp.sum(jnp.where(lane == e, comb_ref[...], 0.0), axis=1, keepdims=True)
    acc_scr[...] += cw * jnp.dot(hh, wd_ref[0], preferred_element_type=F32)

    @pl.when((e == pl.num_programs(1) - 1) & (kk == pl.num_programs(2) - 1))
    def _():
        mod = mod_ref[0]
        out_ref[...] = x_ref[...] + mod[5:6] * (_rms(acc_scr[...]) * g3_ref[...])


def _moe(u, comb, x, seg, mod, wgu, wd, g3, th):
    n, d = x.shape
    tm = seg.tile(1024)
    ne, hd, _ = wd.shape
    nk = hd // th
    return pl.pallas_call(
        _moe_kernel,
        out_shape=jax.ShapeDtypeStruct((n, d), F32),
        grid=(n // tm, ne, nk),
        in_specs=[pl.BlockSpec((tm, d), lambda i, e, k: (i, 0)),
                  pl.BlockSpec((tm, LANES), lambda i, e, k: (i, 0)),
                  pl.BlockSpec((tm, d), lambda i, e, k: (i, 0)),
                  seg.mod_spec(tm, d, 3),
                  pl.BlockSpec((1, d, th), lambda i, e, k: (e, 0, k)),
                  pl.BlockSpec((1, d, th), lambda i, e, k: (e, 0, nk + k)),
                  pl.BlockSpec((1, th, d), lambda i, e, k: (e, k, 0)),
                  _full(g3, 3)],
        out_specs=pl.BlockSpec((tm, d), lambda i, e, k: (i, 0)),
        scratch_shapes=[pltpu.VMEM((tm, d), F32)],
        input_output_aliases={2: 0},
        compiler_params=_cp("arbitrary", "arbitrary", "arbitrary"),
        name="moe",
    )(u, comb, x, mod, wgu, wgu, wd, g3)


def _proj_h_body(u, w_refs, out_refs):
    for w_ref, o_ref in zip(w_refs, out_refs):
        o_ref[...] = jnp.dot(u, w_ref[...], preferred_element_type=F32).astype(BF16)


def _proj_h_lat_kernel(x_ref, mod_ref, g_ref, *refs, rows_per_col):
    n = (len(refs) - 1) // 2
    w_refs, out_refs, u_scr = refs[:n], refs[n:2 * n], refs[2 * n]
    mod = mod_ref[0]
    g = g_ref[...]
    for cl in range(SUBLANES):
        u_scr[cl * rows_per_col:(cl + 1) * rows_per_col, :] = _normmod(x_ref[:, cl, :], g, mod, 0).astype(BF16)
    _proj_h_body(u_scr[...], w_refs, out_refs)


def _proj_h_ctx_kernel(x_ref, mod_ref, g_ref, *refs):
    n = len(refs) // 2
    u = _normmod(x_ref[...], g_ref[...], mod_ref[0], 0).astype(BF16)
    _proj_h_body(u, refs[:n], refs[n:])


def _proj_h_lat(x, seg, mod, g, ws):
    n, d = x.shape
    grid_rows = seg.seqlen // GRID_W
    x3 = x.reshape(n // GRID_W, GRID_W, d)
    tcol = SUBLANES * grid_rows
    ncb = GRID_W // SUBLANES
    return pl.pallas_call(
        functools.partial(_proj_h_lat_kernel, rows_per_col=grid_rows),
        out_shape=[jax.ShapeDtypeStruct((n, w.shape[1]), BF16) for w in ws],
        grid=(seg.batch, ncb),
        in_specs=[pl.BlockSpec((grid_rows, SUBLANES, d), lambda b, c: (b, c, 0)),
                  pl.BlockSpec((1, N_MOD, d), lambda b, c: (b, 0, 0)), _full(g, 2)]
                 + [_full(w, 2) for w in ws],
        out_specs=[pl.BlockSpec((tcol, w.shape[1]), lambda b, c: (b * ncb + c, 0)) for w in ws],
        scratch_shapes=[pltpu.VMEM((tcol, d), BF16)],
        compiler_params=_cp("arbitrary", "arbitrary"),
        name="hgrn_proj_lat",
    )(x3, mod, g, *ws)


def _proj_h_ctx(x, seg, mod, g, ws):
    n, d = x.shape
    tm = seg.tile(512)
    return pl.pallas_call(
        _proj_h_ctx_kernel,
        out_shape=[jax.ShapeDtypeStruct((n, w.shape[1]), BF16) for w in ws],
        grid=(n // tm,),
        in_specs=[pl.BlockSpec((tm, d), lambda j: (j, 0)), seg.mod_spec(tm, d, 1), _full(g, 1)]
                 + [_full(w, 1) for w in ws],
        out_specs=[pl.BlockSpec((tm, w.shape[1]), lambda j: (j, 0)) for w in ws],
        compiler_params=_cp("arbitrary"),
        name="hgrn_proj_ctx",
    )(x, mod, g, *ws)


def _hgrn_dir(q_ref, v_ref, f_ref, bias_ref, lb_ref, h_ref, s_scr, reverse, d):
    L = CHUNK
    lb = lb_ref[d:d + 1, :]
    f = lb + (1.0 - lb) * _sigmoid(f_ref[...].astype(F32) + bias_ref[d:d + 1, :])
    a_all = _cumsum_time(jnp.log(f), _cumsum_mat(L, reverse))
    r = lax.broadcasted_iota(jnp.int32, (L, L), 0)
    c = lax.broadcasted_iota(jnp.int32, (L, L), 1)
    causal = (c >= r) if reverse else (c <= r)
    same64 = (r // 64) == (c // 64)
    diag32 = ((r // 32) == (c // 32)) & causal
    t = lax.broadcasted_iota(jnp.int32, (L, 1), 0)
    if reverse:
        late0, late1 = t < 64, (t % 64) < 32
        a0, a1 = (64,), (32, 96)
        a2 = (31, 63, 95, 127)
        last = 0
    else:
        late0, late1 = t >= 64, (t % 64) >= 32
        a0, a1 = (63,), (31, 95)
        a2 = (0, 32, 64, 96)
        last = L - 1
    for h in range(H_HEADS):
        sl = slice(h * H_DK, (h + 1) * H_DK)
        a = a_all[:, sl]
        q = q_ref[:, sl].astype(F32)
        k = 1.0 - f[:, sl]
        v = v_ref[:, h * H_DV:(h + 1) * H_DV]
        anc0 = jnp.broadcast_to(a[a0[0]:a0[0] + 1, :], (L, H_DK))
        anc1 = jnp.where(t < 64, a[a1[0]:a1[0] + 1, :], a[a1[1]:a1[1] + 1, :])
        anc2 = jnp.where(t < 32, a[a2[0]:a2[0] + 1, :],
                         jnp.where(t < 64, a[a2[1]:a2[1] + 1, :],
                                   jnp.where(t < 96, a[a2[2]:a2[2] + 1, :], a[a2[3]:a2[3] + 1, :])))

        def level(anc, late):
            qs = q * jnp.exp(a - anc)
            ks = k * jnp.exp(anc - a)
            if late is not None:
                qs = jnp.where(late, qs, 0.0)
                ks = jnp.where(late, 0.0, ks)
            return lax.dot_general(qs.astype(BF16), ks.astype(BF16), NT_DIMS, preferred_element_type=F32)

        s = jnp.where(diag32, level(anc2, None), jnp.where(same64, level(anc1, late1), level(anc0, late0)))
        st = s_scr[d, h]
        qd = (q * jnp.exp(a)).astype(BF16)
        o = (jnp.dot(s.astype(BF16), v, preferred_element_type=F32)
             + lax.dot_general(qd, st.astype(BF16), NT_DIMS, preferred_element_type=F32))
        h_ref[:, h * H_DV:(h + 1) * H_DV] = o.astype(BF16)
        a_last = a[last:last + 1, :]
        kd = (k * jnp.exp(a_last - a)).astype(BF16)
        s_scr[d, h] = st * jnp.exp(a_last) + lax.dot_general(v, kd, TN_DIMS, preferred_element_type=F32)


def _hgrn_scan_kernel(qf, vf, ff, qb, vb, fb, bias_ref, lb_ref, s0_ref, hf_ref, hb_ref, st_ref, s_scr):
    n = pl.program_id(1)

    @pl.when(n == 0)
    def _():
        s_scr[...] = s0_ref[0]

    _hgrn_dir(qf, vf, ff, bias_ref, lb_ref, hf_ref, s_scr, False, 0)
    _hgrn_dir(qb, vb, fb, bias_ref, lb_ref, hb_ref, s_scr, True, 1)

    @pl.when(n == pl.num_programs(1) - 1)
    def _():
        st_ref[0] = s_scr[...]


def _hgrn_scan(q, v, f_f, f_b, bias, lb, s0, seg):
    L = CHUNK
    assert seg.seqlen % L == 0
    n = seg.seqlen // L
    fwd = lambda b, i: (b * n + i, 0)
    bwd = lambda b, i: (b * n + (n - 1 - i), 0)
    rows_, w = q.shape
    st = lambda b, i: (b, 0, 0, 0, 0)
    return pl.pallas_call(
        _hgrn_scan_kernel,
        out_shape=[jax.ShapeDtypeStruct((rows_, w), BF16), jax.ShapeDtypeStruct((rows_, w), BF16),
                   jax.ShapeDtypeStruct(s0.shape, F32)],
        grid=(seg.batch, n),
        in_specs=[pl.BlockSpec((L, w), fwd), pl.BlockSpec((L, w), fwd), pl.BlockSpec((L, w), fwd),
                  pl.BlockSpec((L, w), bwd), pl.BlockSpec((L, w), bwd), pl.BlockSpec((L, w), bwd),
                  _full(bias, 2), _full(lb, 2), pl.BlockSpec((1,) + s0.shape[1:], st)],
        out_specs=[pl.BlockSpec((L, w), fwd), pl.BlockSpec((L, w), bwd),
                   pl.BlockSpec((1,) + s0.shape[1:], st)],
        scratch_shapes=[pltpu.VMEM(s0.shape[1:], F32)],
        compiler_params=_cp("arbitrary", "arbitrary"),
        name="hgrn_scan",
    )(q, v, f_f, q, v, f_b, bias, lb, s0)


def _mix_out_col_kernel(hf_ref, hb_ref, o_ref, x_ref, mod_ref, gh_ref, w_ref, g1_ref, out_ref,
                        *, n_heads, rows_per_col):
    mod = mod_ref[0]
    y = _head_out(hf_ref[...], hb_ref[...], o_ref[...], gh_ref[...], w_ref[...], n_heads)
    upd = mod[2:3] * (_rms(y) * g1_ref[...])
    for cl in range(SUBLANES):
        out_ref[:, cl, :] = x_ref[:, cl, :] + upd[cl * rows_per_col:(cl + 1) * rows_per_col, :]


def _mix_out_col(hf, hb, o, x, seg, mod, gh, w, g1, n_heads):
    n, d = x.shape
    grid_rows = seg.seqlen // GRID_W
    tcol = SUBLANES * grid_rows
    ncb = GRID_W // SUBLANES
    x3 = x.reshape(n // GRID_W, GRID_W, d)
    hmap = lambda b, c: (b * ncb + c, 0)
    xmap = lambda b, c: (b, c, 0)
    consts = (gh, w, g1)
    out = pl.pallas_call(
        functools.partial(_mix_out_col_kernel, n_heads=n_heads, rows_per_col=grid_rows),
        out_shape=jax.ShapeDtypeStruct(x3.shape, F32),
        grid=(seg.batch, ncb),
        in_specs=[pl.BlockSpec((tcol, hf.shape[1]), hmap), pl.BlockSpec((tcol, hf.shape[1]), hmap),
                  pl.BlockSpec((tcol, o.shape[1]), hmap),
                  pl.BlockSpec((grid_rows, SUBLANES, d), xmap),
                  pl.BlockSpec((1, N_MOD, d), lambda b, c: (b, 0, 0))] + [_full(a, 2) for a in consts],
        out_specs=pl.BlockSpec((grid_rows, SUBLANES, d), xmap),
        input_output_aliases={3: 0},
        compiler_params=_cp("arbitrary", "arbitrary"),
        name="mix_out_col",
    )(hf, hb, o, x3, mod, *consts)
    return out.reshape(n, d)


def kernel(x, c, ctx, c_ctx, mod_w, mod_b, norm_g, m_w_in, m_b_gate, m_w_conv, m_g_head, m_w_out,
           h_w_in, h_b_f, h_lb_raw, h_g_head, h_w_out, f_w_gu, f_w_down,
           e_w_router, e_b_router, e_w_gu, e_w_down):
    batch, seq, d = x.shape
    ctx_len = ctx.shape[1]
    depth = mod_w.shape[0]
    assert seq % GRID_W == 0 and seq // GRID_W == CHUNK and ctx_len % CHUNK == 0
    assert batch < MOD_ROWS and d % LANES == 0
    lat_seg = _Seg(batch, seq, False)
    ctx_seg = _Seg(batch, ctx_len, True)

    c_all = jnp.zeros((MOD_ROWS, d), F32).at[:batch].set(c).at[batch].set(c_ctx)
    mod_all = _modulation(c_all, mod_w, mod_b).reshape(depth, MOD_ROWS, N_MOD, d)

    lb_all = jax.nn.softmax(h_lb_raw.astype(F32), axis=0)
    lb_all = jnp.cumsum(lb_all, axis=0) - lb_all[0]

    xl = x.reshape(batch * seq, d)
    xc = ctx.reshape(batch * ctx_len, d)
    row = lambda v: v.reshape(1, -1).astype(F32)

    for i in range(depth):
        j = i // 2
        with_ctx = i < depth - 1
        mod = mod_all[i]
        g0, g1, g2, g3 = (row(norm_g[i, s]) for s in range(4))
        if i % 2 == 0:
            qk_w = 2 * M_HEADS * M_DK
            v_w = M_HEADS * M_DV
            w_in = m_w_in[j]
            wqk = w_in[:, :qk_w].astype(BF16)
            wv = w_in[:, qk_w:qk_w + v_w].astype(BF16)
            wo = w_in[:, qk_w + v_w:qk_w + 2 * v_w].astype(BF16)
            ng = 4 * M_HEADS
            wg = jnp.zeros((d, LANES), F32).at[:, :ng].set(w_in[:, qk_w + 2 * v_w:]).astype(BF16)
            gb = jnp.zeros((1, LANES), F32).at[0, :ng].set(m_b_gate[j].reshape(-1))
            col = jnp.arange(LANES)
            gm = (((col // M_HEADS) % 2 == 1) & (col < ng)).astype(F32).reshape(1, LANES)
            pw = (mod, g0, wqk, wv, wo, wg, m_w_conv[j].astype(F32), gb, gm)
            qc, kc, vc, oc, gc = _proj_m(xc, ctx_seg, *pw)
            ql, kl, vl, ol, gl = _proj_m(xl, lat_seg, *pw)
            s0 = jnp.zeros((batch, 2, M_HEADS // 2, LANES, 2 * M_DV), F32)
            m0 = jnp.zeros((batch, 2 * M_HEADS, LANES), F32)
            hcf, hcb, s1, m1 = _mlstm_scan(qc, kc, vc, gc, s0, m0, ctx_seg)
            hlf, hlb, _, _ = _mlstm_scan(ql, kl, vl, gl, s1, m1, lat_seg)
            gh = row(m_g_head[j])
            wout = m_w_out[j].astype(BF16)
            xl = _mix_out(hlf, hlb, ol, xl, lat_seg, mod, gh, wout, g1, M_HEADS)
            if with_ctx:
                xc = _mix_out(hcf, hcb, oc, xc, ctx_seg, mod, gh, wout, g1, M_HEADS)
        else:
            kw = H_HEADS * H_DK
            w_in = h_w_in[j].astype(BF16)
            ws = [w_in[:, s * kw:(s + 1) * kw] for s in range(5)]
            qc, vc, ffc, fbc, gc = _proj_h_ctx(xc, ctx_seg, mod, g0, ws)
            ql, vl, ffl, fbl, gl = _proj_h_lat(xl, lat_seg, mod, g0, ws)
            bias = h_b_f[j].astype(F32)
            lb = lb_all[j]
            s0 = jnp.zeros((batch, 2, H_HEADS, H_DV, H_DK), F32)
            hcf, hcb, s1 = _hgrn_scan(qc, vc, ffc, fbc, bias, lb, s0, ctx_seg)
            hlf, hlb, _ = _hgrn_scan(ql, vl, ffl, fbl, bias, lb, s1, lat_seg)
            gh = row(h_g_head[j])
            wout = h_w_out[j].astype(BF16)
            xl = _mix_out_col(hlf, hlb, gl, xl, lat_seg, mod, gh, wout, g1, H_HEADS)
            if with_ctx:
                xc = _mix_out(hcf, hcb, gc, xc, ctx_seg, mod, gh, wout, g1, H_HEADS)
        streams = [(xl, lat_seg)] + ([(xc, ctx_seg)] if with_ctx else [])
        outs = []
        if i % 2 == 0:
            wgu, wd = f_w_gu[j].astype(BF16), f_w_down[j].astype(BF16)
            for xs, seg in streams:
                outs.append(_ffn(xs, seg, mod, g2, wgu, wd, g3, th=256))
        else:
            wr = jnp.zeros((d, LANES), F32).at[:, :N_EXPERTS].set(e_w_router[j])
            br = jnp.zeros((1, LANES), F32).at[0, :N_EXPERTS].set(e_b_router[j])
            wgu, wd = e_w_gu[j].astype(BF16), e_w_down[j].astype(BF16)
            for xs, seg in streams:
                u, comb = _router(xs, seg, mod, g2, wr, br)
                outs.append(_moe(u, comb, xs, seg, mod, wgu, wd, g3, th=256))
        xl = outs[0]
        if with_ctx:
            xc = outs[1]
    return xl.reshape(batch, seq, d)
```

```python
import functools

import jax
import jax.numpy as jnp
from jax import lax
from jax.experimental import pallas as pl
from jax.experimental.pallas import tpu as pltpu

F32 = jnp.float32
BF16 = jnp.bfloat16
EPS = 1e-6
NEG = -1e30

GRID_W = 64
N_MOD = 6
M_HEADS, M_DK, M_DV = 8, 64, 128
H_HEADS, H_DK, H_DV = 8, 128, 128
N_EXPERTS = 8
LANES = 128
SUBLANES = 8
CHUNK = 128
MOD_ROWS = 16
VMEM_LIMIT = 56 * 1024 * 1024

HI = lax.Precision.HIGHEST
NT_DIMS = (((1,), (1,)), ((), ()))
TN_DIMS = (((0,), (0,)), ((), ()))


def _cp(*sem):
    return pltpu.CompilerParams(dimension_semantics=sem, vmem_limit_bytes=VMEM_LIMIT)


def _full(a, nargs):
    zeros = (0,) * a.ndim
    return pl.BlockSpec(a.shape, lambda *_: zeros)


def _sigmoid(x):
    return 1.0 / (1.0 + jnp.exp(-x))


def _silu(x):
    return x * _sigmoid(x)


def _log_sigmoid(x):
    return jnp.minimum(x, 0.0) - jnp.log(1.0 + jnp.exp(-jnp.abs(x)))


def _rms(x):
    return x * lax.rsqrt(jnp.mean(x * x, axis=-1, keepdims=True) + EPS)


def _normmod(x, g, mod, s):
    return _rms(x) * (g * (1.0 + mod[s + 1:s + 2])) + mod[s:s + 1]


def _split2(x):
    hi = x.astype(BF16)
    lo = (x - hi.astype(F32)).astype(BF16)
    return hi, lo


def _cumsum_mat(n, reverse):
    r = lax.broadcasted_iota(jnp.int32, (n, n), 0)
    c = lax.broadcasted_iota(jnp.int32, (n, n), 1)
    return jnp.where((c >= r) if reverse else (c <= r), 1.0, 0.0).astype(BF16)


def _cumsum_time(x, tri):
    hi, lo = _split2(x)
    return (jnp.dot(tri, hi, preferred_element_type=F32)
            + jnp.dot(tri, lo, preferred_element_type=F32))


class _Seg:
    def __init__(self, batch, seqlen, is_ctx):
        self.batch, self.seqlen, self.is_ctx = batch, seqlen, is_ctx
        self.n = batch * seqlen

    def tile(self, pref):
        tm = min(pref, self.n)
        assert self.n % tm == 0 and (self.seqlen % tm == 0 or tm % self.seqlen == 0)
        return tm

    def mod_spec(self, tm, d, nargs):
        if self.is_ctx:
            row = lambda j: self.batch
        else:
            row = lambda j: (j * tm) // self.seqlen
        if nargs == 1:
            return pl.BlockSpec((1, N_MOD, d), lambda j: (row(j), 0, 0))
        if nargs == 2:
            return pl.BlockSpec((1, N_MOD, d), lambda j, k: (row(j), 0, 0))
        return pl.BlockSpec((1, N_MOD, d), lambda j, e, k: (row(j), 0, 0))


def _mod_kernel(c_ref, w_ref, b_ref, o_ref):
    c = c_ref[...]
    o_ref[0] = jnp.dot(_silu(c), w_ref[0], preferred_element_type=F32, precision=HI) + b_ref[0]


def _modulation(c_all, mod_w, mod_b):
    depth, d, n = mod_w.shape
    tn = n // 4
    return pl.pallas_call(
        _mod_kernel,
        out_shape=jax.ShapeDtypeStruct((depth, MOD_ROWS, n), F32),
        grid=(depth, n // tn),
        in_specs=[pl.BlockSpec((MOD_ROWS, d), lambda i, j: (0, 0)),
                  pl.BlockSpec((1, d, tn), lambda i, j: (i, 0, j)),
                  pl.BlockSpec((1, 1, tn), lambda i, j: (i, 0, j))],
        out_specs=pl.BlockSpec((1, MOD_ROWS, tn), lambda i, j: (i, 0, j)),
        compiler_params=_cp("arbitrary", "arbitrary"),
        name="modulation",
    )(c_all, mod_w, mod_b.reshape(depth, 1, n))


def _proj_m_kernel(xp_ref, x_ref, xn_ref, mod_ref, g_ref, wqk_ref, wv_ref, wo_ref, wg_ref,
                   cw_ref, gb_ref, gm_ref, q_ref, k_ref, v_ref, o_ref, gt_ref, *, tm, seqlen):
    j = pl.program_id(0)
    mod = mod_ref[0]
    g = g_ref[...]
    u = _normmod(x_ref[...], g, mod, 0).astype(BF16)
    uh = _normmod(jnp.concatenate([xp_ref[...], xn_ref[...]], axis=0), g, mod, 0).astype(BF16)
    wqk = wqk_ref[...]
    z = jnp.dot(u, wqk, preferred_element_type=F32)
    zh = jnp.dot(uh, wqk, preferred_element_type=F32)
    local = lax.broadcasted_iota(jnp.int32, (tm, 1), 0)
    pos = lax.rem(j * tm + local, seqlen)
    zp = jnp.where(local == 0, zh[SUBLANES - 1:SUBLANES], pltpu.roll(z, 1, 0))
    zn = jnp.where(local == tm - 1, zh[SUBLANES:SUBLANES + 1], pltpu.roll(z, tm - 1, 0))
    zp = jnp.where(pos == 0, 0.0, zp)
    zn = jnp.where(pos == seqlen - 1, 0.0, zn)
    cw = cw_ref[...]
    a = _silu(cw[0:1] * zp + cw[1:2] * z + cw[2:3] * zn)
    half = a.shape[1] // 2
    q_ref[...] = a[:, :half].astype(BF16)
    k_ref[...] = (a[:, half:] * (M_DK ** -0.5)).astype(BF16)
    v_ref[...] = jnp.dot(u, wv_ref[...], preferred_element_type=F32).astype(BF16)
    o_ref[...] = jnp.dot(u, wo_ref[...], preferred_element_type=F32).astype(BF16)
    zg = jnp.dot(u, wg_ref[...], preferred_element_type=F32) + gb_ref[...]
    gt_ref[...] = jnp.where(gm_ref[...] > 0.5, _log_sigmoid(zg), zg)


def _proj_m(x, seg, mod, g, wqk, wv, wo, wg, cw, gb, gm):
    n, d = x.shape
    tm = seg.tile(512)
    nb8 = n // SUBLANES
    r8 = tm // SUBLANES
    kern = functools.partial(_proj_m_kernel, tm=tm, seqlen=seg.seqlen)
    nqk = wqk.shape[1]
    consts = (g, wqk, wv, wo, wg, cw, gb, gm)
    return pl.pallas_call(
        kern,
        out_shape=[jax.ShapeDtypeStruct((n, nqk // 2), BF16),
                   jax.ShapeDtypeStruct((n, nqk // 2), BF16),
                   jax.ShapeDtypeStruct((n, wv.shape[1]), BF16),
                   jax.ShapeDtypeStruct((n, wo.shape[1]), BF16),
                   jax.ShapeDtypeStruct((n, LANES), F32)],
        grid=(n // tm,),
        in_specs=[pl.BlockSpec((SUBLANES, d), lambda j: (jnp.maximum(j * r8 - 1, 0), 0)),
                  pl.BlockSpec((tm, d), lambda j: (j, 0)),
                  pl.BlockSpec((SUBLANES, d), lambda j: (jnp.minimum((j + 1) * r8, nb8 - 1), 0)),
                  seg.mod_spec(tm, d, 1)] + [_full(a, 1) for a in consts],
        out_specs=[pl.BlockSpec((tm, nqk // 2), lambda j: (j, 0)),
                   pl.BlockSpec((tm, nqk // 2), lambda j: (j, 0)),
                   pl.BlockSpec((tm, wv.shape[1]), lambda j: (j, 0)),
                   pl.BlockSpec((tm, wo.shape[1]), lambda j: (j, 0)),
                   pl.BlockSpec((tm, LANES), lambda j: (j, 0))],
        compiler_params=_cp("arbitrary"),
        name="mlstm_proj",
    )(x, x, x, mod, *consts)


def _mlstm_dir(q_ref, k_ref, v_ref, g_ref, h_ref, s_scr, m_scr, reverse, d):
    L = CHUNK
    g = g_ref[...]
    tri = _cumsum_mat(L, reverse)
    b = _cumsum_time(g, tri)
    gT = g.T
    bT = b.T
    r = lax.broadcasted_iota(jnp.int32, (L, L), 0)
    c = lax.broadcasted_iota(jnp.int32, (L, L), 1)
    causal = (c >= r) if reverse else (c <= r)
    last = 0 if reverse else L - 1
    c_ig = 2 * M_HEADS * d
    c_lf = c_ig + M_HEADS
    lane = lax.broadcasted_iota(jnp.int32, (1, LANES), 1)
    srow = lax.broadcasted_iota(jnp.int32, (LANES, 1), 0)
    ones = jnp.ones((L, M_DV), BF16)
    for p in range(M_HEADS // 2):
        q_p = q_ref[:, p * LANES:(p + 1) * LANES]
        k_p = k_ref[:, p * LANES:(p + 1) * LANES]
        s_pair = s_scr[d, p]
        s_bf = s_pair.astype(BF16)
        dsum = None
        decs = []
        for e in range(2):
            h = 2 * p + e
            in_head = (lane // M_DK) == e
            k_e = jnp.where(in_head, k_p, jnp.zeros_like(k_p))
            q_e = jnp.where(in_head, q_p, jnp.zeros_like(q_p))
            ig_col = g[:, c_ig + h:c_ig + h + 1]
            b_col = b[:, c_lf + h:c_lf + h + 1]
            ig_row = gT[c_ig + h:c_ig + h + 1, :]
            b_row = bT[c_lf + h:c_lf + h + 1, :]
            m_prev = m_scr[d * M_HEADS + h:d * M_HEADS + h + 1, 0:1]
            b_last = b_col[last:last + 1, :]
            dmat = jnp.where(causal, b_col - b_row + ig_row, NEG)
            inter = b_col + m_prev
            m_t = jnp.maximum(inter, jnp.max(dmat, axis=1, keepdims=True))
            qk = lax.dot_general(q_p, k_e, NT_DIMS, preferred_element_type=F32)
            s = (qk * jnp.exp(dmat - m_t)).astype(BF16)
            vp = jnp.concatenate([v_ref[:, h * M_DV:(h + 1) * M_DV], ones], axis=1)
            tot = (jnp.dot(s, vp, preferred_element_type=F32)
                   + jnp.exp(inter - m_t) * jnp.dot(q_e, s_bf, preferred_element_type=F32))
            den = jnp.maximum(jnp.abs(tot[:, M_DV:]), jnp.exp(-m_t))
            h_ref[:, h * M_DV:(h + 1) * M_DV] = (tot[:, :M_DV] / den).astype(BF16)
            g_col = b_last - b_col + ig_col
            g_row = b_last - b_row + ig_row
            m_new = jnp.maximum(b_last + m_prev, jnp.max(g_row, axis=1, keepdims=True))
            kw = (k_e.astype(F32) * jnp.exp(g_col - m_new)).astype(BF16)
            ds = lax.dot_general(kw, vp, TN_DIMS, preferred_element_type=F32)
            dsum = ds if dsum is None else dsum + ds
            decs.append(jnp.exp(b_last + m_prev - m_new))
            m_scr[d * M_HEADS + h:d * M_HEADS + h + 1, :] = jnp.broadcast_to(m_new, (1, LANES))
        dec = jnp.where(srow < M_DK, decs[0], decs[1])
        s_scr[d, p] = dec * s_pair + dsum


def _mlstm_scan_kernel(qf, kf, vf, gf, qb, kb, vb, gb, s0_ref, m0_ref,
                       hf_ref, hb_ref, st_ref, mt_ref, s_scr, m_scr):
    n = pl.program_id(1)

    @pl.when(n == 0)
    def _():
        s_scr[...] = s0_ref[0]
        m_scr[...] = m0_ref[0]

    _mlstm_dir(qf, kf, vf, gf, hf_ref, s_scr, m_scr, False, 0)
    _mlstm_dir(qb, kb, vb, gb, hb_ref, s_scr, m_scr, True, 1)

    @pl.when(n == pl.num_programs(1) - 1)
    def _():
        st_ref[0] = s_scr[...]
        mt_ref[0] = m_scr[...]


def _mlstm_scan(q, k, v, gates, s0, m0, seg):
    L = CHUNK
    assert seg.seqlen % L == 0
    n = seg.seqlen // L
    fwd = lambda b, i: (b * n + i, 0)
    bwd = lambda b, i: (b * n + (n - 1 - i), 0)
    dq, dv = q.shape[1], v.shape[1]
    st = lambda b, i: (b, 0, 0, 0, 0)
    mt = lambda b, i: (b, 0, 0)
    return pl.pallas_call(
        _mlstm_scan_kernel,
        out_shape=[jax.ShapeDtypeStruct((seg.n, dv), BF16), jax.ShapeDtypeStruct((seg.n, dv), BF16),
                   jax.ShapeDtypeStruct(s0.shape, F32), jax.ShapeDtypeStruct(m0.shape, F32)],
        grid=(seg.batch, n),
        in_specs=[pl.BlockSpec((L, dq), fwd), pl.BlockSpec((L, dq), fwd), pl.BlockSpec((L, dv), fwd),
                  pl.BlockSpec((L, LANES), fwd),
                  pl.BlockSpec((L, dq), bwd), pl.BlockSpec((L, dq), bwd), pl.BlockSpec((L, dv), bwd),
                  pl.BlockSpec((L, LANES), bwd),
                  pl.BlockSpec((1,) + s0.shape[1:], st), pl.BlockSpec((1,) + m0.shape[1:], mt)],
        out_specs=[pl.BlockSpec((L, dv), fwd), pl.BlockSpec((L, dv), bwd),
                   pl.BlockSpec((1,) + s0.shape[1:], st), pl.BlockSpec((1,) + m0.shape[1:], mt)],
        scratch_shapes=[pltpu.VMEM(s0.shape[1:], F32), pltpu.VMEM(m0.shape[1:], F32)],
        compiler_params=_cp("arbitrary", "arbitrary"),
        name="mlstm_scan",
    )(q, k, v, gates, q, k, v, gates, s0, m0)


def _head_out(hf, hb, gate_pre, gh, w, n_heads):
    h = hf.astype(F32) + hb.astype(F32)
    dv = h.shape[1] // n_heads
    parts = [_rms(h[:, i * dv:(i + 1) * dv]) for i in range(n_heads)]
    hn = jnp.concatenate(parts, axis=1) * gh * _sigmoid(gate_pre.astype(F32))
    return jnp.dot(hn.astype(BF16), w, preferred_element_type=F32)


def _mix_out_kernel(hf_ref, hb_ref, o_ref, x_ref, mod_ref, gh_ref, w_ref, g1_ref, out_ref, *, n_heads):
    mod = mod_ref[0]
    y = _head_out(hf_ref[...], hb_ref[...], o_ref[...], gh_ref[...], w_ref[...], n_heads)
    out_ref[...] = x_ref[...] + mod[2:3] * (_rms(y) * g1_ref[...])


def _mix_out(hf, hb, o, x, seg, mod, gh, w, g1, n_heads):
    n, d = x.shape
    tm = seg.tile(512)
    rmap = lambda j: (j, 0)
    consts = (gh, w, g1)
    return pl.pallas_call(
        functools.partial(_mix_out_kernel, n_heads=n_heads),
        out_shape=jax.ShapeDtypeStruct(x.shape, F32),
        grid=(n // tm,),
        in_specs=[pl.BlockSpec((tm, hf.shape[1]), rmap), pl.BlockSpec((tm, hf.shape[1]), rmap),
                  pl.BlockSpec((tm, o.shape[1]), rmap), pl.BlockSpec((tm, d), rmap),
                  seg.mod_spec(tm, d, 1)] + [_full(a, 1) for a in consts],
        out_specs=pl.BlockSpec((tm, d), rmap),
        input_output_aliases={3: 0},
        compiler_params=_cp("arbitrary"),
        name="mix_out",
    )(hf, hb, o, x, mod, *consts)


def _ffn_kernel(x_ref, mod_ref, g2_ref, wg_ref, wu_ref, wd_ref, g3_ref, out_ref, u_scr, acc_scr):
    kk = pl.program_id(1)

    @pl.when(kk == 0)
    def _():
        u_scr[...] = _normmod(x_ref[...], g2_ref[...], mod_ref[0], 3).astype(BF16)
        acc_scr[...] = jnp.zeros_like(acc_scr)

    u = u_scr[...]
    gg = jnp.dot(u, wg_ref[...], preferred_element_type=F32)
    up = jnp.dot(u, wu_ref[...], preferred_element_type=F32)
    hh = (_silu(gg) * up).astype(BF16)
    acc_scr[...] += jnp.dot(hh, wd_ref[...], preferred_element_type=F32)

    @pl.when(kk == pl.num_programs(1) - 1)
    def _():
        mod = mod_ref[0]
        out_ref[...] = x_ref[...] + mod[5:6] * (_rms(acc_scr[...]) * g3_ref[...])


def _ffn(x, seg, mod, g2, wgu, wd, g3, th):
    n, d = x.shape
    tm = seg.tile(1024)
    hd = wd.shape[0]
    assert hd % th == 0
    nk = hd // th
    return pl.pallas_call(
        _ffn_kernel,
        out_shape=jax.ShapeDtypeStruct((n, d), F32),
        grid=(n // tm, nk),
        in_specs=[pl.BlockSpec((tm, d), lambda i, k: (i, 0)),
                  seg.mod_spec(tm, d, 2),
                  _full(g2, 2),
                  pl.BlockSpec((d, th), lambda i, k: (0, k)),
                  pl.BlockSpec((d, th), lambda i, k: (0, nk + k)),
                  pl.BlockSpec((th, d), lambda i, k: (k, 0)),
                  _full(g3, 2)],
        out_specs=pl.BlockSpec((tm, d), lambda i, k: (i, 0)),
        scratch_shapes=[pltpu.VMEM((tm, d), BF16), pltpu.VMEM((tm, d), F32)],
        input_output_aliases={0: 0},
        compiler_params=_cp("arbitrary", "arbitrary"),
        name="ffn",
    )(x, mod, g2, wgu, wgu, wd, g3)


def _router_kernel(x_ref, mod_ref, g2_ref, wr_ref, br_ref, u_ref, comb_ref):
    u = _normmod(x_ref[...], g2_ref[...], mod_ref[0], 3)
    u_ref[...] = u.astype(BF16)
    lane = lax.broadcasted_iota(jnp.int32, (1, LANES), 1)
    logits = jnp.dot(u, wr_ref[...], preferred_element_type=F32, precision=HI) + br_ref[...]
    logits = jnp.where(lane < N_EXPERTS, logits, NEG)
    m1 = jnp.max(logits, axis=1, keepdims=True)
    i1 = jnp.min(jnp.where(logits == m1, lane, LANES), axis=1, keepdims=True)
    rest = jnp.where(lane == i1, NEG, logits)
    m2 = jnp.max(rest, axis=1, keepdims=True)
    i2 = jnp.min(jnp.where(rest == m2, lane, LANES), axis=1, keepdims=True)
    e2 = jnp.exp(m2 - m1)
    w1 = 1.0 / (1.0 + e2)
    comb_ref[...] = jnp.where(lane == i1, w1, 0.0) + jnp.where(lane == i2, e2 * w1, 0.0)


def _router(x, seg, mod, g2, wr, br):
    n, d = x.shape
    tm = seg.tile(1024)
    consts = (g2, wr, br)
    return pl.pallas_call(
        _router_kernel,
        out_shape=[jax.ShapeDtypeStruct((n, d), BF16), jax.ShapeDtypeStruct((n, LANES), F32)],
        grid=(n // tm,),
        in_specs=[pl.BlockSpec((tm, d), lambda i: (i, 0)), seg.mod_spec(tm, d, 1)]
                 + [_full(a, 1) for a in consts],
        out_specs=[pl.BlockSpec((tm, d), lambda i: (i, 0)), pl.BlockSpec((tm, LANES), lambda i: (i, 0))],
        compiler_params=_cp("arbitrary"),
        name="router",
    )(x, mod, *consts)


def _moe_kernel(u_ref, comb_ref, x_ref, mod_ref, wg_ref, wu_ref, wd_ref, g3_ref, out_ref, acc_scr):
    e = pl.program_id(1)
    kk = pl.program_id(2)

    @pl.when((e == 0) & (kk == 0))
    def _():
        acc_scr[...] = jnp.zeros_like(acc_scr)

    u = u_ref[...]
    gg = jnp.dot(u, wg_ref[0], preferred_element_type=F32)
    up = jnp.dot(u, wu_ref[0], preferred_element_type=F32)
    hh = (_silu(gg) * up).astype(BF16)
    lane = lax.broadcasted_iota(jnp.int32, (1, LANES), 1)
    cw = jnp.sum(jnp.where(lane == e, comb_ref[...], 0.0), axis=1, keepdims=True)
    acc_scr[...] += cw * jnp.dot(hh, wd_ref[0], preferred_element_type=F32)

    @pl.when((e == pl.num_programs(1) - 1) & (kk == pl.num_programs(2) - 1))
    def _():
        mod = mod_ref[0]
        out_ref[...] = x_ref[...] + mod[5:6] * (_rms(acc_scr[...]) * g3_ref[...])


def _moe(u, comb, x, seg, mod, wgu, wd, g3, th):
    n, d = x.shape
    tm = seg.tile(1024)
    ne, hd, _ = wd.shape
    nk = hd // th
    return pl.pallas_call(
        _moe_kernel,
        out_shape=jax.ShapeDtypeStruct((n, d), F32),
        grid=(n // tm, ne, nk),
        in_specs=[pl.BlockSpec((tm, d), lambda i, e, k: (i, 0)),
                  pl.BlockSpec((tm, LANES), lambda i, e, k: (i, 0)),
                  pl.BlockSpec((tm, d), lambda i, e, k: (i, 0)),
                  seg.mod_spec(tm, d, 3),
                  pl.BlockSpec((1, d, th), lambda i, e, k: (e, 0, k)),
                  pl.BlockSpec((1, d, th), lambda i, e, k: (e, 0, nk + k)),
                  pl.BlockSpec((1, th, d), lambda i, e, k: (e, k, 0)),
                  _full(g3, 3)],
        out_specs=pl.BlockSpec((tm, d), lambda i, e, k: (i, 0)),
        scratch_shapes=[pltpu.VMEM((tm, d), F32)],
        input_output_aliases={2: 0},
        compiler_params=_cp("arbitrary", "arbitrary", "arbitrary"),
        name="moe",
    )(u, comb, x, mod, wgu, wgu, wd, g3)


ROW_TILE = 512
COPY_BLOCK = 2048
META_E, META_RANK, META_W = 0, 2, 4


def _route_kernel(x_ref, mod_ref, g2_ref, wr_ref, br_ref, u3_ref, meta_ref, cnt_ref, tri_scr, run_scr):
    i = pl.program_id(0)
    tm = x_ref.shape[0]

    @pl.when(i == 0)
    def _():
        r = lax.broadcasted_iota(jnp.int32, (tm, tm), 0)
        c = lax.broadcasted_iota(jnp.int32, (tm, tm), 1)
        tri_scr[...] = jnp.where(c < r, 1.0, 0.0).astype(BF16)
        run_scr[...] = jnp.zeros_like(run_scr)

    u = _normmod(x_ref[...], g2_ref[...], mod_ref[0], 3)
    for s in range(SUBLANES):
        u3_ref[:, s, :] = u[:, s * LANES:(s + 1) * LANES]
    lane = lax.broadcasted_iota(jnp.int32, (1, LANES), 1)
    logits = jnp.dot(u, wr_ref[...], preferred_element_type=F32, precision=HI) + br_ref[...]
    logits = jnp.where(lane < N_EXPERTS, logits, NEG)
    m1 = jnp.max(logits, axis=1, keepdims=True)
    i1 = jnp.min(jnp.where(logits == m1, lane, LANES), axis=1, keepdims=True)
    rest = jnp.where(lane == i1, NEG, logits)
    m2 = jnp.max(rest, axis=1, keepdims=True)
    i2 = jnp.min(jnp.where(rest == m2, lane, LANES), axis=1, keepdims=True)
    e2 = jnp.exp(m2 - m1)
    w1 = 1.0 / (1.0 + e2)
    sel = ((lane == i1) | (lane == i2))
    onehot = jnp.where(sel, 1.0, 0.0)
    before = jnp.dot(tri_scr[...], onehot.astype(BF16), preferred_element_type=F32) + run_scr[...]
    rank1 = jnp.sum(jnp.where(lane == i1, before, 0.0), axis=1, keepdims=True)
    rank2 = jnp.sum(jnp.where(lane == i2, before, 0.0), axis=1, keepdims=True)
    run_scr[...] = run_scr[...] + jnp.sum(onehot, axis=0, keepdims=True)
    vals = (i1.astype(F32), i2.astype(F32), rank1, rank2, w1, e2 * w1)
    meta = jnp.zeros((tm, LANES), F32)
    for col, val in enumerate(vals):
        meta = jnp.where(lane == col, val, meta)
    meta_ref[...] = meta
    cnt_ref[...] = jnp.broadcast_to(run_scr[...], cnt_ref.shape)


def _route(x, seg, mod, g2, wr, br):
    n, d = x.shape
    tm = seg.tile(1024)
    consts = (g2, wr, br)
    return pl.pallas_call(
        _route_kernel,
        out_shape=[jax.ShapeDtypeStruct((n, SUBLANES, d // SUBLANES), F32),
                   jax.ShapeDtypeStruct((n, LANES), F32),
                   jax.ShapeDtypeStruct((SUBLANES, LANES), F32)],
        grid=(n // tm,),
        in_specs=[pl.BlockSpec((tm, d), lambda i: (i, 0)), seg.mod_spec(tm, d, 1)]
                 + [_full(a, 1) for a in consts],
        out_specs=[pl.BlockSpec((tm, SUBLANES, d // SUBLANES), lambda i: (i, 0, 0)),
                   pl.BlockSpec((tm, LANES), lambda i: (i, 0)),
                   pl.BlockSpec((SUBLANES, LANES), lambda i: (0, 0))],
        scratch_shapes=[pltpu.VMEM((tm, tm), BF16), pltpu.VMEM((1, LANES), F32)],
        compiler_params=_cp("arbitrary"),
        name="route",
    )(x, mod, *consts)


def _row_copy_kernel(sidx_ref, didx_ref, src_ref, dst_ref, sem):
    nb = sidx_ref.shape[2]

    def issue(i, carry):
        pltpu.make_async_copy(src_ref.at[sidx_ref[0, 0, i]], dst_ref.at[didx_ref[0, 0, i]], sem).start()
        return carry

    lax.fori_loop(0, nb, issue, 0, unroll=8)

    def drain(i, carry):
        pltpu.make_async_copy(src_ref.at[0], dst_ref.at[0], sem).wait()
        return carry

    lax.fori_loop(0, nb, drain, 0, unroll=8)


def _row_copy(src3, src_idx, dst_idx, n_dst):
    nb = COPY_BLOCK
    total = src_idx.shape[0]
    assert total % nb == 0
    steps = total // nb
    imap = lambda i: (i, 0, 0)
    return pl.pallas_call(
        _row_copy_kernel,
        out_shape=jax.ShapeDtypeStruct((n_dst,) + src3.shape[1:], src3.dtype),
        grid=(steps,),
        in_specs=[pl.BlockSpec((1, 1, nb), imap, memory_space=pltpu.SMEM),
                  pl.BlockSpec((1, 1, nb), imap, memory_space=pltpu.SMEM),
                  pl.BlockSpec(memory_space=pl.ANY)],
        out_specs=pl.BlockSpec(memory_space=pl.ANY),
        scratch_shapes=[pltpu.SemaphoreType.DMA(())],
        compiler_params=_cp("arbitrary"),
        name="row_copy",
    )(src_idx.reshape(steps, 1, nb), dst_idx.reshape(steps, 1, nb), src3)


def _experts_kernel(te_ref, used_ref, xs_ref, wg_ref, wu_ref, wd_ref, y_ref, u_scr, acc_scr):
    i = pl.program_id(0)
    kk = pl.program_id(1)

    @pl.when(i < used_ref[0])
    def _():
        @pl.when(kk == 0)
        def _():
            for s in range(SUBLANES):
                u_scr[:, s * LANES:(s + 1) * LANES] = xs_ref[:, s, :].astype(BF16)
            acc_scr[...] = jnp.zeros_like(acc_scr)

        u = u_scr[...]
        gg = jnp.dot(u, wg_ref[0], preferred_element_type=F32)
        up = jnp.dot(u, wu_ref[0], preferred_element_type=F32)
        hh = (_silu(gg) * up).astype(BF16)
        acc_scr[...] += jnp.dot(hh, wd_ref[0], preferred_element_type=F32)

        @pl.when(kk == pl.num_programs(1) - 1)
        def _():
            y = acc_scr[...]
            for s in range(SUBLANES):
                y_ref[:, s, :] = y[:, s * LANES:(s + 1) * LANES]


def _experts(xs3, tile_expert, used, wgu, wd, th):
    tr = ROW_TILE
    n_rows = xs3.shape[0]
    ne, hd, d = wd.shape
    nk = hd // th
    tiles = tile_expert.shape[0]
    assert tiles * tr <= n_rows
    row = lambda i, k, te, us: (jnp.minimum(i, us[0] - 1), 0, 0)
    kfix = lambda i, k, us: jnp.where(i < us[0], k, nk - 1)
    grid_spec = pltpu.PrefetchScalarGridSpec(
        num_scalar_prefetch=2,
        grid=(tiles, nk),
        in_specs=[pl.BlockSpec((tr, SUBLANES, LANES), row),
                  pl.BlockSpec((1, d, th), lambda i, k, te, us: (te[i], 0, kfix(i, k, us))),
                  pl.BlockSpec((1, d, th), lambda i, k, te, us: (te[i], 0, nk + kfix(i, k, us))),
                  pl.BlockSpec((1, th, d), lambda i, k, te, us: (te[i], kfix(i, k, us), 0))],
        out_specs=pl.BlockSpec((tr, SUBLANES, LANES), row),
        scratch_shapes=[pltpu.VMEM((tr, d), BF16), pltpu.VMEM((tr, d), F32)],
    )
    return pl.pallas_call(
        _experts_kernel,
        out_shape=jax.ShapeDtypeStruct((tiles * tr, SUBLANES, LANES), F32),
        grid_spec=grid_spec,
        compiler_params=_cp("arbitrary", "arbitrary"),
        name="experts",
    )(tile_expert, used, xs3, wgu, wgu, wd)


def _combine_kernel(ya_ref, yb_ref, meta_ref, x_ref, mod_ref, g3_ref, out_ref):
    meta = meta_ref[...]
    w1 = meta[:, META_W:META_W + 1]
    w2 = meta[:, META_W + 1:META_W + 2]
    ys = [w1 * ya_ref[:, s, :] + w2 * yb_ref[:, s, :] for s in range(SUBLANES)]
    ss = ys[0] * ys[0]
    for y in ys[1:]:
        ss = ss + y * y
    d = x_ref.shape[1]
    r = lax.rsqrt(jnp.sum(ss, axis=1, keepdims=True) / d + EPS)
    mod = mod_ref[0]
    for s, y in enumerate(ys):
        sl = slice(s * LANES, (s + 1) * LANES)
        out_ref[:, sl] = x_ref[:, sl] + mod[5:6, sl] * (y * r * g3_ref[:, sl])


def _combine(yt3, meta, x, seg, mod, g3):
    n, d = x.shape
    tm = seg.tile(512)
    nt = n // tm
    return pl.pallas_call(
        _combine_kernel,
        out_shape=jax.ShapeDtypeStruct((n, d), F32),
        grid=(nt,),
        in_specs=[pl.BlockSpec((tm, SUBLANES, LANES), lambda j: (j, 0, 0)),
                  pl.BlockSpec((tm, SUBLANES, LANES), lambda j: (nt + j, 0, 0)),
                  pl.BlockSpec((tm, LANES), lambda j: (j, 0)),
                  pl.BlockSpec((tm, d), lambda j: (j, 0)),
                  seg.mod_spec(tm, d, 1), _full(g3, 1)],
        out_specs=pl.BlockSpec((tm, d), lambda j: (j, 0)),
        input_output_aliases={3: 0},
        compiler_params=_cp("arbitrary"),
        name="combine",
    )(yt3, yt3, meta, x, mod, g3)


def _moe_routed(x, seg, mod, g2, wr, br, wgu, wd, g3, th):
    n, d = x.shape
    assert d == SUBLANES * LANES
    tr = ROW_TILE
    u3, meta, cnt = _route(x, seg, mod, g2, wr, br)
    counts = cnt[0, :N_EXPERTS].astype(jnp.int32)
    sizes = ((counts + tr - 1) // tr) * tr
    ends = jnp.cumsum(sizes)
    offs = ends - sizes
    mi = meta[:, :META_W].astype(jnp.int32)
    eidx = mi[:, META_E:META_E + 2]
    dest = jnp.sum(jnp.where(eidx[:, :, None] == jnp.arange(N_EXPERTS)[None, None, :],
                             offs[None, None, :], 0), axis=2) + mi[:, META_RANK:META_RANK + 2]
    tiles = (2 * n) // tr + N_EXPERTS
    n_rows = tiles * tr
    used = (ends[-1] // tr).astype(jnp.int32).reshape(1)
    tile_start = jnp.arange(tiles, dtype=jnp.int32) * tr
    tile_expert = jnp.sum(tile_start[:, None] >= ends[None, :], axis=1).astype(jnp.int32)
    last_expert = jnp.sum(jnp.maximum(ends[-1] - tr, 0) >= ends).astype(jnp.int32)
    tile_expert = jnp.where(tile_start < ends[-1], tile_expert, last_expert)
    pad = jnp.arange(N_EXPERTS * tr, dtype=jnp.int32).reshape(N_EXPERTS, tr)
    pad_row = offs[:, None] + counts[:, None] + jnp.arange(tr, dtype=jnp.int32)[None, :]
    pad_dst = jnp.where(pad_row < ends[:, None], pad_row, n_rows + pad).reshape(-1)
    tok = jnp.arange(n, dtype=jnp.int32)
    src_idx = jnp.concatenate([tok, tok, jnp.zeros_like(pad_dst)])
    dst_idx = jnp.concatenate([dest[:, 0], dest[:, 1], pad_dst]).astype(jnp.int32)
    xs3 = _row_copy(u3, src_idx, dst_idx, n_rows + N_EXPERTS * tr)
    y3 = _experts(xs3, tile_expert, used, wgu, wd, th)
    back_src = jnp.concatenate([dest[:, 0], dest[:, 1]]).astype(jnp.int32)
    yt3 = _row_copy(y3, back_src, jnp.arange(2 * n, dtype=jnp.int32), 2 * n)
    return _combine(yt3, meta, x, seg, mod, g3)


def _proj_h_body(u, w_refs, out_refs):
    for w_ref, o_ref in zip(w_refs, out_refs):
        o_ref[...] = jnp.dot(u, w_ref[...], preferred_element_type=F32).astype(BF16)


def _proj_h_lat_kernel(x_ref, mod_ref, g_ref, *refs, rows_per_col):
    n = (len(refs) - 1) // 2
    w_refs, out_refs, u_scr = refs[:n], refs[n:2 * n], refs[2 * n]
    mod = mod_ref[0]
    g = g_ref[...]
    for cl in range(SUBLANES):
        u_scr[cl * rows_per_col:(cl + 1) * rows_per_col, :] = _normmod(x_ref[:, cl, :], g, mod, 0).astype(BF16)
    _proj_h_body(u_scr[...], w_refs, out_refs)


def _proj_h_ctx_kernel(x_ref, mod_ref, g_ref, *refs):
    n = len(refs) // 2
    u = _normmod(x_ref[...], g_ref[...], mod_ref[0], 0).astype(BF16)
    _proj_h_body(u, refs[:n], refs[n:])


def _proj_h_lat(x, seg, mod, g, ws):
    n, d = x.shape
    grid_rows = seg.seqlen // GRID_W
    x3 = x.reshape(n // GRID_W, GRID_W, d)
    tcol = SUBLANES * grid_rows
    ncb = GRID_W // SUBLANES
    return pl.pallas_call(
        functools.partial(_proj_h_lat_kernel, rows_per_col=grid_rows),
        out_shape=[jax.ShapeDtypeStruct((n, w.shape[1]), BF16) for w in ws],
        grid=(seg.batch, ncb),
        in_specs=[pl.BlockSpec((grid_rows, SUBLANES, d), lambda b, c: (b, c, 0)),
                  pl.BlockSpec((1, N_MOD, d), lambda b, c: (b, 0, 0)), _full(g, 2)]
                 + [_full(w, 2) for w in ws],
        out_specs=[pl.BlockSpec((tcol, w.shape[1]), lambda b, c: (b * ncb + c, 0)) for w in ws],
        scratch_shapes=[pltpu.VMEM((tcol, d), BF16)],
        compiler_params=_cp("arbitrary", "arbitrary"),
        name="hgrn_proj_lat",
    )(x3, mod, g, *ws)


def _proj_h_ctx(x, seg, mod, g, ws):
    n, d = x.shape
    tm = seg.tile(512)
    return pl.pallas_call(
        _proj_h_ctx_kernel,
        out_shape=[jax.ShapeDtypeStruct((n, w.shape[1]), BF16) for w in ws],
        grid=(n // tm,),
        in_specs=[pl.BlockSpec((tm, d), lambda j: (j, 0)), seg.mod_spec(tm, d, 1), _full(g, 1)]
                 + [_full(w, 1) for w in ws],
        out_specs=[pl.BlockSpec((tm, w.shape[1]), lambda j: (j, 0)) for w in ws],
        compiler_params=_cp("arbitrary"),
        name="hgrn_proj_ctx",
    )(x, mod, g, *ws)


def _hgrn_dir(q_ref, v_ref, f_ref, bias_ref, lb_ref, h_ref, s_scr, reverse, d):
    L = CHUNK
    lb = lb_ref[d:d + 1, :]
    f = lb + (1.0 - lb) * _sigmoid(f_ref[...].astype(F32) + bias_ref[d:d + 1, :])
    a_all = _cumsum_time(jnp.log(f), _cumsum_mat(L, reverse))
    r = lax.broadcasted_iota(jnp.int32, (L, L), 0)
    c = lax.broadcasted_iota(jnp.int32, (L, L), 1)
    causal = (c >= r) if reverse else (c <= r)
    same64 = (r // 64) == (c // 64)
    diag32 = ((r // 32) == (c // 32)) & causal
    t = lax.broadcasted_iota(jnp.int32, (L, 1), 0)
    if reverse:
        late0, late1 = t < 64, (t % 64) < 32
        a0, a1 = (64,), (32, 96)
        a2 = (31, 63, 95, 127)
        last = 0
    else:
        late0, late1 = t >= 64, (t % 64) >= 32
        a0, a1 = (63,), (31, 95)
        a2 = (0, 32, 64, 96)
        last = L - 1
    for h in range(H_HEADS):
        sl = slice(h * H_DK, (h + 1) * H_DK)
        a = a_all[:, sl]
        q = q_ref[:, sl].astype(F32)
        k = 1.0 - f[:, sl]
        v = v_ref[:, h * H_DV:(h + 1) * H_DV]
        anc0 = jnp.broadcast_to(a[a0[0]:a0[0] + 1, :], (L, H_DK))
        anc1 = jnp.where(t < 64, a[a1[0]:a1[0] + 1, :], a[a1[1]:a1[1] + 1, :])
        anc2 = jnp.where(t < 32, a[a2[0]:a2[0] + 1, :],
                         jnp.where(t < 64, a[a2[1]:a2[1] + 1, :],
                                   jnp.where(t < 96, a[a2[2]:a2[2] + 1, :], a[a2[3]:a2[3] + 1, :])))

        def level(anc, late):
            qs = q * jnp.exp(a - anc)
            ks = k * jnp.exp(anc - a)
            if late is not None:
                qs = jnp.where(late, qs, 0.0)
                ks = jnp.where(late, 0.0, ks)
            return lax.dot_general(qs.astype(BF16), ks.astype(BF16), NT_DIMS, preferred_element_type=F32)

        s = jnp.where(diag32, level(anc2, None), jnp.where(same64, level(anc1, late1), level(anc0, late0)))
        st = s_scr[d, h]
        qd = (q * jnp.exp(a)).astype(BF16)
        o = (jnp.dot(s.astype(BF16), v, preferred_element_type=F32)
             + lax.dot_general(qd, st.astype(BF16), NT_DIMS, preferred_element_type=F32))
        h_ref[:, h * H_DV:(h + 1) * H_DV] = o.astype(BF16)
        a_last = a[last:last + 1, :]
        kd = (k * jnp.exp(a_last - a)).astype(BF16)
        s_scr[d, h] = st * jnp.exp(a_last) + lax.dot_general(v, kd, TN_DIMS, preferred_element_type=F32)


def _hgrn_scan_kernel(qf, vf, ff, qb, vb, fb, bias_ref, lb_ref, s0_ref, hf_ref, hb_ref, st_ref, s_scr):
    n = pl.program_id(1)

    @pl.when(n == 0)
    def _():
        s_scr[...] = s0_ref[0]

    _hgrn_dir(qf, vf, ff, bias_ref, lb_ref, hf_ref, s_scr, False, 0)
    _hgrn_dir(qb, vb, fb, bias_ref, lb_ref, hb_ref, s_scr, True, 1)

    @pl.when(n == pl.num_programs(1) - 1)
    def _():
        st_ref[0] = s_scr[...]


def _hgrn_scan(q, v, f_f, f_b, bias, lb, s0, seg):
    L = CHUNK
    assert seg.seqlen % L == 0
    n = seg.seqlen // L
    fwd = lambda b, i: (b * n + i, 0)
    bwd = lambda b, i: (b * n + (n - 1 - i), 0)
    rows_, w = q.shape
    st = lambda b, i: (b, 0, 0, 0, 0)
    return pl.pallas_call(
        _hgrn_scan_kernel,
        out_shape=[jax.ShapeDtypeStruct((rows_, w), BF16), jax.ShapeDtypeStruct((rows_, w), BF16),
                   jax.ShapeDtypeStruct(s0.shape, F32)],
        grid=(seg.batch, n),
        in_specs=[pl.BlockSpec((L, w), fwd), pl.BlockSpec((L, w), fwd), pl.BlockSpec((L, w), fwd),
                  pl.BlockSpec((L, w), bwd), pl.BlockSpec((L, w), bwd), pl.BlockSpec((L, w), bwd),
                  _full(bias, 2), _full(lb, 2), pl.BlockSpec((1,) + s0.shape[1:], st)],
        out_specs=[pl.BlockSpec((L, w), fwd), pl.BlockSpec((L, w), bwd),
                   pl.BlockSpec((1,) + s0.shape[1:], st)],
        scratch_shapes=[pltpu.VMEM(s0.shape[1:], F32)],
        compiler_params=_cp("arbitrary", "arbitrary"),
        name="hgrn_scan",
    )(q, v, f_f, q, v, f_b, bias, lb, s0)


def _mix_out_col_kernel(hf_ref, hb_ref, o_ref, x_ref, mod_ref, gh_ref, w_ref, g1_ref, out_ref,
                        *, n_heads, rows_per_col):
    mod = mod_ref[0]
    y = _head_out(hf_ref[...], hb_ref[...], o_ref[...], gh_ref[...], w_ref[...], n_heads)
    upd = mod[2:3] * (_rms(y) * g1_ref[...])
    for cl in range(SUBLANES):
        out_ref[:, cl, :] = x_ref[:, cl, :] + upd[cl * rows_per_col:(cl + 1) * rows_per_col, :]


def _mix_out_col(hf, hb, o, x, seg, mod, gh, w, g1, n_heads):
    n, d = x.shape
    grid_rows = seg.seqlen // GRID_W
    tcol = SUBLANES * grid_rows
    ncb = GRID_W // SUBLANES
    x3 = x.reshape(n // GRID_W, GRID_W, d)
    hmap = lambda b, c: (b * ncb + c, 0)
    xmap = lambda b, c: (b, c, 0)
    consts = (gh, w, g1)
    out = pl.pallas_call(
        functools.partial(_mix_out_col_kernel, n_heads=n_heads, rows_per_col=grid_rows),
        out_shape=jax.ShapeDtypeStruct(x3.shape, F32),
        grid=(seg.batch, ncb),
        in_specs=[pl.BlockSpec((tcol, hf.shape[1]), hmap), pl.BlockSpec((tcol, hf.shape[1]), hmap),
                  pl.BlockSpec((tcol, o.shape[1]), hmap),
                  pl.BlockSpec((grid_rows, SUBLANES, d), xmap),
                  pl.BlockSpec((1, N_MOD, d), lambda b, c: (b, 0, 0))] + [_full(a, 2) for a in consts],
        out_specs=pl.BlockSpec((grid_rows, SUBLANES, d), xmap),
        input_output_aliases={3: 0},
        compiler_params=_cp("arbitrary", "arbitrary"),
        name="mix_out_col",
    )(hf, hb, o, x3, mod, *consts)
    return out.reshape(n, d)


def kernel(x, c, ctx, c_ctx, mod_w, mod_b, norm_g, m_w_in, m_b_gate, m_w_conv, m_g_head, m_w_out,
           h_w_in, h_b_f, h_lb_raw, h_g_head, h_w_out, f_w_gu, f_w_down,
           e_w_router, e_b_router, e_w_gu, e_w_down):
    batch, seq, d = x.shape
    ctx_len = ctx.shape[1]
    depth = mod_w.shape[0]
    assert seq % GRID_W == 0 and seq // GRID_W == CHUNK and ctx_len % CHUNK == 0
    assert batch < MOD_ROWS and d % LANES == 0
    lat_seg = _Seg(batch, seq, False)
    ctx_seg = _Seg(batch, ctx_len, True)

    c_all = jnp.zeros((MOD_ROWS, d), F32).at[:batch].set(c).at[batch].set(c_ctx)
    mod_all = _modulation(c_all, mod_w, mod_b).reshape(depth, MOD_ROWS, N_MOD, d)

    lb_all = jax.nn.softmax(h_lb_raw.astype(F32), axis=0)
    lb_all = jnp.cumsum(lb_all, axis=0) - lb_all[0]

    xl = x.reshape(batch * seq, d)
    xc = ctx.reshape(batch * ctx_len, d)
    row = lambda v: v.reshape(1, -1).astype(F32)

    for i in range(depth):
        j = i // 2
        with_ctx = i < depth - 1
        mod = mod_all[i]
        g0, g1, g2, g3 = (row(norm_g[i, s]) for s in range(4))
        if i % 2 == 0:
            qk_w = 2 * M_HEADS * M_DK
            v_w = M_HEADS * M_DV
            w_in = m_w_in[j]
            wqk = w_in[:, :qk_w].astype(BF16)
            wv = w_in[:, qk_w:qk_w + v_w].astype(BF16)
            wo = w_in[:, qk_w + v_w:qk_w + 2 * v_w].astype(BF16)
            ng = 4 * M_HEADS
            wg = jnp.zeros((d, LANES), F32).at[:, :ng].set(w_in[:, qk_w + 2 * v_w:]).astype(BF16)
            gb = jnp.zeros((1, LANES), F32).at[0, :ng].set(m_b_gate[j].reshape(-1))
            col = jnp.arange(LANES)
            gm = (((col // M_HEADS) % 2 == 1) & (col < ng)).astype(F32).reshape(1, LANES)
            pw = (mod, g0, wqk, wv, wo, wg, m_w_conv[j].astype(F32), gb, gm)
            qc, kc, vc, oc, gc = _proj_m(xc, ctx_seg, *pw)
            ql, kl, vl, ol, gl = _proj_m(xl, lat_seg, *pw)
            s0 = jnp.zeros((batch, 2, M_HEADS // 2, LANES, 2 * M_DV), F32)
            m0 = jnp.zeros((batch, 2 * M_HEADS, LANES), F32)
            hcf, hcb, s1, m1 = _mlstm_scan(qc, kc, vc, gc, s0, m0, ctx_seg)
            hlf, hlb, _, _ = _mlstm_scan(ql, kl, vl, gl, s1, m1, lat_seg)
            gh = row(m_g_head[j])
            wout = m_w_out[j].astype(BF16)
            xl = _mix_out(hlf, hlb, ol, xl, lat_seg, mod, gh, wout, g1, M_HEADS)
            if with_ctx:
                xc = _mix_out(hcf, hcb, oc, xc, ctx_seg, mod, gh, wout, g1, M_HEADS)
        else:
            kw = H_HEADS * H_DK
            w_in = h_w_in[j].astype(BF16)
            ws = [w_in[:, s * kw:(s + 1) * kw] for s in range(5)]
            qc, vc, ffc, fbc, gc = _proj_h_ctx(xc, ctx_seg, mod, g0, ws)
            ql, vl, ffl, fbl, gl = _proj_h_lat(xl, lat_seg, mod, g0, ws)
            bias = h_b_f[j].astype(F32)
            lb = lb_all[j]
            s0 = jnp.zeros((batch, 2, H_HEADS, H_DV, H_DK), F32)
            hcf, hcb, s1 = _hgrn_scan(qc, vc, ffc, fbc, bias, lb, s0, ctx_seg)
            hlf, hlb, _ = _hgrn_scan(ql, vl, ffl, fbl, bias, lb, s1, lat_seg)
            gh = row(h_g_head[j])
            wout = h_w_out[j].astype(BF16)
            xl = _mix_out_col(hlf, hlb, gl, xl, lat_seg, mod, gh, wout, g1, H_HEADS)
            if with_ctx:
                xc = _mix_out(hcf, hcb, gc, xc, ctx_seg, mod, gh, wout, g1, H_HEADS)
        streams = [(xl, lat_seg)] + ([(xc, ctx_seg)] if with_ctx else [])
        outs = []
        if i % 2 == 0:
            wgu, wd = f_w_gu[j].astype(BF16), f_w_down[j].astype(BF16)
            for xs, seg in streams:
                outs.append(_ffn(xs, seg, mod, g2, wgu, wd, g3, th=256))
        else:
            wr = jnp.zeros((d, LANES), F32).at[:, :N_EXPERTS].set(e_w_router[j])
            br = jnp.zeros((1, LANES), F32).at[0, :N_EXPERTS].set(e_b_router[j])
            wgu, wd = e_w_gu[j].astype(BF16), e_w_down[j].astype(BF16)
            outs.append(_moe_routed(xl, lat_seg, mod, g2, wr, br, wgu, wd, g3, th=256))
            if with_ctx:
                u, comb = _router(xc, ctx_seg, mod, g2, wr, br)
                outs.append(_moe(u, comb, xc, ctx_seg, mod, wgu, wd, g3, th=256))
        xl = outs[0]
        if with_ctx:
            xc = outs[1]
    return xl.reshape(batch, seq, d)
```

```python
import functools

import jax
import jax.numpy as jnp
from jax import lax
from jax.experimental import pallas as pl
from jax.experimental.pallas import tpu as pltpu

F32 = jnp.float32
BF16 = jnp.bfloat16
EPS = 1e-6
NEG = -1e30

GRID_W = 64
N_MOD = 6
M_HEADS, M_DK, M_DV = 8, 64, 128
H_HEADS, H_DK, H_DV = 8, 128, 128
N_EXPERTS = 8
LANES = 128
SUBLANES = 8
CHUNK = 128
MOD_ROWS = 16
VMEM_LIMIT = 56 * 1024 * 1024

HI = lax.Precision.HIGHEST
NT_DIMS = (((1,), (1,)), ((), ()))
TN_DIMS = (((0,), (0,)), ((), ()))


def _cp(*sem):
    return pltpu.CompilerParams(dimension_semantics=sem, vmem_limit_bytes=VMEM_LIMIT)


def _full(a, nargs):
    zeros = (0,) * a.ndim
    return pl.BlockSpec(a.shape, lambda *_: zeros)


def _sigmoid(x):
    return 1.0 / (1.0 + jnp.exp(-x))


def _silu(x):
    return x * _sigmoid(x)


def _log_sigmoid(x):
    return jnp.minimum(x, 0.0) - jnp.log(1.0 + jnp.exp(-jnp.abs(x)))


def _rms(x):
    return x * lax.rsqrt(jnp.mean(x * x, axis=-1, keepdims=True) + EPS)


def _normmod(x, g, mod, s):
    return _rms(x) * (g * (1.0 + mod[s + 1:s + 2])) + mod[s:s + 1]


def _split2(x):
    hi = x.astype(BF16)
    lo = (x - hi.astype(F32)).astype(BF16)
    return hi, lo


def _cumsum_mat(n, reverse):
    r = lax.broadcasted_iota(jnp.int32, (n, n), 0)
    c = lax.broadcasted_iota(jnp.int32, (n, n), 1)
    return jnp.where((c >= r) if reverse else (c <= r), 1.0, 0.0).astype(BF16)


def _cumsum_time(x, tri):
    hi, lo = _split2(x)
    return (jnp.dot(tri, hi, preferred_element_type=F32)
            + jnp.dot(tri, lo, preferred_element_type=F32))


class _Seg:
    def __init__(self, batch, seqlen, is_ctx):
        self.batch, self.seqlen, self.is_ctx = batch, seqlen, is_ctx
        self.n = batch * seqlen

    def tile(self, pref):
        tm = min(pref, self.n if self.is_ctx else self.seqlen)
        assert self.n % tm == 0 and (self.seqlen % tm == 0 or tm % self.seqlen == 0)
        return tm

    def mod_spec(self, tm, d, nargs):
        if self.is_ctx:
            row = lambda j: self.batch
        else:
            row = lambda j: (j * tm) // self.seqlen
        if nargs == 1:
            return pl.BlockSpec((1, N_MOD, d), lambda j: (row(j), 0, 0))
        if nargs == 2:
            return pl.BlockSpec((1, N_MOD, d), lambda j, k: (row(j), 0, 0))
        return pl.BlockSpec((1, N_MOD, d), lambda j, e, k: (row(j), 0, 0))


def _mod_kernel(c_ref, w_ref, b_ref, o_ref):
    c = c_ref[...]
    o_ref[0] = jnp.dot(_silu(c), w_ref[0], preferred_element_type=F32, precision=HI) + b_ref[0]


def _modulation(c_all, mod_w, mod_b):
    depth, d, n = mod_w.shape
    tn = n // 4
    return pl.pallas_call(
        _mod_kernel,
        out_shape=jax.ShapeDtypeStruct((depth, MOD_ROWS, n), F32),
        grid=(depth, n // tn),
        in_specs=[pl.BlockSpec((MOD_ROWS, d), lambda i, j: (0, 0)),
                  pl.BlockSpec((1, d, tn), lambda i, j: (i, 0, j)),
                  pl.BlockSpec((1, 1, tn), lambda i, j: (i, 0, j))],
        out_specs=pl.BlockSpec((1, MOD_ROWS, tn), lambda i, j: (i, 0, j)),
        compiler_params=_cp("arbitrary", "arbitrary"),
        name="modulation",
    )(c_all, mod_w, mod_b.reshape(depth, 1, n))


def _proj_m_kernel(xp_ref, x_ref, xn_ref, mod_ref, g_ref, wqk_ref, wv_ref, wo_ref, wg_ref,
                   cw_ref, gb_ref, gm_ref, q_ref, kt_ref, v_ref, o_ref, gt_ref, *, tm, seqlen):
    j = pl.program_id(0)
    mod = mod_ref[0]
    g = g_ref[...]
    u = _normmod(x_ref[...], g, mod, 0).astype(BF16)
    uh = _normmod(jnp.concatenate([xp_ref[...], xn_ref[...]], axis=0), g, mod, 0).astype(BF16)
    wqk = wqk_ref[...]
    z = jnp.dot(u, wqk, preferred_element_type=F32)
    zh = jnp.dot(uh, wqk, preferred_element_type=F32)
    local = lax.broadcasted_iota(jnp.int32, (tm, 1), 0)
    pos = lax.rem(j * tm + local, seqlen)
    zp = jnp.where(local == 0, zh[SUBLANES - 1:SUBLANES], pltpu.roll(z, 1, 0))
    zn = jnp.where(local == tm - 1, zh[SUBLANES:SUBLANES + 1], pltpu.roll(z, tm - 1, 0))
    zp = jnp.where(pos == 0, 0.0, zp)
    zn = jnp.where(pos == seqlen - 1, 0.0, zn)
    cw = cw_ref[...]
    a = _silu(cw[0:1] * zp + cw[1:2] * z + cw[2:3] * zn)
    half = a.shape[1] // 2
    q_ref[...] = a[:, :half].astype(BF16)
    kt_ref[...] = (a[:, half:] * (M_DK ** -0.5)).T.astype(BF16)
    v_ref[...] = jnp.dot(u, wv_ref[...], preferred_element_type=F32).astype(BF16)
    o_ref[...] = jnp.dot(u, wo_ref[...], preferred_element_type=F32).astype(BF16)
    zg = jnp.dot(u, wg_ref[...], preferred_element_type=F32) + gb_ref[...]
    gt_ref[...] = jnp.where(gm_ref[...] > 0.5, _log_sigmoid(zg), zg)


def _proj_m(x, seg, mod, g, wqk, wv, wo, wg, cw, gb, gm):
    n, d = x.shape
    tm = seg.tile(512)
    nb8 = n // SUBLANES
    r8 = tm // SUBLANES
    kern = functools.partial(_proj_m_kernel, tm=tm, seqlen=seg.seqlen)
    nqk = wqk.shape[1]
    consts = (g, wqk, wv, wo, wg, cw, gb, gm)
    return pl.pallas_call(
        kern,
        out_shape=[jax.ShapeDtypeStruct((n, nqk // 2), BF16),
                   jax.ShapeDtypeStruct((nqk // 2, n), BF16),
                   jax.ShapeDtypeStruct((n, wv.shape[1]), BF16),
                   jax.ShapeDtypeStruct((n, wo.shape[1]), BF16),
                   jax.ShapeDtypeStruct((n, LANES), F32)],
        grid=(n // tm,),
        in_specs=[pl.BlockSpec((SUBLANES, d), lambda j: (jnp.maximum(j * r8 - 1, 0), 0)),
                  pl.BlockSpec((tm, d), lambda j: (j, 0)),
                  pl.BlockSpec((SUBLANES, d), lambda j: (jnp.minimum((j + 1) * r8, nb8 - 1), 0)),
                  seg.mod_spec(tm, d, 1)] + [_full(a, 1) for a in consts],
        out_specs=[pl.BlockSpec((tm, nqk // 2), lambda j: (j, 0)),
                   pl.BlockSpec((nqk // 2, tm), lambda j: (0, j)),
                   pl.BlockSpec((tm, wv.shape[1]), lambda j: (j, 0)),
                   pl.BlockSpec((tm, wo.shape[1]), lambda j: (j, 0)),
                   pl.BlockSpec((tm, LANES), lambda j: (j, 0))],
        compiler_params=_cp("arbitrary"),
        name="mlstm_proj",
    )(x, x, x, mod, *consts)


def _mlstm_dir(q_ref, kt_ref, v_ref, g_ref, h_ref, s_scr, m_scr, reverse, d):
    L = CHUNK
    g = g_ref[...]
    b = _cumsum_time(g, _cumsum_mat(L, reverse))
    gT = g.T
    bT = b.T
    r = lax.broadcasted_iota(jnp.int32, (L, L), 0)
    c = lax.broadcasted_iota(jnp.int32, (L, L), 1)
    causal = (c >= r) if reverse else (c <= r)
    last = 0 if reverse else L - 1
    c_ig = 2 * M_HEADS * d
    c_lf = c_ig + M_HEADS
    lane = lax.broadcasted_iota(jnp.int32, (1, LANES), 1)
    ones = jnp.ones((L, M_DV), BF16)
    for p in range(M_HEADS // 2):
        q_p = q_ref[:, p * LANES:(p + 1) * LANES]
        kt_p = kt_ref[p * LANES:(p + 1) * LANES, :]
        s_bf = s_scr[d, p].astype(BF16)
        for e in range(2):
            h = 2 * p + e
            rows = slice(e * M_DK, (e + 1) * M_DK)
            q_e = jnp.where((lane // M_DK) == e, q_p, jnp.zeros_like(q_p))
            b_col = b[:, c_lf + h:c_lf + h + 1]
            beta = gT[c_ig + h:c_ig + h + 1, :] - bT[c_lf + h:c_lf + h + 1, :]
            m_prev = m_scr[d * M_HEADS + h:d * M_HEADS + h + 1, 0:1]
            bm = jnp.where(causal, beta, NEG)
            mu = jnp.maximum(m_prev, jnp.max(bm, axis=1, keepdims=True))
            qk = jnp.dot(q_e, kt_p, preferred_element_type=F32)
            s = (qk * jnp.exp(bm - mu)).astype(BF16)
            vp = jnp.concatenate([v_ref[:, h * M_DV:(h + 1) * M_DV], ones], axis=1)
            tot = (jnp.dot(s, vp, preferred_element_type=F32)
                   + jnp.exp(m_prev - mu) * jnp.dot(q_e, s_bf, preferred_element_type=F32))
            den = jnp.maximum(jnp.abs(tot[:, M_DV:]), jnp.exp(-(b_col + mu)))
            h_ref[:, h * M_DV:(h + 1) * M_DV] = (tot[:, :M_DV] / den).astype(BF16)
            mu_last = mu[last:last + 1, :]
            ktw = (kt_p[rows, :].astype(F32) * jnp.exp(beta - mu_last)).astype(BF16)
            s_scr[d, p, rows, :] = (jnp.exp(m_prev - mu_last) * s_scr[d, p, rows, :]
                                    + jnp.dot(ktw, vp, preferred_element_type=F32))
            m_new = b_col[last:last + 1, :] + mu_last
            m_scr[d * M_HEADS + h:d * M_HEADS + h + 1, :] = jnp.broadcast_to(m_new, (1, LANES))


def _mlstm_scan_kernel(qf, kf, vf, gf, qb, kb, vb, gb, s0_ref, m0_ref,
                       hf_ref, hb_ref, st_ref, mt_ref, s_scr, m_scr):
    n = pl.program_id(1)

    @pl.when(n == 0)
    def _():
        s_scr[...] = s0_ref[0]
        m_scr[...] = m0_ref[0]

    _mlstm_dir(qf, kf, vf, gf, hf_ref, s_scr, m_scr, False, 0)
    _mlstm_dir(qb, kb, vb, gb, hb_ref, s_scr, m_scr, True, 1)

    @pl.when(n == pl.num_programs(1) - 1)
    def _():
        st_ref[0] = s_scr[...]
        mt_ref[0] = m_scr[...]


def _mlstm_scan(q, kt, v, gates, s0, m0, seg):
    L = CHUNK
    assert seg.seqlen % L == 0
    n = seg.seqlen // L
    fwd = lambda b, i: (b * n + i, 0)
    bwd = lambda b, i: (b * n + (n - 1 - i), 0)
    fwd_t = lambda b, i: (0, b * n + i)
    bwd_t = lambda b, i: (0, b * n + (n - 1 - i))
    dq, dv = q.shape[1], v.shape[1]
    st = lambda b, i: (b, 0, 0, 0, 0)
    mt = lambda b, i: (b, 0, 0)
    return pl.pallas_call(
        _mlstm_scan_kernel,
        out_shape=[jax.ShapeDtypeStruct((seg.n, dv), BF16), jax.ShapeDtypeStruct((seg.n, dv), BF16),
                   jax.ShapeDtypeStruct(s0.shape, F32), jax.ShapeDtypeStruct(m0.shape, F32)],
        grid=(seg.batch, n),
        in_specs=[pl.BlockSpec((L, dq), fwd), pl.BlockSpec((dq, L), fwd_t), pl.BlockSpec((L, dv), fwd),
                  pl.BlockSpec((L, LANES), fwd),
                  pl.BlockSpec((L, dq), bwd), pl.BlockSpec((dq, L), bwd_t), pl.BlockSpec((L, dv), bwd),
                  pl.BlockSpec((L, LANES), bwd),
                  pl.BlockSpec((1,) + s0.shape[1:], st), pl.BlockSpec((1,) + m0.shape[1:], mt)],
        out_specs=[pl.BlockSpec((L, dv), fwd), pl.BlockSpec((L, dv), bwd),
                   pl.BlockSpec((1,) + s0.shape[1:], st), pl.BlockSpec((1,) + m0.shape[1:], mt)],
        scratch_shapes=[pltpu.VMEM(s0.shape[1:], F32), pltpu.VMEM(m0.shape[1:], F32)],
        compiler_params=_cp("arbitrary", "arbitrary"),
        name="mlstm_scan",
    )(q, kt, v, gates, q, kt, v, gates, s0, m0)


def _head_out(hf, hb, gate_pre, gh, w, n_heads):
    h = hf.astype(F32) + hb.astype(F32)
    dv = h.shape[1] // n_heads
    parts = [_rms(h[:, i * dv:(i + 1) * dv]) for i in range(n_heads)]
    hn = jnp.concatenate(parts, axis=1) * gh * _sigmoid(gate_pre.astype(F32))
    return jnp.dot(hn.astype(BF16), w, preferred_element_type=F32)


def _mix_out_kernel(hf_ref, hb_ref, o_ref, x_ref, mod_ref, gh_ref, w_ref, g1_ref, out_ref, *, n_heads):
    mod = mod_ref[0]
    y = _head_out(hf_ref[...], hb_ref[...], o_ref[...], gh_ref[...], w_ref[...], n_heads)
    out_ref[...] = x_ref[...] + mod[2:3] * (_rms(y) * g1_ref[...])


def _mix_out(hf, hb, o, x, seg, mod, gh, w, g1, n_heads):
    n, d = x.shape
    tm = seg.tile(512)
    rmap = lambda j: (j, 0)
    consts = (gh, w, g1)
    return pl.pallas_call(
        functools.partial(_mix_out_kernel, n_heads=n_heads),
        out_shape=jax.ShapeDtypeStruct(x.shape, F32),
        grid=(n // tm,),
        in_specs=[pl.BlockSpec((tm, hf.shape[1]), rmap), pl.BlockSpec((tm, hf.shape[1]), rmap),
                  pl.BlockSpec((tm, o.shape[1]), rmap), pl.BlockSpec((tm, d), rmap),
                  seg.mod_spec(tm, d, 1)] + [_full(a, 1) for a in consts],
        out_specs=pl.BlockSpec((tm, d), rmap),
        input_output_aliases={3: 0},
        compiler_params=_cp("arbitrary"),
        name="mix_out",
    )(hf, hb, o, x, mod, *consts)


def _ffn_kernel(x_ref, mod_ref, g2_ref, wg_ref, wu_ref, wd_ref, g3_ref, out_ref, u_scr, acc_scr):
    kk = pl.program_id(1)

    @pl.when(kk == 0)
    def _():
        u_scr[...] = _normmod(x_ref[...], g2_ref[...], mod_ref[0], 3).astype(BF16)
        acc_scr[...] = jnp.zeros_like(acc_scr)

    u = u_scr[...]
    gg = jnp.dot(u, wg_ref[...], preferred_element_type=F32)
    up = jnp.dot(u, wu_ref[...], preferred_element_type=F32)
    hh = (_silu(gg) * up).astype(BF16)
    acc_scr[...] += jnp.dot(hh, wd_ref[...], preferred_element_type=F32)

    @pl.when(kk == pl.num_programs(1) - 1)
    def _():
        mod = mod_ref[0]
        out_ref[...] = x_ref[...] + mod[5:6] * (_rms(acc_scr[...]) * g3_ref[...])


def _ffn(x, seg, mod, g2, wgu, wd, g3, th):
    n, d = x.shape
    tm = seg.tile(1024)
    hd = wd.shape[0]
    assert hd % th == 0
    nk = hd // th
    return pl.pallas_call(
        _ffn_kernel,
        out_shape=jax.ShapeDtypeStruct((n, d), F32),
        grid=(n // tm, nk),
        in_specs=[pl.BlockSpec((tm, d), lambda i, k: (i, 0)),
                  seg.mod_spec(tm, d, 2),
                  _full(g2, 2),
                  pl.BlockSpec((d, th), lambda i, k: (0, k)),
                  pl.BlockSpec((d, th), lambda i, k: (0, nk + k)),
                  pl.BlockSpec((th, d), lambda i, k: (k, 0)),
                  _full(g3, 2)],
        out_specs=pl.BlockSpec((tm, d), lambda i, k: (i, 0)),
        scratch_shapes=[pltpu.VMEM((tm, d), BF16), pltpu.VMEM((tm, d), F32)],
        input_output_aliases={0: 0},
        compiler_params=_cp("arbitrary", "arbitrary"),
        name="ffn",
    )(x, mod, g2, wgu, wgu, wd, g3)


def _router_kernel(x_ref, mod_ref, g2_ref, wr_ref, br_ref, u_ref, comb_ref):
    u = _normmod(x_ref[...], g2_ref[...], mod_ref[0], 3)
    u_ref[...] = u.astype(BF16)
    lane = lax.broadcasted_iota(jnp.int32, (1, LANES), 1)
    logits = jnp.dot(u, wr_ref[...], preferred_element_type=F32, precision=HI) + br_ref[...]
    logits = jnp.where(lane < N_EXPERTS, logits, NEG)
    m1 = jnp.max(logits, axis=1, keepdims=True)
    i1 = jnp.min(jnp.where(logits == m1, lane, LANES), axis=1, keepdims=True)
    rest = jnp.where(lane == i1, NEG, logits)
    m2 = jnp.max(rest, axis=1, keepdims=True)
    i2 = jnp.min(jnp.where(rest == m2, lane, LANES), axis=1, keepdims=True)
    e2 = jnp.exp(m2 - m1)
    w1 = 1.0 / (1.0 + e2)
    comb_ref[...] = jnp.where(lane == i1, w1, 0.0) + jnp.where(lane == i2, e2 * w1, 0.0)


def _router(x, seg, mod, g2, wr, br):
    n, d = x.shape
    tm = seg.tile(1024)
    consts = (g2, wr, br)
    return pl.pallas_call(
        _router_kernel,
        out_shape=[jax.ShapeDtypeStruct((n, d), BF16), jax.ShapeDtypeStruct((n, LANES), F32)],
        grid=(n // tm,),
        in_specs=[pl.BlockSpec((tm, d), lambda i: (i, 0)), seg.mod_spec(tm, d, 1)]
                 + [_full(a, 1) for a in consts],
        out_specs=[pl.BlockSpec((tm, d), lambda i: (i, 0)), pl.BlockSpec((tm, LANES), lambda i: (i, 0))],
        compiler_params=_cp("arbitrary"),
        name="router",
    )(x, mod, *consts)


def _moe_kernel(u_ref, comb_ref, x_ref, mod_ref, wg_ref, wu_ref, wd_ref, g3_ref, out_ref, acc_scr):
    e = pl.program_id(1)
    kk = pl.program_id(2)

    @pl.when((e == 0) & (kk == 0))
    def _():
        acc_scr[...] = jnp.zeros_like(acc_scr)

    u = u_ref[...]
    gg = jnp.dot(u, wg_ref[0], preferred_element_type=F32)
    up = jnp.dot(u, wu_ref[0], preferred_element_type=F32)
    hh = (_silu(gg) * up).astype(BF16)
    lane = lax.broadcasted_iota(jnp.int32, (1, LANES), 1)
    cw = jnp.sum(jnp.where(lane == e, comb_ref[...], 0.0), axis=1, keepdims=True)
    acc_scr[...] += cw * jnp.dot(hh, wd_ref[0], preferred_element_type=F32)

    @pl.when((e == pl.num_programs(1) - 1) & (kk == pl.num_programs(2) - 1))
    def _():
        mod = mod_ref[0]
        out_ref[...] = x_ref[...] + mod[5:6] * (_rms(acc_scr[...]) * g3_ref[...])


def _moe(u, comb, x, seg, mod, wgu, wd, g3, th):
    n, d = x.shape
    tm = seg.tile(1024)
    ne, hd, _ = wd.shape
    nk = hd // th
    return pl.pallas_call(
        _moe_kernel,
        out_shape=jax.ShapeDtypeStruct((n, d), F32),
        grid=(n // tm, ne, nk),
        in_specs=[pl.BlockSpec((tm, d), lambda i, e, k: (i, 0)),
                  pl.BlockSpec((tm, LANES), lambda i, e, k: (i, 0)),
                  pl.BlockSpec((tm, d), lambda i, e, k: (i, 0)),
                  seg.mod_spec(tm, d, 3),
                  pl.BlockSpec((1, d, th), lambda i, e, k: (e, 0, k)),
                  pl.BlockSpec((1, d, th), lambda i, e, k: (e, 0, nk + k)),
                  pl.BlockSpec((1, th, d), lambda i, e, k: (e, k, 0)),
                  _full(g3, 3)],
        out_specs=pl.BlockSpec((tm, d), lambda i, e, k: (i, 0)),
        scratch_shapes=[pltpu.VMEM((tm, d), F32)],
        input_output_aliases={2: 0},
        compiler_params=_cp("arbitrary", "arbitrary", "arbitrary"),
        name="moe",
    )(u, comb, x, mod, wgu, wgu, wd, g3)


ROW_TILE = 1024
META_E, META_RANK, META_W = 0, 2, 4


def _route_kernel(x_ref, mod_ref, g2_ref, wr_ref, br_ref, meta_ref, cnt_ref, tri_scr, run_scr):
    i = pl.program_id(0)
    tm = x_ref.shape[0]

    @pl.when(i == 0)
    def _():
        r = lax.broadcasted_iota(jnp.int32, (tm, tm), 0)
        c = lax.broadcasted_iota(jnp.int32, (tm, tm), 1)
        tri_scr[...] = jnp.where(c < r, 1.0, 0.0).astype(BF16)
        run_scr[...] = jnp.zeros_like(run_scr)

    u = _normmod(x_ref[...], g2_ref[...], mod_ref[0], 3)
    lane = lax.broadcasted_iota(jnp.int32, (1, LANES), 1)
    logits = jnp.dot(u, wr_ref[...], preferred_element_type=F32, precision=HI) + br_ref[...]
    logits = jnp.where(lane < N_EXPERTS, logits, NEG)
    m1 = jnp.max(logits, axis=1, keepdims=True)
    i1 = jnp.min(jnp.where(logits == m1, lane, LANES), axis=1, keepdims=True)
    rest = jnp.where(lane == i1, NEG, logits)
    m2 = jnp.max(rest, axis=1, keepdims=True)
    i2 = jnp.min(jnp.where(rest == m2, lane, LANES), axis=1, keepdims=True)
    e2 = jnp.exp(m2 - m1)
    w1 = 1.0 / (1.0 + e2)
    sel = ((lane == i1) | (lane == i2))
    onehot = jnp.where(sel, 1.0, 0.0)
    before = jnp.dot(tri_scr[...], onehot.astype(BF16), preferred_element_type=F32) + run_scr[...]
    rank1 = jnp.sum(jnp.where(lane == i1, before, 0.0), axis=1, keepdims=True)
    rank2 = jnp.sum(jnp.where(lane == i2, before, 0.0), axis=1, keepdims=True)
    run_scr[...] = run_scr[...] + jnp.sum(onehot, axis=0, keepdims=True)
    vals = (i1.astype(F32), i2.astype(F32), rank1, rank2, w1, e2 * w1)
    meta = jnp.zeros((tm, LANES), F32)
    for col, val in enumerate(vals):
        meta = jnp.where(lane == col, val, meta)
    meta_ref[...] = meta
    cnt_ref[...] = jnp.broadcast_to(run_scr[...], cnt_ref.shape)


def _route(x, seg, mod, g2, wr, br):
    n, d = x.shape
    tm = seg.tile(1024)
    consts = (g2, wr, br)
    return pl.pallas_call(
        _route_kernel,
        out_shape=[jax.ShapeDtypeStruct((n, LANES), F32),
                   jax.ShapeDtypeStruct((SUBLANES, LANES), F32)],
        grid=(n // tm,),
        in_specs=[pl.BlockSpec((tm, d), lambda i: (i, 0)), seg.mod_spec(tm, d, 1)]
                 + [_full(a, 1) for a in consts],
        out_specs=[pl.BlockSpec((tm, LANES), lambda i: (i, 0)),
                   pl.BlockSpec((SUBLANES, LANES), lambda i: (0, 0))],
        scratch_shapes=[pltpu.VMEM((tm, tm), BF16), pltpu.VMEM((1, LANES), F32)],
        compiler_params=_cp("arbitrary"),
        name="route",
    )(x, mod, *consts)


def _drain(src_ref, dst_ref, sem, count):
    def body(i, carry):
        pltpu.make_async_copy(src_ref, dst_ref, sem).wait()
        return carry

    lax.fori_loop(0, count, body, 0, unroll=8)


def _dispatch_kernel(dst_ref, pad_ref, x_ref, mod_ref, g2_ref, xs_ref, ubuf, sem):
    j = pl.program_id(0)
    tm = x_ref.shape[0]
    u = _normmod(x_ref[...], g2_ref[...], mod_ref[0], 3)
    for s in range(SUBLANES):
        ubuf[:, s, :] = u[:, s * LANES:(s + 1) * LANES]

    def issue(t, carry):
        pltpu.make_async_copy(ubuf.at[t], xs_ref.at[dst_ref[0, 0, t]], sem).start()
        pltpu.make_async_copy(ubuf.at[t], xs_ref.at[dst_ref[0, 0, tm + t]], sem).start()
        return carry

    lax.fori_loop(0, tm, issue, 0, unroll=8)
    npad = pad_ref.shape[2]

    @pl.when(j == 0)
    def _():
        def issue_pad(p, carry):
            pltpu.make_async_copy(ubuf.at[0], xs_ref.at[pad_ref[0, 0, p]], sem).start()
            return carry

        lax.fori_loop(0, npad, issue_pad, 0, unroll=8)
        _drain(ubuf.at[0], xs_ref.at[0], sem, npad)

    _drain(ubuf.at[0], xs_ref.at[0], sem, 2 * tm)


def _dispatch(x, seg, mod, g2, dest, pad_dst, n_rows):
    n, d = x.shape
    tm = seg.tile(512)
    steps = n // tm
    dst = dest.reshape(steps, tm, 2).transpose(0, 2, 1).reshape(steps, 1, 2 * tm)
    npad = pad_dst.shape[0]
    return pl.pallas_call(
        _dispatch_kernel,
        out_shape=jax.ShapeDtypeStruct((n_rows, SUBLANES, d // SUBLANES), F32),
        grid=(steps,),
        in_specs=[pl.BlockSpec((1, 1, 2 * tm), lambda j: (j, 0, 0), memory_space=pltpu.SMEM),
                  pl.BlockSpec((1, 1, npad), lambda j: (0, 0, 0), memory_space=pltpu.SMEM),
                  pl.BlockSpec((tm, d), lambda j: (j, 0)), seg.mod_spec(tm, d, 1), _full(g2, 1)],
        out_specs=pl.BlockSpec(memory_space=pl.ANY),
        scratch_shapes=[pltpu.VMEM((tm, SUBLANES, d // SUBLANES), F32), pltpu.SemaphoreType.DMA(())],
        compiler_params=_cp("arbitrary"),
        name="dispatch",
    )(dst, pad_dst.reshape(1, 1, npad), x, mod, g2)


def _experts_kernel(te_ref, used_ref, xs_ref, wg_ref, wu_ref, wd_ref, y_ref, u_scr, acc_scr):
    i = pl.program_id(0)
    kk = pl.program_id(1)

    @pl.when(i < used_ref[0])
    def _():
        @pl.when(kk == 0)
        def _():
            for s in range(SUBLANES):
                u_scr[:, s * LANES:(s + 1) * LANES] = xs_ref[:, s, :].astype(BF16)
            acc_scr[...] = jnp.zeros_like(acc_scr)

        u = u_scr[...]
        gg = jnp.dot(u, wg_ref[0], preferred_element_type=F32)
        up = jnp.dot(u, wu_ref[0], preferred_element_type=F32)
        hh = (_silu(gg) * up).astype(BF16)
        acc_scr[...] += jnp.dot(hh, wd_ref[0], preferred_element_type=F32)

        @pl.when(kk == pl.num_programs(1) - 1)
        def _():
            y = acc_scr[...]
            for s in range(SUBLANES):
                y_ref[:, s, :] = y[:, s * LANES:(s + 1) * LANES]


def _experts(xs3, tile_expert, used, wgu, wd, th):
    tr = ROW_TILE
    n_rows = xs3.shape[0]
    ne, hd, d = wd.shape
    nk = hd // th
    tiles = tile_expert.shape[0]
    assert tiles * tr <= n_rows
    row = lambda i, k, te, us: (jnp.minimum(i, us[0] - 1), 0, 0)
    kfix = lambda i, k, us: jnp.where(i < us[0], k, nk - 1)
    grid_spec = pltpu.PrefetchScalarGridSpec(
        num_scalar_prefetch=2,
        grid=(tiles, nk),
        in_specs=[pl.BlockSpec((tr, SUBLANES, LANES), row),
                  pl.BlockSpec((1, d, th), lambda i, k, te, us: (te[i], 0, kfix(i, k, us))),
                  pl.BlockSpec((1, d, th), lambda i, k, te, us: (te[i], 0, nk + kfix(i, k, us))),
                  pl.BlockSpec((1, th, d), lambda i, k, te, us: (te[i], kfix(i, k, us), 0))],
        out_specs=pl.BlockSpec((tr, SUBLANES, LANES), row),
        scratch_shapes=[pltpu.VMEM((tr, d), BF16), pltpu.VMEM((tr, d), F32)],
    )
    return pl.pallas_call(
        _experts_kernel,
        out_shape=jax.ShapeDtypeStruct((tiles * tr, SUBLANES, LANES), F32),
        grid_spec=grid_spec,
        compiler_params=_cp("arbitrary", "arbitrary"),
        name="experts",
    )(tile_expert, used, xs3, wgu, wgu, wd)


def _combine_kernel(src_ref, y_ref, meta_ref, x_ref, mod_ref, g3_ref, out_ref, ybuf, sem):
    tm = x_ref.shape[0]

    def issue(t, carry):
        pltpu.make_async_copy(y_ref.at[src_ref[0, 0, t]], ybuf.at[0, t], sem).start()
        pltpu.make_async_copy(y_ref.at[src_ref[0, 0, tm + t]], ybuf.at[1, t], sem).start()
        return carry

    lax.fori_loop(0, tm, issue, 0, unroll=8)
    _drain(y_ref.at[0], ybuf.at[0, 0], sem, 2 * tm)
    meta = meta_ref[...]
    w1 = meta[:, META_W:META_W + 1]
    w2 = meta[:, META_W + 1:META_W + 2]
    ys = [w1 * ybuf[0, :, s, :] + w2 * ybuf[1, :, s, :] for s in range(SUBLANES)]
    ss = ys[0] * ys[0]
    for y in ys[1:]:
        ss = ss + y * y
    d = x_ref.shape[1]
    r = lax.rsqrt(jnp.sum(ss, axis=1, keepdims=True) / d + EPS)
    mod = mod_ref[0]
    for s, y in enumerate(ys):
        sl = slice(s * LANES, (s + 1) * LANES)
        out_ref[:, sl] = x_ref[:, sl] + mod[5:6, sl] * (y * r * g3_ref[:, sl])


def _combine(y3, dest, meta, x, seg, mod, g3):
    n, d = x.shape
    tm = seg.tile(512)
    steps = n // tm
    src = dest.reshape(steps, tm, 2).transpose(0, 2, 1).reshape(steps, 1, 2 * tm)
    return pl.pallas_call(
        _combine_kernel,
        out_shape=jax.ShapeDtypeStruct((n, d), F32),
        grid=(steps,),
        in_specs=[pl.BlockSpec((1, 1, 2 * tm), lambda j: (j, 0, 0), memory_space=pltpu.SMEM),
                  pl.BlockSpec(memory_space=pl.ANY),
                  pl.BlockSpec((tm, LANES), lambda j: (j, 0)),
                  pl.BlockSpec((tm, d), lambda j: (j, 0)),
                  seg.mod_spec(tm, d, 1), _full(g3, 1)],
        out_specs=pl.BlockSpec((tm, d), lambda j: (j, 0)),
        scratch_shapes=[pltpu.VMEM((2, tm, SUBLANES, d // SUBLANES), F32), pltpu.SemaphoreType.DMA(())],
        input_output_aliases={3: 0},
        compiler_params=_cp("arbitrary"),
        name="combine",
    )(src, y3, meta, x, mod, g3)


def _moe_routed(x, seg, mod, g2, wr, br, wgu, wd, g3, th):
    n, d = x.shape
    assert d == SUBLANES * LANES
    tr = ROW_TILE
    meta, cnt = _route(x, seg, mod, g2, wr, br)
    counts = cnt[0, :N_EXPERTS].astype(jnp.int32)
    sizes = ((counts + tr - 1) // tr) * tr
    ends = jnp.cumsum(sizes)
    offs = ends - sizes
    mi = meta[:, :META_W].astype(jnp.int32)
    eidx = mi[:, META_E:META_E + 2]
    dest = jnp.sum(jnp.where(eidx[:, :, None] == jnp.arange(N_EXPERTS)[None, None, :],
                             offs[None, None, :], 0), axis=2) + mi[:, META_RANK:META_RANK + 2]
    dest = dest.astype(jnp.int32)
    tiles = (2 * n) // tr + N_EXPERTS
    n_rows = tiles * tr
    used = (ends[-1] // tr).astype(jnp.int32).reshape(1)
    tile_start = jnp.arange(tiles, dtype=jnp.int32) * tr
    tile_expert = jnp.sum(tile_start[:, None] >= ends[None, :], axis=1).astype(jnp.int32)
    last_expert = jnp.sum(jnp.maximum(ends[-1] - tr, 0) >= ends).astype(jnp.int32)
    tile_expert = jnp.where(tile_start < ends[-1], tile_expert, last_expert)
    pad = jnp.arange(N_EXPERTS * tr, dtype=jnp.int32).reshape(N_EXPERTS, tr)
    pad_row = offs[:, None] + counts[:, None] + jnp.arange(tr, dtype=jnp.int32)[None, :]
    pad_dst = jnp.where(pad_row < ends[:, None], pad_row, n_rows + pad).reshape(-1).astype(jnp.int32)
    xs3 = _dispatch(x, seg, mod, g2, dest, pad_dst, n_rows + N_EXPERTS * tr)
    y3 = _experts(xs3, tile_expert, used, wgu, wd, th)
    return _combine(y3, dest, meta, x, seg, mod, g3)


def _proj_h_body(u, w_refs, out_refs):
    for w_ref, o_ref in zip(w_refs, out_refs):
        o_ref[...] = jnp.dot(u, w_ref[...], preferred_element_type=F32).astype(BF16)


def _proj_h_lat_kernel(x_ref, mod_ref, g_ref, *refs, rows_per_col):
    n = (len(refs) - 1) // 2
    w_refs, out_refs, u_scr = refs[:n], refs[n:2 * n], refs[2 * n]
    mod = mod_ref[0]
    g = g_ref[...]
    for cl in range(SUBLANES):
        u_scr[cl * rows_per_col:(cl + 1) * rows_per_col, :] = _normmod(x_ref[:, cl, :], g, mod, 0).astype(BF16)
    _proj_h_body(u_scr[...], w_refs, out_refs)


def _proj_h_ctx_kernel(x_ref, mod_ref, g_ref, *refs):
    n = len(refs) // 2
    u = _normmod(x_ref[...], g_ref[...], mod_ref[0], 0).astype(BF16)
    _proj_h_body(u, refs[:n], refs[n:])


def _proj_h_lat(x, seg, mod, g, ws):
    n, d = x.shape
    grid_rows = seg.seqlen // GRID_W
    x3 = x.reshape(n // GRID_W, GRID_W, d)
    tcol = SUBLANES * grid_rows
    ncb = GRID_W // SUBLANES
    return pl.pallas_call(
        functools.partial(_proj_h_lat_kernel, rows_per_col=grid_rows),
        out_shape=[jax.ShapeDtypeStruct((n, w.shape[1]), BF16) for w in ws],
        grid=(seg.batch, ncb),
        in_specs=[pl.BlockSpec((grid_rows, SUBLANES, d), lambda b, c: (b, c, 0)),
                  pl.BlockSpec((1, N_MOD, d), lambda b, c: (b, 0, 0)), _full(g, 2)]
                 + [_full(w, 2) for w in ws],
        out_specs=[pl.BlockSpec((tcol, w.shape[1]), lambda b, c: (b * ncb + c, 0)) for w in ws],
        scratch_shapes=[pltpu.VMEM((tcol, d), BF16)],
        compiler_params=_cp("arbitrary", "arbitrary"),
        name="hgrn_proj_lat",
    )(x3, mod, g, *ws)


def _proj_h_ctx(x, seg, mod, g, ws):
    n, d = x.shape
    tm = seg.tile(512)
    return pl.pallas_call(
        _proj_h_ctx_kernel,
        out_shape=[jax.ShapeDtypeStruct((n, w.shape[1]), BF16) for w in ws],
        grid=(n // tm,),
        in_specs=[pl.BlockSpec((tm, d), lambda j: (j, 0)), seg.mod_spec(tm, d, 1), _full(g, 1)]
                 + [_full(w, 1) for w in ws],
        out_specs=[pl.BlockSpec((tm, w.shape[1]), lambda j: (j, 0)) for w in ws],
        compiler_params=_cp("arbitrary"),
        name="hgrn_proj_ctx",
    )(x, mod, g, *ws)


def _hgrn_dir(q_ref, v_ref, f_ref, bias_ref, lb_ref, h_ref, s_scr, reverse, d):
    L = CHUNK
    lb = lb_ref[d:d + 1, :]
    f = lb + (1.0 - lb) * _sigmoid(f_ref[...].astype(F32) + bias_ref[d:d + 1, :])
    a_all = _cumsum_time(jnp.log(f), _cumsum_mat(L, reverse))
    r = lax.broadcasted_iota(jnp.int32, (L, L), 0)
    c = lax.broadcasted_iota(jnp.int32, (L, L), 1)
    causal = (c >= r) if reverse else (c <= r)
    same64 = (r // 64) == (c // 64)
    diag32 = ((r // 32) == (c // 32)) & causal
    t = lax.broadcasted_iota(jnp.int32, (L, 1), 0)
    if reverse:
        late0, late1 = t < 64, (t % 64) < 32
        a0, a1 = (64,), (32, 96)
        a2 = (31, 63, 95, 127)
        last = 0
    else:
        late0, late1 = t >= 64, (t % 64) >= 32
        a0, a1 = (63,), (31, 95)
        a2 = (0, 32, 64, 96)
        last = L - 1
    for h in range(H_HEADS):
        sl = slice(h * H_DK, (h + 1) * H_DK)
        a = a_all[:, sl]
        q = q_ref[:, sl].astype(F32)
        k = 1.0 - f[:, sl]
        v = v_ref[:, h * H_DV:(h + 1) * H_DV]
        anc0 = jnp.broadcast_to(a[a0[0]:a0[0] + 1, :], (L, H_DK))
        anc1 = jnp.where(t < 64, a[a1[0]:a1[0] + 1, :], a[a1[1]:a1[1] + 1, :])
        anc2 = jnp.where(t < 32, a[a2[0]:a2[0] + 1, :],
                         jnp.where(t < 64, a[a2[1]:a2[1] + 1, :],
                                   jnp.where(t < 96, a[a2[2]:a2[2] + 1, :], a[a2[3]:a2[3] + 1, :])))

        def level(anc, late):
            qs = q * jnp.exp(a - anc)
            ks = k * jnp.exp(anc - a)
            if late is not None:
                qs = jnp.where(late, qs, 0.0)
                ks = jnp.where(late, 0.0, ks)
            return lax.dot_general(qs.astype(BF16), ks.astype(BF16), NT_DIMS, preferred_element_type=F32)

        s = jnp.where(diag32, level(anc2, None), jnp.where(same64, level(anc1, late1), level(anc0, late0)))
        st = s_scr[d, h]
        qd = (q * jnp.exp(a)).astype(BF16)
        o = (jnp.dot(s.astype(BF16), v, preferred_element_type=F32)
             + lax.dot_general(qd, st.astype(BF16), NT_DIMS, preferred_element_type=F32))
        h_ref[:, h * H_DV:(h + 1) * H_DV] = o.astype(BF16)
        a_last = a[last:last + 1, :]
        kd = (k * jnp.exp(a_last - a)).astype(BF16)
        s_scr[d, h] = st * jnp.exp(a_last) + lax.dot_general(v, kd, TN_DIMS, preferred_element_type=F32)


def _hgrn_scan_kernel(qf, vf, ff, qb, vb, fb, bias_ref, lb_ref, s0_ref, hf_ref, hb_ref, st_ref, s_scr):
    n = pl.program_id(1)

    @pl.when(n == 0)
    def _():
        s_scr[...] = s0_ref[0]

    _hgrn_dir(qf, vf, ff, bias_ref, lb_ref, hf_ref, s_scr, False, 0)
    _hgrn_dir(qb, vb, fb, bias_ref, lb_ref, hb_ref, s_scr, True, 1)

    @pl.when(n == pl.num_programs(1) - 1)
    def _():
        st_ref[0] = s_scr[...]


def _hgrn_scan(q, v, f_f, f_b, bias, lb, s0, seg):
    L = CHUNK
    assert seg.seqlen % L == 0
    n = seg.seqlen // L
    fwd = lambda b, i: (b * n + i, 0)
    bwd = lambda b, i: (b * n + (n - 1 - i), 0)
    rows_, w = q.shape
    st = lambda b, i: (b, 0, 0, 0, 0)
    return pl.pallas_call(
        _hgrn_scan_kernel,
        out_shape=[jax.ShapeDtypeStruct((rows_, w), BF16), jax.ShapeDtypeStruct((rows_, w), BF16),
                   jax.ShapeDtypeStruct(s0.shape, F32)],
        grid=(seg.batch, n),
        in_specs=[pl.BlockSpec((L, w), fwd), pl.BlockSpec((L, w), fwd), pl.BlockSpec((L, w), fwd),
                  pl.BlockSpec((L, w), bwd), pl.BlockSpec((L, w), bwd), pl.BlockSpec((L, w), bwd),
                  _full(bias, 2), _full(lb, 2), pl.BlockSpec((1,) + s0.shape[1:], st)],
        out_specs=[pl.BlockSpec((L, w), fwd), pl.BlockSpec((L, w), bwd),
                   pl.BlockSpec((1,) + s0.shape[1:], st)],
        scratch_shapes=[pltpu.VMEM(s0.shape[1:], F32)],
        compiler_params=_cp("arbitrary", "arbitrary"),
        name="hgrn_scan",
    )(q, v, f_f, q, v, f_b, bias, lb, s0)


def _mix_out_col_kernel(hf_ref, hb_ref, o_ref, x_ref, mod_ref, gh_ref, w_ref, g1_ref, out_ref,
                        *, n_heads, rows_per_col):
    mod = mod_ref[0]
    y = _head_out(hf_ref[...], hb_ref[...], o_ref[...], gh_ref[...], w_ref[...], n_heads)
    upd = mod[2:3] * (_rms(y) * g1_ref[...])
    for cl in range(SUBLANES):
        out_ref[:, cl, :] = x_ref[:, cl, :] + upd[cl * rows_per_col:(cl + 1) * rows_per_col, :]


def _mix_out_col(hf, hb, o, x, seg, mod, gh, w, g1, n_heads):
    n, d = x.shape
    grid_rows = seg.seqlen // GRID_W
    tcol = SUBLANES * grid_rows
    ncb = GRID_W // SUBLANES
    x3 = x.reshape(n // GRID_W, GRID_W, d)
    hmap = lambda b, c: (b * ncb + c, 0)
    xmap = lambda b, c: (b, c, 0)
    consts = (gh, w, g1)
    out = pl.pallas_call(
        functools.partial(_mix_out_col_kernel, n_heads=n_heads, rows_per_col=grid_rows),
        out_shape=jax.ShapeDtypeStruct(x3.shape, F32),
        grid=(seg.batch, ncb),
        in_specs=[pl.BlockSpec((tcol, hf.shape[1]), hmap), pl.BlockSpec((tcol, hf.shape[1]), hmap),
                  pl.BlockSpec((tcol, o.shape[1]), hmap),
                  pl.BlockSpec((grid_rows, SUBLANES, d), xmap),
                  pl.BlockSpec((1, N_MOD, d), lambda b, c: (b, 0, 0))] + [_full(a, 2) for a in consts],
        out_specs=pl.BlockSpec((grid_rows, SUBLANES, d), xmap),
        input_output_aliases={3: 0},
        compiler_params=_cp("arbitrary", "arbitrary"),
        name="mix_out_col",
    )(hf, hb, o, x3, mod, *consts)
    return out.reshape(n, d)


def kernel(x, c, ctx, c_ctx, mod_w, mod_b, norm_g, m_w_in, m_b_gate, m_w_conv, m_g_head, m_w_out,
           h_w_in, h_b_f, h_lb_raw, h_g_head, h_w_out, f_w_gu, f_w_down,
           e_w_router, e_b_router, e_w_gu, e_w_down):
    batch, seq, d = x.shape
    ctx_len = ctx.shape[1]
    depth = mod_w.shape[0]
    assert seq % GRID_W == 0 and seq // GRID_W == CHUNK and ctx_len % CHUNK == 0
    assert batch < MOD_ROWS and d % LANES == 0
    lat_seg = _Seg(batch, seq, False)
    ctx_seg = _Seg(batch, ctx_len, True)

    c_all = jnp.zeros((MOD_ROWS, d), F32).at[:batch].set(c).at[batch].set(c_ctx)
    mod_all = _modulation(c_all, mod_w, mod_b).reshape(depth, MOD_ROWS, N_MOD, d)

    lb_all = jax.nn.softmax(h_lb_raw.astype(F32), axis=0)
    lb_all = jnp.cumsum(lb_all, axis=0) - lb_all[0]

    xl = x.reshape(batch * seq, d)
    xc = ctx.reshape(batch * ctx_len, d)
    row = lambda v: v.reshape(1, -1).astype(F32)

    for i in range(depth):
        j = i // 2
        with_ctx = i < depth - 1
        mod = mod_all[i]
        g0, g1, g2, g3 = (row(norm_g[i, s]) for s in range(4))
        if i % 2 == 0:
            qk_w = 2 * M_HEADS * M_DK
            v_w = M_HEADS * M_DV
            w_in = m_w_in[j]
            wqk = w_in[:, :qk_w].astype(BF16)
            wv = w_in[:, qk_w:qk_w + v_w].astype(BF16)
            wo = w_in[:, qk_w + v_w:qk_w + 2 * v_w].astype(BF16)
            ng = 4 * M_HEADS
            wg = jnp.zeros((d, LANES), F32).at[:, :ng].set(w_in[:, qk_w + 2 * v_w:]).astype(BF16)
            gb = jnp.zeros((1, LANES), F32).at[0, :ng].set(m_b_gate[j].reshape(-1))
            col = jnp.arange(LANES)
            gm = (((col // M_HEADS) % 2 == 1) & (col < ng)).astype(F32).reshape(1, LANES)
            pw = (mod, g0, wqk, wv, wo, wg, m_w_conv[j].astype(F32), gb, gm)
            qc, kc, vc, oc, gc = _proj_m(xc, ctx_seg, *pw)
            ql, kl, vl, ol, gl = _proj_m(xl, lat_seg, *pw)
            s0 = jnp.zeros((batch, 2, M_HEADS // 2, LANES, 2 * M_DV), F32)
            m0 = jnp.zeros((batch, 2 * M_HEADS, LANES), F32)
            hcf, hcb, s1, m1 = _mlstm_scan(qc, kc, vc, gc, s0, m0, ctx_seg)
            hlf, hlb, _, _ = _mlstm_scan(ql, kl, vl, gl, s1, m1, lat_seg)
            gh = row(m_g_head[j])
            wout = m_w_out[j].astype(BF16)
            xl = _mix_out(hlf, hlb, ol, xl, lat_seg, mod, gh, wout, g1, M_HEADS)
            if with_ctx:
                xc = _mix_out(hcf, hcb, oc, xc, ctx_seg, mod, gh, wout, g1, M_HEADS)
        else:
            kw = H_HEADS * H_DK
            w_in = h_w_in[j].astype(BF16)
            ws = [w_in[:, s * kw:(s + 1) * kw] for s in range(5)]
            qc, vc, ffc, fbc, gc = _proj_h_ctx(xc, ctx_seg, mod, g0, ws)
            ql, vl, ffl, fbl, gl = _proj_h_lat(xl, lat_seg, mod, g0, ws)
            bias = h_b_f[j].astype(F32)
            lb = lb_all[j]
            s0 = jnp.zeros((batch, 2, H_HEADS, H_DV, H_DK), F32)
            hcf, hcb, s1 = _hgrn_scan(qc, vc, ffc, fbc, bias, lb, s0, ctx_seg)
            hlf, hlb, _ = _hgrn_scan(ql, vl, ffl, fbl, bias, lb, s1, lat_seg)
            gh = row(h_g_head[j])
            wout = h_w_out[j].astype(BF16)
            xl = _mix_out_col(hlf, hlb, gl, xl, lat_seg, mod, gh, wout, g1, H_HEADS)
            if with_ctx:
                xc = _mix_out(hcf, hcb, gc, xc, ctx_seg, mod, gh, wout, g1, H_HEADS)
        streams = [(xl, lat_seg)] + ([(xc, ctx_seg)] if with_ctx else [])
        outs = []
        if i % 2 == 0:
            wgu, wd = f_w_gu[j].astype(BF16), f_w_down[j].astype(BF16)
            for xs, seg in streams:
                outs.append(_ffn(xs, seg, mod, g2, wgu, wd, g3, th=256))
        else:
            wr = jnp.zeros((d, LANES), F32).at[:, :N_EXPERTS].set(e_w_router[j])
            br = jnp.zeros((1, LANES), F32).at[0, :N_EXPERTS].set(e_b_router[j])
            wgu, wd = e_w_gu[j].astype(BF16), e_w_down[j].astype(BF16)
            outs.append(_moe_routed(xl, lat_seg, mod, g2, wr, br, wgu, wd, g3, th=256))
            if with_ctx:
                u, comb = _router(xc, ctx_seg, mod, g2, wr, br)
                outs.append(_moe(u, comb, xc, ctx_seg, mod, wgu, wd, g3, th=256))
        xl = outs[0]
        if with_ctx:
            xc = outs[1]
    return xl.reshape(batch, seq, d)
```

```python
import functools

import jax
import jax.numpy as jnp
from jax import lax
from jax.experimental import pallas as pl
from jax.experimental.pallas import tpu as pltpu

F32 = jnp.float32
BF16 = jnp.bfloat16
EPS = 1e-6
NEG = -1e30

GRID_W = 64
N_MOD = 6
M_HEADS, M_DK, M_DV = 8, 64, 128
H_HEADS, H_DK, H_DV = 8, 128, 128
N_EXPERTS = 8
LANES = 128
SUBLANES = 8
CHUNK = 128
MOD_ROWS = 16
VMEM_LIMIT = 56 * 1024 * 1024

HI = lax.Precision.HIGHEST
NT_DIMS = (((1,), (1,)), ((), ()))
TN_DIMS = (((0,), (0,)), ((), ()))


def _cp(*sem):
    return pltpu.CompilerParams(dimension_semantics=sem, vmem_limit_bytes=VMEM_LIMIT)


def _full(a, nargs):
    zeros = (0,) * a.ndim
    return pl.BlockSpec(a.shape, lambda *_: zeros)


def _sigmoid(x):
    return 1.0 / (1.0 + jnp.exp(-x))


def _silu(x):
    return x * _sigmoid(x)


def _log_sigmoid(x):
    return jnp.minimum(x, 0.0) - jnp.log(1.0 + jnp.exp(-jnp.abs(x)))


def _rms(x):
    return x * lax.rsqrt(jnp.mean(x * x, axis=-1, keepdims=True) + EPS)


def _normmod(x, g, mod, s):
    return _rms(x) * (g * (1.0 + mod[s + 1:s + 2])) + mod[s:s + 1]


def _split2(x):
    hi = x.astype(BF16)
    lo = (x - hi.astype(F32)).astype(BF16)
    return hi, lo


def _cumsum_mat(n, reverse):
    r = lax.broadcasted_iota(jnp.int32, (n, n), 0)
    c = lax.broadcasted_iota(jnp.int32, (n, n), 1)
    return jnp.where((c >= r) if reverse else (c <= r), 1.0, 0.0).astype(BF16)


def _cumsum_time(x, tri):
    hi, lo = _split2(x)
    return (jnp.dot(tri, hi, preferred_element_type=F32)
            + jnp.dot(tri, lo, preferred_element_type=F32))


class _Seg:
    def __init__(self, batch, seqlen, is_ctx):
        self.batch, self.seqlen, self.is_ctx = batch, seqlen, is_ctx
        self.n = batch * seqlen

    def tile(self, pref):
        tm = min(pref, self.n if self.is_ctx else self.seqlen)
        assert self.n % tm == 0 and (self.seqlen % tm == 0 or tm % self.seqlen == 0)
        return tm

    def mod_spec(self, tm, d, nargs):
        if self.is_ctx:
            row = lambda j: self.batch
        else:
            row = lambda j: (j * tm) // self.seqlen
        if nargs == 1:
            return pl.BlockSpec((1, N_MOD, d), lambda j: (row(j), 0, 0))
        if nargs == 2:
            return pl.BlockSpec((1, N_MOD, d), lambda j, k: (row(j), 0, 0))
        return pl.BlockSpec((1, N_MOD, d), lambda j, e, k: (row(j), 0, 0))


def _mod_kernel(c_ref, w_ref, b_ref, o_ref):
    c = c_ref[...]
    o_ref[0] = jnp.dot(_silu(c), w_ref[0], preferred_element_type=F32, precision=HI) + b_ref[0]


def _modulation(c_all, mod_w, mod_b):
    depth, d, n = mod_w.shape
    tn = n // 4
    return pl.pallas_call(
        _mod_kernel,
        out_shape=jax.ShapeDtypeStruct((depth, MOD_ROWS, n), F32),
        grid=(depth, n // tn),
        in_specs=[pl.BlockSpec((MOD_ROWS, d), lambda i, j: (0, 0)),
                  pl.BlockSpec((1, d, tn), lambda i, j: (i, 0, j)),
                  pl.BlockSpec((1, 1, tn), lambda i, j: (i, 0, j))],
        out_specs=pl.BlockSpec((1, MOD_ROWS, tn), lambda i, j: (i, 0, j)),
        compiler_params=_cp("arbitrary", "arbitrary"),
        name="modulation",
    )(c_all, mod_w, mod_b.reshape(depth, 1, n))


def _proj_m_kernel(xp_ref, x_ref, xn_ref, mod_ref, g_ref, wqk_ref, wv_ref, wo_ref, wg_ref,
                   cw_ref, gb_ref, gm_ref, q_ref, kt_ref, v_ref, o_ref, gt_ref, *, tm, seqlen):
    j = pl.program_id(0)
    mod = mod_ref[0]
    g = g_ref[...]
    u = _normmod(x_ref[...], g, mod, 0).astype(BF16)
    uh = _normmod(jnp.concatenate([xp_ref[...], xn_ref[...]], axis=0), g, mod, 0).astype(BF16)
    wqk = wqk_ref[...]
    z = jnp.dot(u, wqk, preferred_element_type=F32)
    zh = jnp.dot(uh, wqk, preferred_element_type=F32)
    local = lax.broadcasted_iota(jnp.int32, (tm, 1), 0)
    pos = lax.rem(j * tm + local, seqlen)
    zp = jnp.where(local == 0, zh[SUBLANES - 1:SUBLANES], pltpu.roll(z, 1, 0))
    zn = jnp.where(local == tm - 1, zh[SUBLANES:SUBLANES + 1], pltpu.roll(z, tm - 1, 0))
    zp = jnp.where(pos == 0, 0.0, zp)
    zn = jnp.where(pos == seqlen - 1, 0.0, zn)
    cw = cw_ref[...]
    a = _silu(cw[0:1] * zp + cw[1:2] * z + cw[2:3] * zn)
    half = a.shape[1] // 2
    q_ref[...] = a[:, :half].astype(BF16)
    kt_ref[...] = (a[:, half:] * (M_DK ** -0.5)).T.astype(BF16)
    v_ref[...] = jnp.dot(u, wv_ref[...], preferred_element_type=F32).astype(BF16)
    o_ref[...] = jnp.dot(u, wo_ref[...], preferred_element_type=F32).astype(BF16)
    zg = jnp.dot(u, wg_ref[...], preferred_element_type=F32) + gb_ref[...]
    gt_ref[...] = jnp.where(gm_ref[...] > 0.5, _log_sigmoid(zg), zg)


def _proj_m(x, seg, mod, g, wqk, wv, wo, wg, cw, gb, gm):
    n, d = x.shape
    tm = seg.tile(512)
    nb8 = n // SUBLANES
    r8 = tm // SUBLANES
    kern = functools.partial(_proj_m_kernel, tm=tm, seqlen=seg.seqlen)
    nqk = wqk.shape[1]
    consts = (g, wqk, wv, wo, wg, cw, gb, gm)
    return pl.pallas_call(
        kern,
        out_shape=[jax.ShapeDtypeStruct((n, nqk // 2), BF16),
                   jax.ShapeDtypeStruct((nqk // 2, n), BF16),
                   jax.ShapeDtypeStruct((n, wv.shape[1]), BF16),
                   jax.ShapeDtypeStruct((n, wo.shape[1]), BF16),
                   jax.ShapeDtypeStruct((n, LANES), F32)],
        grid=(n // tm,),
        in_specs=[pl.BlockSpec((SUBLANES, d), lambda j: (jnp.maximum(j * r8 - 1, 0), 0)),
                  pl.BlockSpec((tm, d), lambda j: (j, 0)),
                  pl.BlockSpec((SUBLANES, d), lambda j: (jnp.minimum((j + 1) * r8, nb8 - 1), 0)),
                  seg.mod_spec(tm, d, 1)] + [_full(a, 1) for a in consts],
        out_specs=[pl.BlockSpec((tm, nqk // 2), lambda j: (j, 0)),
                   pl.BlockSpec((nqk // 2, tm), lambda j: (0, j)),
                   pl.BlockSpec((tm, wv.shape[1]), lambda j: (j, 0)),
                   pl.BlockSpec((tm, wo.shape[1]), lambda j: (j, 0)),
                   pl.BlockSpec((tm, LANES), lambda j: (j, 0))],
        compiler_params=_cp("arbitrary"),
        name="mlstm_proj",
    )(x, x, x, mod, *consts)


def _mlstm_dir(q_ref, kt_ref, v_ref, g_ref, h_ref, s_scr, m_scr, reverse, d):
    L = CHUNK
    g = g_ref[...]
    b = _cumsum_time(g, _cumsum_mat(L, reverse))
    gT = g.T
    bT = b.T
    r = lax.broadcasted_iota(jnp.int32, (L, L), 0)
    c = lax.broadcasted_iota(jnp.int32, (L, L), 1)
    causal = (c >= r) if reverse else (c <= r)
    last = 0 if reverse else L - 1
    c_ig = 2 * M_HEADS * d
    c_lf = c_ig + M_HEADS
    lane = lax.broadcasted_iota(jnp.int32, (1, LANES), 1)
    ones = jnp.ones((L, M_DV), BF16)
    for p in range(M_HEADS // 2):
        q_p = q_ref[:, p * LANES:(p + 1) * LANES]
        kt_p = kt_ref[p * LANES:(p + 1) * LANES, :]
        s_bf = s_scr[d, p].astype(BF16)
        for e in range(2):
            h = 2 * p + e
            rows = slice(e * M_DK, (e + 1) * M_DK)
            q_e = jnp.where((lane // M_DK) == e, q_p, jnp.zeros_like(q_p))
            b_col = b[:, c_lf + h:c_lf + h + 1]
            beta = gT[c_ig + h:c_ig + h + 1, :] - bT[c_lf + h:c_lf + h + 1, :]
            m_prev = m_scr[d * M_HEADS + h:d * M_HEADS + h + 1, 0:1]
            bm = jnp.where(causal, beta, NEG)
            mu = jnp.maximum(m_prev, jnp.max(bm, axis=1, keepdims=True))
            qk = jnp.dot(q_e, kt_p, preferred_element_type=F32)
            s = (qk * jnp.exp(bm - mu)).astype(BF16)
            vp = jnp.concatenate([v_ref[:, h * M_DV:(h + 1) * M_DV], ones], axis=1)
            tot = (jnp.dot(s, vp, preferred_element_type=F32)
                   + jnp.exp(m_prev - mu) * jnp.dot(q_e, s_bf, preferred_element_type=F32))
            den = jnp.maximum(jnp.abs(tot[:, M_DV:]), jnp.exp(-(b_col + mu)))
            h_ref[:, h * M_DV:(h + 1) * M_DV] = (tot[:, :M_DV] / den).astype(BF16)
            mu_last = mu[last:last + 1, :]
            ktw = (kt_p[rows, :].astype(F32) * jnp.exp(beta - mu_last)).astype(BF16)
            s_scr[d, p, rows, :] = (jnp.exp(m_prev - mu_last) * s_scr[d, p, rows, :]
                                    + jnp.dot(ktw, vp, preferred_element_type=F32))
            m_new = b_col[last:last + 1, :] + mu_last
            m_scr[d * M_HEADS + h:d * M_HEADS + h + 1, :] = jnp.broadcast_to(m_new, (1, LANES))


def _mlstm_scan_kernel(qf, kf, vf, gf, qb, kb, vb, gb, s0_ref, m0_ref,
                       hf_ref, hb_ref, st_ref, mt_ref, s_scr, m_scr):
    n = pl.program_id(1)

    @pl.when(n == 0)
    def _():
        s_scr[...] = s0_ref[0]
        m_scr[...] = m0_ref[0]

    _mlstm_dir(qf, kf, vf, gf, hf_ref, s_scr, m_scr, False, 0)
    _mlstm_dir(qb, kb, vb, gb, hb_ref, s_scr, m_scr, True, 1)

    @pl.when(n == pl.num_programs(1) - 1)
    def _():
        st_ref[0] = s_scr[...]
        mt_ref[0] = m_scr[...]


def _mlstm_scan(q, kt, v, gates, s0, m0, seg):
    L = CHUNK
    assert seg.seqlen % L == 0
    n = seg.seqlen // L
    fwd = lambda b, i: (b * n + i, 0)
    bwd = lambda b, i: (b * n + (n - 1 - i), 0)
    fwd_t = lambda b, i: (0, b * n + i)
    bwd_t = lambda b, i: (0, b * n + (n - 1 - i))
    dq, dv = q.shape[1], v.shape[1]
    st = lambda b, i: (b, 0, 0, 0, 0)
    mt = lambda b, i: (b, 0, 0)
    return pl.pallas_call(
        _mlstm_scan_kernel,
        out_shape=[jax.ShapeDtypeStruct((seg.n, dv), BF16), jax.ShapeDtypeStruct((seg.n, dv), BF16),
                   jax.ShapeDtypeStruct(s0.shape, F32), jax.ShapeDtypeStruct(m0.shape, F32)],
        grid=(seg.batch, n),
        in_specs=[pl.BlockSpec((L, dq), fwd), pl.BlockSpec((dq, L), fwd_t), pl.BlockSpec((L, dv), fwd),
                  pl.BlockSpec((L, LANES), fwd),
                  pl.BlockSpec((L, dq), bwd), pl.BlockSpec((dq, L), bwd_t), pl.BlockSpec((L, dv), bwd),
                  pl.BlockSpec((L, LANES), bwd),
                  pl.BlockSpec((1,) + s0.shape[1:], st), pl.BlockSpec((1,) + m0.shape[1:], mt)],
        out_specs=[pl.BlockSpec((L, dv), fwd), pl.BlockSpec((L, dv), bwd),
                   pl.BlockSpec((1,) + s0.shape[1:], st), pl.BlockSpec((1,) + m0.shape[1:], mt)],
        scratch_shapes=[pltpu.VMEM(s0.shape[1:], F32), pltpu.VMEM(m0.shape[1:], F32)],
        compiler_params=_cp("arbitrary", "arbitrary"),
        name="mlstm_scan",
    )(q, kt, v, gates, q, kt, v, gates, s0, m0)


def _head_out(hf, hb, gate_pre, gh, w, n_heads):
    h = hf.astype(F32) + hb.astype(F32)
    dv = h.shape[1] // n_heads
    parts = [_rms(h[:, i * dv:(i + 1) * dv]) for i in range(n_heads)]
    hn = jnp.concatenate(parts, axis=1) * gh * _sigmoid(gate_pre.astype(F32))
    return jnp.dot(hn.astype(BF16), w, preferred_element_type=F32)


def _mix_out_kernel(hf_ref, hb_ref, o_ref, x_ref, mod_ref, gh_ref, w_ref, g1_ref, out_ref, *, n_heads):
    mod = mod_ref[0]
    y = _head_out(hf_ref[...], hb_ref[...], o_ref[...], gh_ref[...], w_ref[...], n_heads)
    out_ref[...] = x_ref[...] + mod[2:3] * (_rms(y) * g1_ref[...])


def _mix_out(hf, hb, o, x, seg, mod, gh, w, g1, n_heads, in_place=True):
    n, d = x.shape
    tm = seg.tile(512)
    rmap = lambda j: (j, 0)
    consts = (gh, w, g1)
    return pl.pallas_call(
        functools.partial(_mix_out_kernel, n_heads=n_heads),
        out_shape=jax.ShapeDtypeStruct(x.shape, F32),
        grid=(n // tm,),
        in_specs=[pl.BlockSpec((tm, hf.shape[1]), rmap), pl.BlockSpec((tm, hf.shape[1]), rmap),
                  pl.BlockSpec((tm, o.shape[1]), rmap), pl.BlockSpec((tm, d), rmap),
                  seg.mod_spec(tm, d, 1)] + [_full(a, 1) for a in consts],
        out_specs=pl.BlockSpec((tm, d), rmap),
        input_output_aliases={3: 0} if in_place else {},
        compiler_params=_cp("arbitrary"),
        name="mix_out",
    )(hf, hb, o, x, mod, *consts)


def _swiglu_acc(u_scr, wgu_ref, wd_ref, acc_scr, th):
    hd = wd_ref.shape[0]
    assert hd % th == 0 and hd % LANES == 0

    def chunk(k, carry):
        off = pl.multiple_of(k * th, th)
        u = u_scr[...]
        gg = jnp.dot(u, wgu_ref[:, pl.ds(off, th)], preferred_element_type=F32)
        up = jnp.dot(u, wgu_ref[:, pl.ds(pl.multiple_of(hd + off, LANES), th)], preferred_element_type=F32)
        hh = (_silu(gg) * up).astype(BF16)
        acc_scr[...] += jnp.dot(hh, wd_ref[pl.ds(off, th), :], preferred_element_type=F32)
        return carry

    acc_scr[...] = jnp.zeros_like(acc_scr)
    lax.fori_loop(0, hd // th, chunk, 0)


def _ffn_kernel(x_ref, mod_ref, g2_ref, wgu_ref, wd_ref, g3_ref, out_ref, u_scr, acc_scr, *, th):
    u_scr[...] = _normmod(x_ref[...], g2_ref[...], mod_ref[0], 3).astype(BF16)
    _swiglu_acc(u_scr, wgu_ref, wd_ref, acc_scr, th)
    mod = mod_ref[0]
    out_ref[...] = x_ref[...] + mod[5:6] * (_rms(acc_scr[...]) * g3_ref[...])


def _resident(a, index_map):
    return pl.BlockSpec(a, index_map, pipeline_mode=pl.Buffered(1))


def _ffn(x, seg, mod, g2, wgu, wd, g3, th):
    n, d = x.shape
    tm = seg.tile(1024)
    return pl.pallas_call(
        functools.partial(_ffn_kernel, th=th),
        out_shape=jax.ShapeDtypeStruct((n, d), F32),
        grid=(n // tm,),
        in_specs=[pl.BlockSpec((tm, d), lambda i: (i, 0)),
                  seg.mod_spec(tm, d, 1),
                  _full(g2, 1),
                  _resident(wgu.shape, lambda i: (0, 0)),
                  _resident(wd.shape, lambda i: (0, 0)),
                  _full(g3, 1)],
        out_specs=pl.BlockSpec((tm, d), lambda i: (i, 0)),
        scratch_shapes=[pltpu.VMEM((tm, d), BF16), pltpu.VMEM((tm, d), F32)],
        input_output_aliases={0: 0},
        compiler_params=_cp("arbitrary"),
        name="ffn",
    )(x, mod, g2, wgu, wd, g3)


def _router_kernel(x_ref, mod_ref, g2_ref, wr_ref, br_ref, u_ref, comb_ref):
    u = _normmod(x_ref[...], g2_ref[...], mod_ref[0], 3)
    u_ref[...] = u.astype(BF16)
    lane = lax.broadcasted_iota(jnp.int32, (1, LANES), 1)
    logits = jnp.dot(u, wr_ref[...], preferred_element_type=F32, precision=HI) + br_ref[...]
    logits = jnp.where(lane < N_EXPERTS, logits, NEG)
    m1 = jnp.max(logits, axis=1, keepdims=True)
    i1 = jnp.min(jnp.where(logits == m1, lane, LANES), axis=1, keepdims=True)
    rest = jnp.where(lane == i1, NEG, logits)
    m2 = jnp.max(rest, axis=1, keepdims=True)
    i2 = jnp.min(jnp.where(rest == m2, lane, LANES), axis=1, keepdims=True)
    e2 = jnp.exp(m2 - m1)
    w1 = 1.0 / (1.0 + e2)
    comb_ref[...] = jnp.where(lane == i1, w1, 0.0) + jnp.where(lane == i2, e2 * w1, 0.0)


def _router(x, seg, mod, g2, wr, br):
    n, d = x.shape
    tm = seg.tile(1024)
    consts = (g2, wr, br)
    return pl.pallas_call(
        _router_kernel,
        out_shape=[jax.ShapeDtypeStruct((n, d), BF16), jax.ShapeDtypeStruct((n, LANES), F32)],
        grid=(n // tm,),
        in_specs=[pl.BlockSpec((tm, d), lambda i: (i, 0)), seg.mod_spec(tm, d, 1)]
                 + [_full(a, 1) for a in consts],
        out_specs=[pl.BlockSpec((tm, d), lambda i: (i, 0)), pl.BlockSpec((tm, LANES), lambda i: (i, 0))],
        compiler_params=_cp("arbitrary"),
        name="router",
    )(x, mod, *consts)


def _moe_kernel(u_ref, comb_ref, x_ref, mod_ref, wg_ref, wu_ref, wd_ref, g3_ref, out_ref, acc_scr):
    e = pl.program_id(1)
    kk = pl.program_id(2)

    @pl.when((e == 0) & (kk == 0))
    def _():
        acc_scr[...] = jnp.zeros_like(acc_scr)

    u = u_ref[...]
    gg = jnp.dot(u, wg_ref[0], preferred_element_type=F32)
    up = jnp.dot(u, wu_ref[0], preferred_element_type=F32)
    hh = (_silu(gg) * up).astype(BF16)
    lane = lax.broadcasted_iota(jnp.int32, (1, LANES), 1)
    cw = jnp.sum(jnp.where(lane == e, comb_ref[...], 0.0), axis=1, keepdims=True)
    acc_scr[...] += cw * jnp.dot(hh, wd_ref[0], preferred_element_type=F32)

    @pl.when((e == pl.num_programs(1) - 1) & (kk == pl.num_programs(2) - 1))
    def _():
        mod = mod_ref[0]
        out_ref[...] = x_ref[...] + mod[5:6] * (_rms(acc_scr[...]) * g3_ref[...])


def _moe(u, comb, x, seg, mod, wgu, wd, g3, th):
    n, d = x.shape
    tm = seg.tile(1024)
    ne, hd, _ = wd.shape
    nk = hd // th
    return pl.pallas_call(
        _moe_kernel,
        out_shape=jax.ShapeDtypeStruct((n, d), F32),
        grid=(n // tm, ne, nk),
        in_specs=[pl.BlockSpec((tm, d), lambda i, e, k: (i, 0)),
                  pl.BlockSpec((tm, LANES), lambda i, e, k: (i, 0)),
                  pl.BlockSpec((tm, d), lambda i, e, k: (i, 0)),
                  seg.mod_spec(tm, d, 3),
                  pl.BlockSpec((1, d, th), lambda i, e, k: (e, 0, k)),
                  pl.BlockSpec((1, d, th), lambda i, e, k: (e, 0, nk + k)),
                  pl.BlockSpec((1, th, d), lambda i, e, k: (e, k, 0)),
                  _full(g3, 3)],
        out_specs=pl.BlockSpec((tm, d), lambda i, e, k: (i, 0)),
        scratch_shapes=[pltpu.VMEM((tm, d), F32)],
        input_output_aliases={2: 0},
        compiler_params=_cp("arbitrary", "arbitrary", "arbitrary"),
        name="moe",
    )(u, comb, x, mod, wgu, wgu, wd, g3)


ROW_TILE = 1024
META_E, META_RANK, META_W = 0, 2, 4


def _route_kernel(x_ref, mod_ref, g2_ref, wr_ref, br_ref, meta_ref, cnt_ref, tri_scr, run_scr):
    i = pl.program_id(0)
    tm = x_ref.shape[0]

    @pl.when(i == 0)
    def _():
        r = lax.broadcasted_iota(jnp.int32, (tm, tm), 0)
        c = lax.broadcasted_iota(jnp.int32, (tm, tm), 1)
        tri_scr[...] = jnp.where(c < r, 1.0, 0.0).astype(BF16)
        run_scr[...] = jnp.zeros_like(run_scr)

    u = _normmod(x_ref[...], g2_ref[...], mod_ref[0], 3)
    lane = lax.broadcasted_iota(jnp.int32, (1, LANES), 1)
    logits = jnp.dot(u, wr_ref[...], preferred_element_type=F32, precision=HI) + br_ref[...]
    logits = jnp.where(lane < N_EXPERTS, logits, NEG)
    m1 = jnp.max(logits, axis=1, keepdims=True)
    i1 = jnp.min(jnp.where(logits == m1, lane, LANES), axis=1, keepdims=True)
    rest = jnp.where(lane == i1, NEG, logits)
    m2 = jnp.max(rest, axis=1, keepdims=True)
    i2 = jnp.min(jnp.where(rest == m2, lane, LANES), axis=1, keepdims=True)
    e2 = jnp.exp(m2 - m1)
    w1 = 1.0 / (1.0 + e2)
    sel = ((lane == i1) | (lane == i2))
    onehot = jnp.where(sel, 1.0, 0.0)
    before = jnp.dot(tri_scr[...], onehot.astype(BF16), preferred_element_type=F32) + run_scr[...]
    rank1 = jnp.sum(jnp.where(lane == i1, before, 0.0), axis=1, keepdims=True)
    rank2 = jnp.sum(jnp.where(lane == i2, before, 0.0), axis=1, keepdims=True)
    run_scr[...] = run_scr[...] + jnp.sum(onehot, axis=0, keepdims=True)
    vals = (i1.astype(F32), i2.astype(F32), rank1, rank2, w1, e2 * w1)
    meta = jnp.zeros((tm, LANES), F32)
    for col, val in enumerate(vals):
        meta = jnp.where(lane == col, val, meta)
    meta_ref[...] = meta
    cnt_ref[...] = jnp.broadcast_to(run_scr[...], cnt_ref.shape)


def _route(x, seg, mod, g2, wr, br):
    n, d = x.shape
    tm = seg.tile(1024)
    consts = (g2, wr, br)
    return pl.pallas_call(
        _route_kernel,
        out_shape=[jax.ShapeDtypeStruct((n, LANES), F32),
                   jax.ShapeDtypeStruct((SUBLANES, LANES), F32)],
        grid=(n // tm,),
        in_specs=[pl.BlockSpec((tm, d), lambda i: (i, 0)), seg.mod_spec(tm, d, 1)]
                 + [_full(a, 1) for a in consts],
        out_specs=[pl.BlockSpec((tm, LANES), lambda i: (i, 0)),
                   pl.BlockSpec((SUBLANES, LANES), lambda i: (0, 0))],
        scratch_shapes=[pltpu.VMEM((tm, tm), BF16), pltpu.VMEM((1, LANES), F32)],
        compiler_params=_cp("arbitrary"),
        name="route",
    )(x, mod, *consts)


def _drain(src_ref, dst_ref, sem, count):
    def body(i, carry):
        pltpu.make_async_copy(src_ref, dst_ref, sem).wait()
        return carry

    lax.fori_loop(0, count, body, 0, unroll=8)


def _row_tile(buf, t):
    return buf.at[pl.ds(pl.multiple_of(t * SUBLANES, SUBLANES), SUBLANES), :]


def _dispatch_kernel(dst_ref, pad_ref, x_ref, mod_ref, g2_ref, xs_ref, ubuf, sem):
    j = pl.program_id(0)
    tm = x_ref.shape[0]
    u = _normmod(x_ref[...], g2_ref[...], mod_ref[0], 3)
    for s in range(SUBLANES):
        ubuf[pl.ds(s, tm, stride=SUBLANES), :] = u[:, s * LANES:(s + 1) * LANES]

    def issue(t, carry):
        pltpu.make_async_copy(_row_tile(ubuf, t), xs_ref.at[dst_ref[0, 0, t]], sem).start(priority=0)
        pltpu.make_async_copy(_row_tile(ubuf, t), xs_ref.at[dst_ref[0, 0, tm + t]], sem).start(priority=1)
        return carry

    lax.fori_loop(0, tm, issue, 0, unroll=8)
    npad = pad_ref.shape[2]
    first = _row_tile(ubuf, 0)

    @pl.when(j == 0)
    def _():
        def issue_pad(p, carry):
            pltpu.make_async_copy(first, xs_ref.at[pad_ref[0, 0, p]], sem).start()
            return carry

        lax.fori_loop(0, npad, issue_pad, 0, unroll=8)
        _drain(first, xs_ref.at[0], sem, npad)

    _drain(first, xs_ref.at[0], sem, 2 * tm)


def _dispatch(x, seg, mod, g2, dest, pad_dst, n_rows):
    n, d = x.shape
    tm = seg.tile(512)
    steps = n // tm
    dst = dest.reshape(steps, tm, 2).transpose(0, 2, 1).reshape(steps, 1, 2 * tm)
    npad = pad_dst.shape[0]
    return pl.pallas_call(
        _dispatch_kernel,
        out_shape=jax.ShapeDtypeStruct((n_rows, SUBLANES, d // SUBLANES), F32),
        grid=(steps,),
        in_specs=[pl.BlockSpec((1, 1, 2 * tm), lambda j: (j, 0, 0), memory_space=pltpu.SMEM),
                  pl.BlockSpec((1, 1, npad), lambda j: (0, 0, 0), memory_space=pltpu.SMEM),
                  pl.BlockSpec((tm, d), lambda j: (j, 0)), seg.mod_spec(tm, d, 1), _full(g2, 1)],
        out_specs=pl.BlockSpec(memory_space=pl.ANY),
        scratch_shapes=[pltpu.VMEM((tm * SUBLANES, d // SUBLANES), F32), pltpu.SemaphoreType.DMA(())],
        compiler_params=_cp("arbitrary"),
        name="dispatch",
    )(dst, pad_dst.reshape(1, 1, npad), x, mod, g2)


def _experts_kernel(te_ref, used_ref, xs_ref, wgu_ref, wd_ref, y_ref, u_scr, acc_scr, *, th):
    i = pl.program_id(0)

    @pl.when(i < used_ref[0])
    def _():
        tr = u_scr.shape[0]
        for s in range(SUBLANES):
            u_scr[:, s * LANES:(s + 1) * LANES] = xs_ref[pl.ds(s, tr, stride=SUBLANES), :].astype(BF16)
        _swiglu_acc(u_scr, wgu_ref.at[0], wd_ref.at[0], acc_scr, th)
        y = acc_scr[...]
        for s in range(SUBLANES):
            y_ref[pl.ds(s, tr, stride=SUBLANES), :] = y[:, s * LANES:(s + 1) * LANES]


def _experts(xs3, tile_expert, used, wgu, wd, th):
    tr = ROW_TILE
    n_rows = xs3.shape[0]
    ne, hd, d = wd.shape
    tiles = tile_expert.shape[0]
    assert tiles * tr <= n_rows
    row = lambda i, te, us: (jnp.minimum(i, us[0] - 1), 0)
    grid_spec = pltpu.PrefetchScalarGridSpec(
        num_scalar_prefetch=2,
        grid=(tiles,),
        in_specs=[pl.BlockSpec((tr * SUBLANES, LANES), row),
                  _resident((1,) + wgu.shape[1:], lambda i, te, us: (te[i], 0, 0)),
                  _resident((1,) + wd.shape[1:], lambda i, te, us: (te[i], 0, 0))],
        out_specs=pl.BlockSpec((tr * SUBLANES, LANES), row),
        scratch_shapes=[pltpu.VMEM((tr, d), BF16), pltpu.VMEM((tr, d), F32)],
    )
    y = pl.pallas_call(
        functools.partial(_experts_kernel, th=th),
        out_shape=jax.ShapeDtypeStruct((tiles * tr * SUBLANES, LANES), F32),
        grid_spec=grid_spec,
        compiler_params=_cp("arbitrary"),
        name="experts",
    )(tile_expert, used, xs3.reshape(n_rows * SUBLANES, LANES), wgu, wd)
    return y.reshape(tiles * tr, SUBLANES, LANES)


def _combine_kernel(src_ref, y_ref, meta_ref, x_ref, mod_ref, g3_ref, out_ref, ybuf, sem):
    tm = x_ref.shape[0]

    def issue(t, carry):
        pltpu.make_async_copy(y_ref.at[src_ref[0, 0, t]], _row_tile(ybuf.at[0], t), sem).start(priority=0)
        pltpu.make_async_copy(y_ref.at[src_ref[0, 0, tm + t]], _row_tile(ybuf.at[1], t), sem).start(priority=1)
        return carry

    lax.fori_loop(0, tm, issue, 0, unroll=8)
    _drain(y_ref.at[0], _row_tile(ybuf.at[0], 0), sem, 2 * tm)
    meta = meta_ref[...]
    w1 = meta[:, META_W:META_W + 1]
    w2 = meta[:, META_W + 1:META_W + 2]
    ys = [w1 * ybuf[0, pl.ds(s, tm, stride=SUBLANES), :] + w2 * ybuf[1, pl.ds(s, tm, stride=SUBLANES), :]
          for s in range(SUBLANES)]
    ss = ys[0] * ys[0]
    for y in ys[1:]:
        ss = ss + y * y
    d = x_ref.shape[1]
    r = lax.rsqrt(jnp.sum(ss, axis=1, keepdims=True) / d + EPS)
    mod = mod_ref[0]
    for s, y in enumerate(ys):
        sl = slice(s * LANES, (s + 1) * LANES)
        out_ref[:, sl] = x_ref[:, sl] + mod[5:6, sl] * (y * r * g3_ref[:, sl])


def _combine(y3, dest, meta, x, seg, mod, g3):
    n, d = x.shape
    tm = seg.tile(512)
    steps = n // tm
    src = dest.reshape(steps, tm, 2).transpose(0, 2, 1).reshape(steps, 1, 2 * tm)
    return pl.pallas_call(
        _combine_kernel,
        out_shape=jax.ShapeDtypeStruct((n, d), F32),
        grid=(steps,),
        in_specs=[pl.BlockSpec((1, 1, 2 * tm), lambda j: (j, 0, 0), memory_space=pltpu.SMEM),
                  pl.BlockSpec(memory_space=pl.ANY),
                  pl.BlockSpec((tm, LANES), lambda j: (j, 0)),
                  pl.BlockSpec((tm, d), lambda j: (j, 0)),
                  seg.mod_spec(tm, d, 1), _full(g3, 1)],
        out_specs=pl.BlockSpec((tm, d), lambda j: (j, 0)),
        scratch_shapes=[pltpu.VMEM((2, tm * SUBLANES, d // SUBLANES), F32), pltpu.SemaphoreType.DMA(())],
        input_output_aliases={3: 0},
        compiler_params=_cp("arbitrary"),
        name="combine",
    )(src, y3, meta, x, mod, g3)


def _moe_routed(x, seg, mod, g2, wr, br, wgu, wd, g3, th):
    n, d = x.shape
    assert d == SUBLANES * LANES
    tr = ROW_TILE
    meta, cnt = _route(x, seg, mod, g2, wr, br)
    counts = cnt[0, :N_EXPERTS].astype(jnp.int32)
    sizes = ((counts + tr - 1) // tr) * tr
    ends = jnp.cumsum(sizes)
    offs = ends - sizes
    mi = meta[:, :META_W].astype(jnp.int32)
    eidx = mi[:, META_E:META_E + 2]
    dest = jnp.sum(jnp.where(eidx[:, :, None] == jnp.arange(N_EXPERTS)[None, None, :],
                             offs[None, None, :], 0), axis=2) + mi[:, META_RANK:META_RANK + 2]
    dest = dest.astype(jnp.int32)
    tiles = (2 * n) // tr + N_EXPERTS
    n_rows = tiles * tr
    used = (ends[-1] // tr).astype(jnp.int32).reshape(1)
    tile_start = jnp.arange(tiles, dtype=jnp.int32) * tr
    tile_expert = jnp.sum(tile_start[:, None] >= ends[None, :], axis=1).astype(jnp.int32)
    last_expert = jnp.sum(jnp.maximum(ends[-1] - tr, 0) >= ends).astype(jnp.int32)
    tile_expert = jnp.where(tile_start < ends[-1], tile_expert, last_expert)
    pad = jnp.arange(N_EXPERTS * tr, dtype=jnp.int32).reshape(N_EXPERTS, tr)
    pad_row = offs[:, None] + counts[:, None] + jnp.arange(tr, dtype=jnp.int32)[None, :]
    pad_dst = jnp.where(pad_row < ends[:, None], pad_row, n_rows + pad).reshape(-1).astype(jnp.int32)
    xs3 = _dispatch(x, seg, mod, g2, dest, pad_dst, n_rows + N_EXPERTS * tr)
    y3 = _experts(xs3, tile_expert, used, wgu, wd, th)
    return _combine(y3, dest, meta, x, seg, mod, g3)


def _proj_h_body(u, w_refs, out_refs):
    for w_ref, o_ref in zip(w_refs, out_refs):
        o_ref[...] = jnp.dot(u, w_ref[...], preferred_element_type=F32).astype(BF16)


def _proj_h_lat_kernel(x_ref, mod_ref, g_ref, *refs, rows_per_col):
    n = len(refs) // 2
    w_refs, out_refs = refs[:n], refs[n:]
    x = jnp.swapaxes(x_ref[...], 0, 1)
    x = x.reshape(SUBLANES * rows_per_col, x.shape[2])
    u = _normmod(x, g_ref[...], mod_ref[0], 0).astype(BF16)
    _proj_h_body(u, w_refs, out_refs)


def _proj_h_ctx_kernel(x_ref, mod_ref, g_ref, *refs):
    n = len(refs) // 2
    u = _normmod(x_ref[...], g_ref[...], mod_ref[0], 0).astype(BF16)
    _proj_h_body(u, refs[:n], refs[n:])


def _proj_h_lat(x, seg, mod, g, ws):
    n, d = x.shape
    grid_rows = seg.seqlen // GRID_W
    x3 = x.reshape(n // GRID_W, GRID_W, d)
    tcol = SUBLANES * grid_rows
    ncb = GRID_W // SUBLANES
    return pl.pallas_call(
        functools.partial(_proj_h_lat_kernel, rows_per_col=grid_rows),
        out_shape=[jax.ShapeDtypeStruct((n, w.shape[1]), BF16) for w in ws],
        grid=(seg.batch, ncb),
        in_specs=[pl.BlockSpec((grid_rows, SUBLANES, d), lambda b, c: (b, c, 0)),
                  pl.BlockSpec((1, N_MOD, d), lambda b, c: (b, 0, 0)), _full(g, 2)]
                 + [_full(w, 2) for w in ws],
        out_specs=[pl.BlockSpec((tcol, w.shape[1]), lambda b, c: (b * ncb + c, 0)) for w in ws],
        compiler_params=_cp("arbitrary", "arbitrary"),
        name="hgrn_proj_lat",
    )(x3, mod, g, *ws)


def _proj_h_ctx(x, seg, mod, g, ws):
    n, d = x.shape
    tm = seg.tile(512)
    return pl.pallas_call(
        _proj_h_ctx_kernel,
        out_shape=[jax.ShapeDtypeStruct((n, w.shape[1]), BF16) for w in ws],
        grid=(n // tm,),
        in_specs=[pl.BlockSpec((tm, d), lambda j: (j, 0)), seg.mod_spec(tm, d, 1), _full(g, 1)]
                 + [_full(w, 1) for w in ws],
        out_specs=[pl.BlockSpec((tm, w.shape[1]), lambda j: (j, 0)) for w in ws],
        compiler_params=_cp("arbitrary"),
        name="hgrn_proj_ctx",
    )(x, mod, g, *ws)


def _hgrn_dir(q_ref, v_ref, f_ref, bias_ref, lb_ref, h_ref, s_scr, reverse, d):
    L = CHUNK
    lb = lb_ref[d:d + 1, :]
    f = lb + (1.0 - lb) * _sigmoid(f_ref[...].astype(F32) + bias_ref[d:d + 1, :])
    a_all = _cumsum_time(jnp.log(f), _cumsum_mat(L, reverse))
    r = lax.broadcasted_iota(jnp.int32, (L, L), 0)
    c = lax.broadcasted_iota(jnp.int32, (L, L), 1)
    causal = (c >= r) if reverse else (c <= r)
    same64 = (r // 64) == (c // 64)
    diag32 = ((r // 32) == (c // 32)) & causal
    t = lax.broadcasted_iota(jnp.int32, (L, 1), 0)
    if reverse:
        late0, late1 = t < 64, (t % 64) < 32
        a0, a1 = (64,), (32, 96)
        a2 = (31, 63, 95, 127)
        last = 0
    else:
        late0, late1 = t >= 64, (t % 64) >= 32
        a0, a1 = (63,), (31, 95)
        a2 = (0, 32, 64, 96)
        last = L - 1
    for h in range(H_HEADS):
        sl = slice(h * H_DK, (h + 1) * H_DK)
        a = a_all[:, sl]
        q = q_ref[:, sl].astype(F32)
        k = 1.0 - f[:, sl]
        v = v_ref[:, h * H_DV:(h + 1) * H_DV]
        anc0 = jnp.broadcast_to(a[a0[0]:a0[0] + 1, :], (L, H_DK))
        anc1 = jnp.where(t < 64, a[a1[0]:a1[0] + 1, :], a[a1[1]:a1[1] + 1, :])
        anc2 = jnp.where(t < 32, a[a2[0]:a2[0] + 1, :],
                         jnp.where(t < 64, a[a2[1]:a2[1] + 1, :],
                                   jnp.where(t < 96, a[a2[2]:a2[2] + 1, :], a[a2[3]:a2[3] + 1, :])))

        def level(anc, late):
            qs = q * jnp.exp(a - anc)
            ks = k * jnp.exp(anc - a)
            if late is not None:
                qs = jnp.where(late, qs, 0.0)
                ks = jnp.where(late, 0.0, ks)
            return lax.dot_general(qs.astype(BF16), ks.astype(BF16), NT_DIMS, preferred_element_type=F32)

        s = jnp.where(diag32, level(anc2, None), jnp.where(same64, level(anc1, late1), level(anc0, late0)))
        st = s_scr[d, h]
        qd = (q * jnp.exp(a)).astype(BF16)
        o = (jnp.dot(s.astype(BF16), v, preferred_element_type=F32)
             + lax.dot_general(qd, st.astype(BF16), NT_DIMS, preferred_element_type=F32))
        h_ref[:, h * H_DV:(h + 1) * H_DV] = o.astype(BF16)
        a_last = a[last:last + 1, :]
        kd = (k * jnp.exp(a_last - a)).astype(BF16)
        s_scr[d, h] = st * jnp.exp(a_last) + lax.dot_general(v, kd, TN_DIMS, preferred_element_type=F32)


def _hgrn_scan_kernel(qf, vf, ff, qb, vb, fb, bias_ref, lb_ref, s0_ref, hf_ref, hb_ref, st_ref, s_scr):
    n = pl.program_id(1)

    @pl.when(n == 0)
    def _():
        s_scr[...] = s0_ref[0]

    _hgrn_dir(qf, vf, ff, bias_ref, lb_ref, hf_ref, s_scr, False, 0)
    _hgrn_dir(qb, vb, fb, bias_ref, lb_ref, hb_ref, s_scr, True, 1)

    @pl.when(n == pl.num_programs(1) - 1)
    def _():
        st_ref[0] = s_scr[...]


def _hgrn_scan(q, v, f_f, f_b, bias, lb, s0, seg):
    L = CHUNK
    assert seg.seqlen % L == 0
    n = seg.seqlen // L
    fwd = lambda b, i: (b * n + i, 0)
    bwd = lambda b, i: (b * n + (n - 1 - i), 0)
    rows_, w = q.shape
    st = lambda b, i: (b, 0, 0, 0, 0)
    return pl.pallas_call(
        _hgrn_scan_kernel,
        out_shape=[jax.ShapeDtypeStruct((rows_, w), BF16), jax.ShapeDtypeStruct((rows_, w), BF16),
                   jax.ShapeDtypeStruct(s0.shape, F32)],
        grid=(seg.batch, n),
        in_specs=[pl.BlockSpec((L, w), fwd), pl.BlockSpec((L, w), fwd), pl.BlockSpec((L, w), fwd),
                  pl.BlockSpec((L, w), bwd), pl.BlockSpec((L, w), bwd), pl.BlockSpec((L, w), bwd),
                  _full(bias, 2), _full(lb, 2), pl.BlockSpec((1,) + s0.shape[1:], st)],
        out_specs=[pl.BlockSpec((L, w), fwd), pl.BlockSpec((L, w), bwd),
                   pl.BlockSpec((1,) + s0.shape[1:], st)],
        scratch_shapes=[pltpu.VMEM(s0.shape[1:], F32)],
        compiler_params=_cp("arbitrary", "arbitrary"),
        name="hgrn_scan",
    )(q, v, f_f, q, v, f_b, bias, lb, s0)


def _mix_out_col_kernel(hf_ref, hb_ref, o_ref, x_ref, mod_ref, gh_ref, w_ref, g1_ref, out_ref,
                        *, n_heads, rows_per_col):
    mod = mod_ref[0]
    y = _head_out(hf_ref[...], hb_ref[...], o_ref[...], gh_ref[...], w_ref[...], n_heads)
    upd = mod[2:3] * (_rms(y) * g1_ref[...])
    upd = jnp.swapaxes(upd.reshape(SUBLANES, rows_per_col, upd.shape[1]), 0, 1)
    out_ref[...] = x_ref[...] + upd


def _mix_out_col(hf, hb, o, x, seg, mod, gh, w, g1, n_heads):
    n, d = x.shape
    grid_rows = seg.seqlen // GRID_W
    tcol = SUBLANES * grid_rows
    ncb = GRID_W // SUBLANES
    x3 = x.reshape(n // GRID_W, GRID_W, d)
    hmap = lambda b, c: (b * ncb + c, 0)
    xmap = lambda b, c: (b, c, 0)
    consts = (gh, w, g1)
    out = pl.pallas_call(
        functools.partial(_mix_out_col_kernel, n_heads=n_heads, rows_per_col=grid_rows),
        out_shape=jax.ShapeDtypeStruct(x3.shape, F32),
        grid=(seg.batch, ncb),
        in_specs=[pl.BlockSpec((tcol, hf.shape[1]), hmap), pl.BlockSpec((tcol, hf.shape[1]), hmap),
                  pl.BlockSpec((tcol, o.shape[1]), hmap),
                  pl.BlockSpec((grid_rows, SUBLANES, d), xmap),
                  pl.BlockSpec((1, N_MOD, d), lambda b, c: (b, 0, 0))] + [_full(a, 2) for a in consts],
        out_specs=pl.BlockSpec((grid_rows, SUBLANES, d), xmap),
        input_output_aliases={3: 0},
        compiler_params=_cp("arbitrary", "arbitrary"),
        name="mix_out_col",
    )(hf, hb, o, x3, mod, *consts)
    return out.reshape(n, d)


def kernel(x, c, ctx, c_ctx, mod_w, mod_b, norm_g, m_w_in, m_b_gate, m_w_conv, m_g_head, m_w_out,
           h_w_in, h_b_f, h_lb_raw, h_g_head, h_w_out, f_w_gu, f_w_down,
           e_w_router, e_b_router, e_w_gu, e_w_down):
    batch, seq, d = x.shape
    ctx_len = ctx.shape[1]
    depth = mod_w.shape[0]
    assert seq % GRID_W == 0 and seq // GRID_W == CHUNK and ctx_len % CHUNK == 0
    assert batch < MOD_ROWS and d % LANES == 0
    lat_seg = _Seg(batch, seq, False)
    ctx_seg = _Seg(batch, ctx_len, True)

    c_all = jnp.zeros((MOD_ROWS, d), F32).at[:batch].set(c).at[batch].set(c_ctx)
    mod_all = _modulation(c_all, mod_w, mod_b).reshape(depth, MOD_ROWS, N_MOD, d)

    lb_all = jax.nn.softmax(h_lb_raw.astype(F32), axis=0)
    lb_all = jnp.cumsum(lb_all, axis=0) - lb_all[0]

    xl = x.reshape(batch * seq, d)
    xc = ctx.reshape(batch * ctx_len, d)
    row = lambda v: v.reshape(1, -1).astype(F32)

    for i in range(depth):
        j = i // 2
        with_ctx = i < depth - 1
        mod = mod_all[i]
        g0, g1, g2, g3 = (row(norm_g[i, s]) for s in range(4))
        if i % 2 == 0:
            qk_w = 2 * M_HEADS * M_DK
            v_w = M_HEADS * M_DV
            w_in = m_w_in[j]
            wqk = w_in[:, :qk_w].astype(BF16)
            wv = w_in[:, qk_w:qk_w + v_w].astype(BF16)
            wo = w_in[:, qk_w + v_w:qk_w + 2 * v_w].astype(BF16)
            ng = 4 * M_HEADS
            wg = jnp.zeros((d, LANES), F32).at[:, :ng].set(w_in[:, qk_w + 2 * v_w:]).astype(BF16)
            gb = jnp.zeros((1, LANES), F32).at[0, :ng].set(m_b_gate[j].reshape(-1))
            col = jnp.arange(LANES)
            gm = (((col // M_HEADS) % 2 == 1) & (col < ng)).astype(F32).reshape(1, LANES)
            pw = (mod, g0, wqk, wv, wo, wg, m_w_conv[j].astype(F32), gb, gm)
            qc, kc, vc, oc, gc = _proj_m(xc, ctx_seg, *pw)
            ql, kl, vl, ol, gl = _proj_m(xl, lat_seg, *pw)
            s0 = jnp.zeros((batch, 2, M_HEADS // 2, LANES, 2 * M_DV), F32)
            m0 = jnp.zeros((batch, 2 * M_HEADS, LANES), F32)
            hcf, hcb, s1, m1 = _mlstm_scan(qc, kc, vc, gc, s0, m0, ctx_seg)
            hlf, hlb, _, _ = _mlstm_scan(ql, kl, vl, gl, s1, m1, lat_seg)
            gh = row(m_g_head[j])
            wout = m_w_out[j].astype(BF16)
            xl = _mix_out(hlf, hlb, ol, xl, lat_seg, mod, gh, wout, g1, M_HEADS, in_place=i > 0)
            if with_ctx:
                xc = _mix_out(hcf, hcb, oc, xc, ctx_seg, mod, gh, wout, g1, M_HEADS, in_place=i > 0)
        else:
            kw = H_HEADS * H_DK
            w_in = h_w_in[j].astype(BF16)
            ws = [w_in[:, s * kw:(s + 1) * kw] for s in range(5)]
            qc, vc, ffc, fbc, gc = _proj_h_ctx(xc, ctx_seg, mod, g0, ws)
            ql, vl, ffl, fbl, gl = _proj_h_lat(xl, lat_seg, mod, g0, ws)
            bias = h_b_f[j].astype(F32)
            lb = lb_all[j]
            s0 = jnp.zeros((batch, 2, H_HEADS, H_DV, H_DK), F32)
            hcf, hcb, s1 = _hgrn_scan(qc, vc, ffc, fbc, bias, lb, s0, ctx_seg)
            hlf, hlb, _ = _hgrn_scan(ql, vl, ffl, fbl, bias, lb, s1, lat_seg)
            gh = row(h_g_head[j])
            wout = h_w_out[j].astype(BF16)
            xl = _mix_out_col(hlf, hlb, gl, xl, lat_seg, mod, gh, wout, g1, H_HEADS)
            if with_ctx:
                xc = _mix_out(hcf, hcb, gc, xc, ctx_seg, mod, gh, wout, g1, H_HEADS)
        streams = [(xl, lat_seg)] + ([(xc, ctx_seg)] if with_ctx else [])
        outs = []
        if i % 2 == 0:
            wgu, wd = f_w_gu[j].astype(BF16), f_w_down[j].astype(BF16)
            for xs, seg in streams:
                outs.append(_ffn(xs, seg, mod, g2, wgu, wd, g3, th=256))
        else:
            wr = jnp.zeros((d, LANES), F32).at[:, :N_EXPERTS].set(e_w_router[j])
            br = jnp.zeros((1, LANES), F32).at[0, :N_EXPERTS].set(e_b_router[j])
            wgu, wd = e_w_gu[j].astype(BF16), e_w_down[j].astype(BF16)
            outs.append(_moe_routed(xl, lat_seg, mod, g2, wr, br, wgu, wd, g3, th=256))
            if with_ctx:
                u, comb = _router(xc, ctx_seg, mod, g2, wr, br)
                outs.append(_moe(u, comb, xc, ctx_seg, mod, wgu, wd, g3, th=256))
        xl = outs[0]
        if with_ctx:
            xc = outs[1]
    return xl.reshape(batch, seq, d)
```

```python
import functools

import jax
import jax.numpy as jnp
from jax import lax
from jax.experimental import pallas as pl
from jax.experimental.pallas import tpu as pltpu

F32 = jnp.float32
BF16 = jnp.bfloat16
EPS = 1e-6
NEG = -1e30

GRID_W = 64
N_MOD = 6
M_HEADS, M_DK, M_DV = 8, 64, 128
H_HEADS, H_DK, H_DV = 8, 128, 128
N_EXPERTS = 8
LANES = 128
SUBLANES = 8
CHUNK = 128
MOD_ROWS = 16
VMEM_LIMIT = 56 * 1024 * 1024

HI = lax.Precision.HIGHEST
NT_DIMS = (((1,), (1,)), ((), ()))
TN_DIMS = (((0,), (0,)), ((), ()))


def _cp(*sem):
    return pltpu.CompilerParams(dimension_semantics=sem, vmem_limit_bytes=VMEM_LIMIT)


def _full(a, nargs):
    zeros = (0,) * a.ndim
    return pl.BlockSpec(a.shape, lambda *_: zeros)


def _sigmoid(x):
    return 1.0 / (1.0 + jnp.exp(-x))


def _silu(x):
    return x * _sigmoid(x)


def _log_sigmoid(x):
    return jnp.minimum(x, 0.0) - jnp.log(1.0 + jnp.exp(-jnp.abs(x)))


def _rms(x):
    return x * lax.rsqrt(jnp.mean(x * x, axis=-1, keepdims=True) + EPS)


def _normmod(x, g, mod, s):
    return _rms(x) * (g * (1.0 + mod[s + 1:s + 2])) + mod[s:s + 1]


def _split2(x):
    hi = x.astype(BF16)
    lo = (x - hi.astype(F32)).astype(BF16)
    return hi, lo


def _cumsum_mat(n, reverse):
    r = lax.broadcasted_iota(jnp.int32, (n, n), 0)
    c = lax.broadcasted_iota(jnp.int32, (n, n), 1)
    return jnp.where((c >= r) if reverse else (c <= r), 1.0, 0.0).astype(BF16)


def _cumsum_time(x, tri):
    hi, lo = _split2(x)
    return (jnp.dot(tri, hi, preferred_element_type=F32)
            + jnp.dot(tri, lo, preferred_element_type=F32))


class _Seg:
    def __init__(self, batch, seqlen, is_ctx):
        self.batch, self.seqlen, self.is_ctx = batch, seqlen, is_ctx
        self.n = batch * seqlen

    def tile(self, pref):
        tm = min(pref, self.n if self.is_ctx else self.seqlen)
        assert self.n % tm == 0 and (self.seqlen % tm == 0 or tm % self.seqlen == 0)
        return tm

    def mod_spec(self, tm, d, nargs):
        if self.is_ctx:
            row = lambda j: self.batch
        else:
            row = lambda j: (j * tm) // self.seqlen
        if nargs == 1:
            return pl.BlockSpec((1, N_MOD, d), lambda j: (row(j), 0, 0))
        if nargs == 2:
            return pl.BlockSpec((1, N_MOD, d), lambda j, k: (row(j), 0, 0))
        return pl.BlockSpec((1, N_MOD, d), lambda j, e, k: (row(j), 0, 0))


def _mod_kernel(c_ref, w_ref, b_ref, o_ref):
    c = c_ref[...]
    o_ref[0] = jnp.dot(_silu(c), w_ref[0], preferred_element_type=F32, precision=HI) + b_ref[0]


def _modulation(c_all, mod_w, mod_b):
    depth, d, n = mod_w.shape
    tn = n // 4
    return pl.pallas_call(
        _mod_kernel,
        out_shape=jax.ShapeDtypeStruct((depth, MOD_ROWS, n), F32),
        grid=(depth, n // tn),
        in_specs=[pl.BlockSpec((MOD_ROWS, d), lambda i, j: (0, 0)),
                  pl.BlockSpec((1, d, tn), lambda i, j: (i, 0, j)),
                  pl.BlockSpec((1, 1, tn), lambda i, j: (i, 0, j))],
        out_specs=pl.BlockSpec((1, MOD_ROWS, tn), lambda i, j: (i, 0, j)),
        compiler_params=_cp("arbitrary", "arbitrary"),
        name="modulation",
    )(c_all, mod_w, mod_b.reshape(depth, 1, n))


def _proj_m_kernel(xp_ref, x_ref, xn_ref, mod_ref, g_ref, wqk_ref, wv_ref, wo_ref, wg_ref,
                   cw_ref, gb_ref, gm_ref, q_ref, kt_ref, v_ref, o_ref, gt_ref, *, tm, seqlen):
    j = pl.program_id(0)
    mod = mod_ref[0]
    g = g_ref[...]
    u = _normmod(x_ref[...], g, mod, 0).astype(BF16)
    uh = _normmod(jnp.concatenate([xp_ref[...], xn_ref[...]], axis=0), g, mod, 0).astype(BF16)
    wqk = wqk_ref[...]
    z = jnp.dot(u, wqk, preferred_element_type=F32)
    zh = jnp.dot(uh, wqk, preferred_element_type=F32)
    local = lax.broadcasted_iota(jnp.int32, (tm, 1), 0)
    pos = lax.rem(j * tm + local, seqlen)
    zp = jnp.where(local == 0, zh[SUBLANES - 1:SUBLANES], pltpu.roll(z, 1, 0))
    zn = jnp.where(local == tm - 1, zh[SUBLANES:SUBLANES + 1], pltpu.roll(z, tm - 1, 0))
    zp = jnp.where(pos == 0, 0.0, zp)
    zn = jnp.where(pos == seqlen - 1, 0.0, zn)
    cw = cw_ref[...]
    a = _silu(cw[0:1] * zp + cw[1:2] * z + cw[2:3] * zn)
    half = a.shape[1] // 2
    q_ref[...] = a[:, :half].astype(BF16)
    kt_ref[...] = (a[:, half:] * (M_DK ** -0.5)).T.astype(BF16)
    v_ref[...] = jnp.dot(u, wv_ref[...], preferred_element_type=F32).astype(BF16)
    o_ref[...] = jnp.dot(u, wo_ref[...], preferred_element_type=F32).astype(BF16)
    zg = jnp.dot(u, wg_ref[...], preferred_element_type=F32) + gb_ref[...]
    gt_ref[...] = jnp.where(gm_ref[...] > 0.5, _log_sigmoid(zg), zg)


def _proj_m(x, seg, mod, g, wqk, wv, wo, wg, cw, gb, gm):
    n, d = x.shape
    tm = seg.tile(512)
    nb8 = n // SUBLANES
    r8 = tm // SUBLANES
    kern = functools.partial(_proj_m_kernel, tm=tm, seqlen=seg.seqlen)
    nqk = wqk.shape[1]
    consts = (g, wqk, wv, wo, wg, cw, gb, gm)
    return pl.pallas_call(
        kern,
        out_shape=[jax.ShapeDtypeStruct((n, nqk // 2), BF16),
                   jax.ShapeDtypeStruct((nqk // 2, n), BF16),
                   jax.ShapeDtypeStruct((n, wv.shape[1]), BF16),
                   jax.ShapeDtypeStruct((n, wo.shape[1]), BF16),
                   jax.ShapeDtypeStruct((n, LANES), F32)],
        grid=(n // tm,),
        in_specs=[pl.BlockSpec((SUBLANES, d), lambda j: (jnp.maximum(j * r8 - 1, 0), 0)),
                  pl.BlockSpec((tm, d), lambda j: (j, 0)),
                  pl.BlockSpec((SUBLANES, d), lambda j: (jnp.minimum((j + 1) * r8, nb8 - 1), 0)),
                  seg.mod_spec(tm, d, 1)] + [_full(a, 1) for a in consts],
        out_specs=[pl.BlockSpec((tm, nqk // 2), lambda j: (j, 0)),
                   pl.BlockSpec((nqk // 2, tm), lambda j: (0, j)),
                   pl.BlockSpec((tm, wv.shape[1]), lambda j: (j, 0)),
                   pl.BlockSpec((tm, wo.shape[1]), lambda j: (j, 0)),
                   pl.BlockSpec((tm, LANES), lambda j: (j, 0))],
        compiler_params=_cp("arbitrary"),
        name="mlstm_proj",
    )(x, x, x, mod, *consts)


MLSTM_PAIR_GROUP = 2


def _mlstm_prologue(g_ref, reverse):
    g = g_ref[...]
    b = _cumsum_time(g, _cumsum_mat(CHUNK, reverse))
    return g.T, b, b.T


def _mlstm_heads(q_ref, kt_ref, v_ref, pro, h_ref, s_scr, m_scr, reverse, d, pairs):
    L = CHUNK
    gT, b, bT = pro
    r = lax.broadcasted_iota(jnp.int32, (L, L), 0)
    c = lax.broadcasted_iota(jnp.int32, (L, L), 1)
    causal = (c >= r) if reverse else (c <= r)
    last = 0 if reverse else L - 1
    c_ig = 2 * M_HEADS * d
    c_lf = c_ig + M_HEADS
    lane = lax.broadcasted_iota(jnp.int32, (1, LANES), 1)
    ones = jnp.ones((L, M_DV), BF16)
    staged = []
    for p in pairs:
        q_p = q_ref[:, p * LANES:(p + 1) * LANES]
        kt_p = kt_ref[p * LANES:(p + 1) * LANES, :]
        for e in range(2):
            h = 2 * p + e
            q_e = jnp.where((lane // M_DK) == e, q_p, jnp.zeros_like(q_p))
            b_col = b[:, c_lf + h:c_lf + h + 1]
            beta = gT[c_ig + h:c_ig + h + 1, :] - bT[c_lf + h:c_lf + h + 1, :]
            m_prev = m_scr[d * M_HEADS + h:d * M_HEADS + h + 1, 0:1]
            bm = jnp.where(causal, beta, NEG)
            mu = jnp.maximum(m_prev, jnp.max(bm, axis=1, keepdims=True))
            qk = jnp.dot(q_e, kt_p, preferred_element_type=F32)
            s = (qk * jnp.exp(bm - mu)).astype(BF16)
            mu_last = mu[last:last + 1, :]
            ktw = (kt_p[e * M_DK:(e + 1) * M_DK, :].astype(F32) * jnp.exp(beta - mu_last)).astype(BF16)
            m_new = b_col[last:last + 1, :] + mu_last
            m_scr[d * M_HEADS + h:d * M_HEADS + h + 1, :] = jnp.broadcast_to(m_new, (1, LANES))
            staged.append((p, e, q_e, s, jnp.exp(m_prev - mu), jnp.exp(-(b_col + mu)), ktw,
                           jnp.exp(m_prev - mu_last)))
    for p, e, q_e, s, w_inter, floor, ktw, dec in staged:
        h = 2 * p + e
        rows = slice(e * M_DK, (e + 1) * M_DK)
        vp = jnp.concatenate([v_ref[:, h * M_DV:(h + 1) * M_DV], ones], axis=1)
        tot = (jnp.dot(s, vp, preferred_element_type=F32)
               + w_inter * jnp.dot(q_e, s_scr[d, p].astype(BF16), preferred_element_type=F32))
        den = jnp.maximum(jnp.abs(tot[:, M_DV:]), floor)
        h_ref[:, h * M_DV:(h + 1) * M_DV] = (tot[:, :M_DV] / den).astype(BF16)
        s_scr[d, p, rows, :] = dec * s_scr[d, p, rows, :] + jnp.dot(ktw, vp, preferred_element_type=F32)


def _mlstm_scan_kernel(qf, kf, vf, gf, qb, kb, vb, gb, s0_ref, m0_ref,
                       hf_ref, hb_ref, st_ref, mt_ref, s_scr, m_scr):
    n = pl.program_id(1)

    @pl.when(n == 0)
    def _():
        s_scr[...] = s0_ref[0]
        m_scr[...] = m0_ref[0]

    pro_f = _mlstm_prologue(gf, False)
    pro_b = _mlstm_prologue(gb, True)
    for p0 in range(0, M_HEADS // 2, MLSTM_PAIR_GROUP):
        pairs = range(p0, p0 + MLSTM_PAIR_GROUP)
        _mlstm_heads(qf, kf, vf, pro_f, hf_ref, s_scr, m_scr, False, 0, pairs)
        _mlstm_heads(qb, kb, vb, pro_b, hb_ref, s_scr, m_scr, True, 1, pairs)

    @pl.when(n == pl.num_programs(1) - 1)
    def _():
        st_ref[0] = s_scr[...]
        mt_ref[0] = m_scr[...]


def _mlstm_scan(q, kt, v, gates, s0, m0, seg):
    L = CHUNK
    assert seg.seqlen % L == 0
    n = seg.seqlen // L
    fwd = lambda b, i: (b * n + i, 0)
    bwd = lambda b, i: (b * n + (n - 1 - i), 0)
    fwd_t = lambda b, i: (0, b * n + i)
    bwd_t = lambda b, i: (0, b * n + (n - 1 - i))
    dq, dv = q.shape[1], v.shape[1]
    st = lambda b, i: (b, 0, 0, 0, 0)
    mt = lambda b, i: (b, 0, 0)
    return pl.pallas_call(
        _mlstm_scan_kernel,
        out_shape=[jax.ShapeDtypeStruct((seg.n, dv), BF16), jax.ShapeDtypeStruct((seg.n, dv), BF16),
                   jax.ShapeDtypeStruct(s0.shape, F32), jax.ShapeDtypeStruct(m0.shape, F32)],
        grid=(seg.batch, n),
        in_specs=[pl.BlockSpec((L, dq), fwd), pl.BlockSpec((dq, L), fwd_t), pl.BlockSpec((L, dv), fwd),
                  pl.BlockSpec((L, LANES), fwd),
                  pl.BlockSpec((L, dq), bwd), pl.BlockSpec((dq, L), bwd_t), pl.BlockSpec((L, dv), bwd),
                  pl.BlockSpec((L, LANES), bwd),
                  pl.BlockSpec((1,) + s0.shape[1:], st), pl.BlockSpec((1,) + m0.shape[1:], mt)],
        out_specs=[pl.BlockSpec((L, dv), fwd), pl.BlockSpec((L, dv), bwd),
                   pl.BlockSpec((1,) + s0.shape[1:], st), pl.BlockSpec((1,) + m0.shape[1:], mt)],
        scratch_shapes=[pltpu.VMEM(s0.shape[1:], F32), pltpu.VMEM(m0.shape[1:], F32)],
        compiler_params=_cp("arbitrary", "arbitrary"),
        name="mlstm_scan",
    )(q, kt, v, gates, q, kt, v, gates, s0, m0)


def _head_out(hf, hb, gate_pre, gh, w, n_heads):
    h = hf.astype(F32) + hb.astype(F32)
    dv = h.shape[1] // n_heads
    parts = [_rms(h[:, i * dv:(i + 1) * dv]) for i in range(n_heads)]
    hn = jnp.concatenate(parts, axis=1) * gh * _sigmoid(gate_pre.astype(F32))
    return jnp.dot(hn.astype(BF16), w, preferred_element_type=F32)


def _mix_out_kernel(hf_ref, hb_ref, o_ref, x_ref, mod_ref, gh_ref, w_ref, g1_ref, out_ref, *, n_heads):
    mod = mod_ref[0]
    y = _head_out(hf_ref[...], hb_ref[...], o_ref[...], gh_ref[...], w_ref[...], n_heads)
    out_ref[...] = x_ref[...] + mod[2:3] * (_rms(y) * g1_ref[...])


def _mix_out(hf, hb, o, x, seg, mod, gh, w, g1, n_heads, in_place=True):
    n, d = x.shape
    tm = seg.tile(512)
    rmap = lambda j: (j, 0)
    consts = (gh, w, g1)
    return pl.pallas_call(
        functools.partial(_mix_out_kernel, n_heads=n_heads),
        out_shape=jax.ShapeDtypeStruct(x.shape, F32),
        grid=(n // tm,),
        in_specs=[pl.BlockSpec((tm, hf.shape[1]), rmap), pl.BlockSpec((tm, hf.shape[1]), rmap),
                  pl.BlockSpec((tm, o.shape[1]), rmap), pl.BlockSpec((tm, d), rmap),
                  seg.mod_spec(tm, d, 1)] + [_full(a, 1) for a in consts],
        out_specs=pl.BlockSpec((tm, d), rmap),
        input_output_aliases={3: 0} if in_place else {},
        compiler_params=_cp("arbitrary"),
        name="mix_out",
    )(hf, hb, o, x, mod, *consts)


def _swiglu_acc(u_scr, wgu_ref, wd_ref, acc_scr, th):
    hd = wd_ref.shape[0]
    assert hd % th == 0 and hd % LANES == 0

    def chunk(k, carry):
        off = pl.multiple_of(k * th, th)
        u = u_scr[...]
        gg = jnp.dot(u, wgu_ref[:, pl.ds(off, th)], preferred_element_type=F32)
        up = jnp.dot(u, wgu_ref[:, pl.ds(pl.multiple_of(hd + off, LANES), th)], preferred_element_type=F32)
        hh = (_silu(gg) * up).astype(BF16)
        acc_scr[...] += jnp.dot(hh, wd_ref[pl.ds(off, th), :], preferred_element_type=F32)
        return carry

    acc_scr[...] = jnp.zeros_like(acc_scr)
    lax.fori_loop(0, hd // th, chunk, 0)


def _ffn_kernel(x_ref, mod_ref, g2_ref, wgu_ref, wd_ref, g3_ref, out_ref, u_scr, acc_scr, *, th):
    u_scr[...] = _normmod(x_ref[...], g2_ref[...], mod_ref[0], 3).astype(BF16)
    _swiglu_acc(u_scr, wgu_ref, wd_ref, acc_scr, th)
    mod = mod_ref[0]
    out_ref[...] = x_ref[...] + mod[5:6] * (_rms(acc_scr[...]) * g3_ref[...])


def _resident(a, index_map):
    return pl.BlockSpec(a, index_map, pipeline_mode=pl.Buffered(1))


def _ffn(x, seg, mod, g2, wgu, wd, g3, th):
    n, d = x.shape
    tm = seg.tile(1024)
    return pl.pallas_call(
        functools.partial(_ffn_kernel, th=th),
        out_shape=jax.ShapeDtypeStruct((n, d), F32),
        grid=(n // tm,),
        in_specs=[pl.BlockSpec((tm, d), lambda i: (i, 0)),
                  seg.mod_spec(tm, d, 1),
                  _full(g2, 1),
                  _resident(wgu.shape, lambda i: (0, 0)),
                  _resident(wd.shape, lambda i: (0, 0)),
                  _full(g3, 1)],
        out_specs=pl.BlockSpec((tm, d), lambda i: (i, 0)),
        scratch_shapes=[pltpu.VMEM((tm, d), BF16), pltpu.VMEM((tm, d), F32)],
        input_output_aliases={0: 0},
        compiler_params=_cp("arbitrary"),
        name="ffn",
    )(x, mod, g2, wgu, wd, g3)


def _router_kernel(x_ref, mod_ref, g2_ref, wr_ref, br_ref, u_ref, comb_ref):
    u = _normmod(x_ref[...], g2_ref[...], mod_ref[0], 3)
    u_ref[...] = u.astype(BF16)
    lane = lax.broadcasted_iota(jnp.int32, (1, LANES), 1)
    logits = jnp.dot(u, wr_ref[...], preferred_element_type=F32, precision=HI) + br_ref[...]
    logits = jnp.where(lane < N_EXPERTS, logits, NEG)
    m1 = jnp.max(logits, axis=1, keepdims=True)
    i1 = jnp.min(jnp.where(logits == m1, lane, LANES), axis=1, keepdims=True)
    rest = jnp.where(lane == i1, NEG, logits)
    m2 = jnp.max(rest, axis=1, keepdims=True)
    i2 = jnp.min(jnp.where(rest == m2, lane, LANES), axis=1, keepdims=True)
    e2 = jnp.exp(m2 - m1)
    w1 = 1.0 / (1.0 + e2)
    comb_ref[...] = jnp.where(lane == i1, w1, 0.0) + jnp.where(lane == i2, e2 * w1, 0.0)


def _router(x, seg, mod, g2, wr, br):
    n, d = x.shape
    tm = seg.tile(1024)
    consts = (g2, wr, br)
    return pl.pallas_call(
        _router_kernel,
        out_shape=[jax.ShapeDtypeStruct((n, d), BF16), jax.ShapeDtypeStruct((n, LANES), F32)],
        grid=(n // tm,),
        in_specs=[pl.BlockSpec((tm, d), lambda i: (i, 0)), seg.mod_spec(tm, d, 1)]
                 + [_full(a, 1) for a in consts],
        out_specs=[pl.BlockSpec((tm, d), lambda i: (i, 0)), pl.BlockSpec((tm, LANES), lambda i: (i, 0))],
        compiler_params=_cp("arbitrary"),
        name="router",
    )(x, mod, *consts)


def _moe_kernel(u_ref, comb_ref, x_ref, mod_ref, wg_ref, wu_ref, wd_ref, g3_ref, out_ref, acc_scr):
    e = pl.program_id(1)
    kk = pl.program_id(2)

    @pl.when((e == 0) & (kk == 0))
    def _():
        acc_scr[...] = jnp.zeros_like(acc_scr)

    u = u_ref[...]
    gg = jnp.dot(u, wg_ref[0], preferred_element_type=F32)
    up = jnp.dot(u, wu_ref[0], preferred_element_type=F32)
    hh = (_silu(gg) * up).astype(BF16)
    lane = lax.broadcasted_iota(jnp.int32, (1, LANES), 1)
    cw = jnp.sum(jnp.where(lane == e, comb_ref[...], 0.0), axis=1, keepdims=True)
    acc_scr[...] += cw * jnp.dot(hh, wd_ref[0], preferred_element_type=F32)

    @pl.when((e == pl.num_programs(1) - 1) & (kk == pl.num_programs(2) - 1))
    def _():
        mod = mod_ref[0]
        out_ref[...] = x_ref[...] + mod[5:6] * (_rms(acc_scr[...]) * g3_ref[...])


def _moe(u, comb, x, seg, mod, wgu, wd, g3, th):
    n, d = x.shape
    tm = seg.tile(1024)
    ne, hd, _ = wd.shape
    nk = hd // th
    return pl.pallas_call(
        _moe_kernel,
        out_shape=jax.ShapeDtypeStruct((n, d), F32),
        grid=(n // tm, ne, nk),
        in_specs=[pl.BlockSpec((tm, d), lambda i, e, k: (i, 0)),
                  pl.BlockSpec((tm, LANES), lambda i, e, k: (i, 0)),
                  pl.BlockSpec((tm, d), lambda i, e, k: (i, 0)),
                  seg.mod_spec(tm, d, 3),
                  pl.BlockSpec((1, d, th), lambda i, e, k: (e, 0, k)),
                  pl.BlockSpec((1, d, th), lambda i, e, k: (e, 0, nk + k)),
                  pl.BlockSpec((1, th, d), lambda i, e, k: (e, k, 0)),
                  _full(g3, 3)],
        out_specs=pl.BlockSpec((tm, d), lambda i, e, k: (i, 0)),
        scratch_shapes=[pltpu.VMEM((tm, d), F32)],
        input_output_aliases={2: 0},
        compiler_params=_cp("arbitrary", "arbitrary", "arbitrary"),
        name="moe",
    )(u, comb, x, mod, wgu, wgu, wd, g3)


ROW_TILE = 1024
META_E, META_RANK, META_W = 0, 2, 4


def _route_kernel(x_ref, mod_ref, g2_ref, wr_ref, br_ref, meta_ref, cnt_ref, tri_scr, run_scr):
    i = pl.program_id(0)
    tm = x_ref.shape[0]

    @pl.when(i == 0)
    def _():
        r = lax.broadcasted_iota(jnp.int32, (tm, tm), 0)
        c = lax.broadcasted_iota(jnp.int32, (tm, tm), 1)
        tri_scr[...] = jnp.where(c < r, 1.0, 0.0).astype(BF16)
        run_scr[...] = jnp.zeros_like(run_scr)

    u = _normmod(x_ref[...], g2_ref[...], mod_ref[0], 3)
    lane = lax.broadcasted_iota(jnp.int32, (1, LANES), 1)
    logits = jnp.dot(u, wr_ref[...], preferred_element_type=F32, precision=HI) + br_ref[...]
    logits = jnp.where(lane < N_EXPERTS, logits, NEG)
    m1 = jnp.max(logits, axis=1, keepdims=True)
    i1 = jnp.min(jnp.where(logits == m1, lane, LANES), axis=1, keepdims=True)
    rest = jnp.where(lane == i1, NEG, logits)
    m2 = jnp.max(rest, axis=1, keepdims=True)
    i2 = jnp.min(jnp.where(rest == m2, lane, LANES), axis=1, keepdims=True)
    e2 = jnp.exp(m2 - m1)
    w1 = 1.0 / (1.0 + e2)
    sel = ((lane == i1) | (lane == i2))
    onehot = jnp.where(sel, 1.0, 0.0)
    before = jnp.dot(tri_scr[...], onehot.astype(BF16), preferred_element_type=F32) + run_scr[...]
    rank1 = jnp.sum(jnp.where(lane == i1, before, 0.0), axis=1, keepdims=True)
    rank2 = jnp.sum(jnp.where(lane == i2, before, 0.0), axis=1, keepdims=True)
    run_scr[...] = run_scr[...] + jnp.sum(onehot, axis=0, keepdims=True)
    vals = (i1.astype(F32), i2.astype(F32), rank1, rank2, w1, e2 * w1)
    meta = jnp.zeros((tm, LANES), F32)
    for col, val in enumerate(vals):
        meta = jnp.where(lane == col, val, meta)
    meta_ref[...] = meta
    cnt_ref[...] = jnp.broadcast_to(run_scr[...], cnt_ref.shape)


def _route(x, seg, mod, g2, wr, br):
    n, d = x.shape
    tm = seg.tile(1024)
    consts = (g2, wr, br)
    return pl.pallas_call(
        _route_kernel,
        out_shape=[jax.ShapeDtypeStruct((n, LANES), F32),
                   jax.ShapeDtypeStruct((SUBLANES, LANES), F32)],
        grid=(n // tm,),
        in_specs=[pl.BlockSpec((tm, d), lambda i: (i, 0)), seg.mod_spec(tm, d, 1)]
                 + [_full(a, 1) for a in consts],
        out_specs=[pl.BlockSpec((tm, LANES), lambda i: (i, 0)),
                   pl.BlockSpec((SUBLANES, LANES), lambda i: (0, 0))],
        scratch_shapes=[pltpu.VMEM((tm, tm), BF16), pltpu.VMEM((1, LANES), F32)],
        compiler_params=_cp("arbitrary"),
        name="route",
    )(x, mod, *consts)


def _drain(src_ref, dst_ref, sem, count):
    def body(i, carry):
        pltpu.make_async_copy(src_ref, dst_ref, sem).wait()
        return carry

    lax.fori_loop(0, count, body, 0, unroll=8)


def _row_tile(buf, t):
    return buf.at[pl.ds(pl.multiple_of(t * SUBLANES, SUBLANES), SUBLANES), :]


def _dispatch_kernel(dst_ref, pad_ref, x_ref, mod_ref, g2_ref, xs_ref, ubuf, sem):
    j = pl.program_id(0)
    tm = x_ref.shape[0]
    u = _normmod(x_ref[...], g2_ref[...], mod_ref[0], 3)
    for s in range(SUBLANES):
        ubuf[pl.ds(s, tm, stride=SUBLANES), :] = u[:, s * LANES:(s + 1) * LANES]

    def issue(t, carry):
        pltpu.make_async_copy(_row_tile(ubuf, t), xs_ref.at[dst_ref[0, 0, t]], sem).start(priority=0)
        pltpu.make_async_copy(_row_tile(ubuf, t), xs_ref.at[dst_ref[0, 0, tm + t]], sem).start(priority=1)
        return carry

    lax.fori_loop(0, tm, issue, 0, unroll=8)
    npad = pad_ref.shape[2]
    first = _row_tile(ubuf, 0)

    @pl.when(j == 0)
    def _():
        def issue_pad(p, carry):
            pltpu.make_async_copy(first, xs_ref.at[pad_ref[0, 0, p]], sem).start()
            return carry

        lax.fori_loop(0, npad, issue_pad, 0, unroll=8)
        _drain(first, xs_ref.at[0], sem, npad)

    _drain(first, xs_ref.at[0], sem, 2 * tm)


def _dispatch(x, seg, mod, g2, dest, pad_dst, n_rows):
    n, d = x.shape
    tm = seg.tile(512)
    steps = n // tm
    dst = dest.reshape(steps, tm, 2).transpose(0, 2, 1).reshape(steps, 1, 2 * tm)
    npad = pad_dst.shape[0]
    return pl.pallas_call(
        _dispatch_kernel,
        out_shape=jax.ShapeDtypeStruct((n_rows, SUBLANES, d // SUBLANES), F32),
        grid=(steps,),
        in_specs=[pl.BlockSpec((1, 1, 2 * tm), lambda j: (j, 0, 0), memory_space=pltpu.SMEM),
                  pl.BlockSpec((1, 1, npad), lambda j: (0, 0, 0), memory_space=pltpu.SMEM),
                  pl.BlockSpec((tm, d), lambda j: (j, 0)), seg.mod_spec(tm, d, 1), _full(g2, 1)],
        out_specs=pl.BlockSpec(memory_space=pl.ANY),
        scratch_shapes=[pltpu.VMEM((tm * SUBLANES, d // SUBLANES), F32), pltpu.SemaphoreType.DMA(())],
        compiler_params=_cp("arbitrary"),
        name="dispatch",
    )(dst, pad_dst.reshape(1, 1, npad), x, mod, g2)


def _experts_kernel(te_ref, used_ref, xs_ref, wgu_ref, wd_ref, y_ref, u_scr, acc_scr, *, th):
    i = pl.program_id(0)

    @pl.when(i < used_ref[0])
    def _():
        tr = u_scr.shape[0]
        for s in range(SUBLANES):
            u_scr[:, s * LANES:(s + 1) * LANES] = xs_ref[pl.ds(s, tr, stride=SUBLANES), :].astype(BF16)
        _swiglu_acc(u_scr, wgu_ref.at[0], wd_ref.at[0], acc_scr, th)
        y = acc_scr[...]
        for s in range(SUBLANES):
            y_ref[pl.ds(s, tr, stride=SUBLANES), :] = y[:, s * LANES:(s + 1) * LANES]


def _experts(xs3, tile_expert, used, wgu, wd, th):
    tr = ROW_TILE
    n_rows = xs3.shape[0]
    ne, hd, d = wd.shape
    tiles = tile_expert.shape[0]
    assert tiles * tr <= n_rows
    row = lambda i, te, us: (jnp.minimum(i, us[0] - 1), 0)
    grid_spec = pltpu.PrefetchScalarGridSpec(
        num_scalar_prefetch=2,
        grid=(tiles,),
        in_specs=[pl.BlockSpec((tr * SUBLANES, LANES), row),
                  _resident((1,) + wgu.shape[1:], lambda i, te, us: (te[i], 0, 0)),
                  _resident((1,) + wd.shape[1:], lambda i, te, us: (te[i], 0, 0))],
        out_specs=pl.BlockSpec((tr * SUBLANES, LANES), row),
        scratch_shapes=[pltpu.VMEM((tr, d), BF16), pltpu.VMEM((tr, d), F32)],
    )
    y = pl.pallas_call(
        functools.partial(_experts_kernel, th=th),
        out_shape=jax.ShapeDtypeStruct((tiles * tr * SUBLANES, LANES), F32),
        grid_spec=grid_spec,
        compiler_params=_cp("arbitrary"),
        name="experts",
    )(tile_expert, used, xs3.reshape(n_rows * SUBLANES, LANES), wgu, wd)
    return y.reshape(tiles * tr, SUBLANES, LANES)


def _combine_kernel(src_ref, y_ref, meta_ref, x_ref, mod_ref, g3_ref, out_ref, ybuf, sem):
    tm = x_ref.shape[0]

    def issue(t, carry):
        pltpu.make_async_copy(y_ref.at[src_ref[0, 0, t]], _row_tile(ybuf.at[0], t), sem).start(priority=0)
        pltpu.make_async_copy(y_ref.at[src_ref[0, 0, tm + t]], _row_tile(ybuf.at[1], t), sem).start(priority=1)
        return carry

    lax.fori_loop(0, tm, issue, 0, unroll=8)
    _drain(y_ref.at[0], _row_tile(ybuf.at[0], 0), sem, 2 * tm)
    meta = meta_ref[...]
    w1 = meta[:, META_W:META_W + 1]
    w2 = meta[:, META_W + 1:META_W + 2]
    ys = [w1 * ybuf[0, pl.ds(s, tm, stride=SUBLANES), :] + w2 * ybuf[1, pl.ds(s, tm, stride=SUBLANES), :]
          for s in range(SUBLANES)]
    ss = ys[0] * ys[0]
    for y in ys[1:]:
        ss = ss + y * y
    d = x_ref.shape[1]
    r = lax.rsqrt(jnp.sum(ss, axis=1, keepdims=True) / d + EPS)
    mod = mod_ref[0]
    for s, y in enumerate(ys):
        sl = slice(s * LANES, (s + 1) * LANES)
        out_ref[:, sl] = x_ref[:, sl] + mod[5:6, sl] * (y * r * g3_ref[:, sl])


def _combine(y3, dest, meta, x, seg, mod, g3):
    n, d = x.shape
    tm = seg.tile(512)
    steps = n // tm
    src = dest.reshape(steps, tm, 2).transpose(0, 2, 1).reshape(steps, 1, 2 * tm)
    return pl.pallas_call(
        _combine_kernel,
        out_shape=jax.ShapeDtypeStruct((n, d), F32),
        grid=(steps,),
        in_specs=[pl.BlockSpec((1, 1, 2 * tm), lambda j: (j, 0, 0), memory_space=pltpu.SMEM),
                  pl.BlockSpec(memory_space=pl.ANY),
                  pl.BlockSpec((tm, LANES), lambda j: (j, 0)),
                  pl.BlockSpec((tm, d), lambda j: (j, 0)),
                  seg.mod_spec(tm, d, 1), _full(g3, 1)],
        out_specs=pl.BlockSpec((tm, d), lambda j: (j, 0)),
        scratch_shapes=[pltpu.VMEM((2, tm * SUBLANES, d // SUBLANES), F32), pltpu.SemaphoreType.DMA(())],
        input_output_aliases={3: 0},
        compiler_params=_cp("arbitrary"),
        name="combine",
    )(src, y3, meta, x, mod, g3)


def _moe_routed(x, seg, mod, g2, wr, br, wgu, wd, g3, th):
    n, d = x.shape
    assert d == SUBLANES * LANES
    tr = ROW_TILE
    meta, cnt = _route(x, seg, mod, g2, wr, br)
    counts = cnt[0, :N_EXPERTS].astype(jnp.int32)
    sizes = ((counts + tr - 1) // tr) * tr
    ends = jnp.cumsum(sizes)
    offs = ends - sizes
    mi = meta[:, :META_W].astype(jnp.int32)
    eidx = mi[:, META_E:META_E + 2]
    dest = jnp.sum(jnp.where(eidx[:, :, None] == jnp.arange(N_EXPERTS)[None, None, :],
                             offs[None, None, :], 0), axis=2) + mi[:, META_RANK:META_RANK + 2]
    dest = dest.astype(jnp.int32)
    tiles = (2 * n) // tr + N_EXPERTS
    n_rows = tiles * tr
    used = (ends[-1] // tr).astype(jnp.int32).reshape(1)
    tile_start = jnp.arange(tiles, dtype=jnp.int32) * tr
    tile_expert = jnp.sum(tile_start[:, None] >= ends[None, :], axis=1).astype(jnp.int32)
    last_expert = jnp.sum(jnp.maximum(ends[-1] - tr, 0) >= ends).astype(jnp.int32)
    tile_expert = jnp.where(tile_start < ends[-1], tile_expert, last_expert)
    pad = jnp.arange(N_EXPERTS * tr, dtype=jnp.int32).reshape(N_EXPERTS, tr)
    pad_row = offs[:, None] + counts[:, None] + jnp.arange(tr, dtype=jnp.int32)[None, :]
    pad_dst = jnp.where(pad_row < ends[:, None], pad_row, n_rows + pad).reshape(-1).astype(jnp.int32)
    xs3 = _dispatch(x, seg, mod, g2, dest, pad_dst, n_rows + N_EXPERTS * tr)
    y3 = _experts(xs3, tile_expert, used, wgu, wd, th)
    return _combine(y3, dest, meta, x, seg, mod, g3)


def _proj_h_body(u, w_refs, out_refs):
    for w_ref, o_ref in zip(w_refs, out_refs):
        o_ref[...] = jnp.dot(u, w_ref[...], preferred_element_type=F32).astype(BF16)


def _proj_h_lat_kernel(x_ref, mod_ref, g_ref, *refs, rows_per_col):
    n = len(refs) // 2
    w_refs, out_refs = refs[:n], refs[n:]
    x = jnp.swapaxes(x_ref[...], 0, 1)
    x = x.reshape(SUBLANES * rows_per_col, x.shape[2])
    u = _normmod(x, g_ref[...], mod_ref[0], 0).astype(BF16)
    _proj_h_body(u, w_refs, out_refs)


def _proj_h_ctx_kernel(x_ref, mod_ref, g_ref, *refs):
    n = len(refs) // 2
    u = _normmod(x_ref[...], g_ref[...], mod_ref[0], 0).astype(BF16)
    _proj_h_body(u, refs[:n], refs[n:])


def _proj_h_lat(x, seg, mod, g, ws):
    n, d = x.shape
    grid_rows = seg.seqlen // GRID_W
    x3 = x.reshape(n // GRID_W, GRID_W, d)
    tcol = SUBLANES * grid_rows
    ncb = GRID_W // SUBLANES
    return pl.pallas_call(
        functools.partial(_proj_h_lat_kernel, rows_per_col=grid_rows),
        out_shape=[jax.ShapeDtypeStruct((n, w.shape[1]), BF16) for w in ws],
        grid=(seg.batch, ncb),
        in_specs=[pl.BlockSpec((grid_rows, SUBLANES, d), lambda b, c: (b, c, 0)),
                  pl.BlockSpec((1, N_MOD, d), lambda b, c: (b, 0, 0)), _full(g, 2)]
                 + [_full(w, 2) for w in ws],
        out_specs=[pl.BlockSpec((tcol, w.shape[1]), lambda b, c: (b * ncb + c, 0)) for w in ws],
        compiler_params=_cp("arbitrary", "arbitrary"),
        name="hgrn_proj_lat",
    )(x3, mod, g, *ws)


def _proj_h_ctx(x, seg, mod, g, ws):
    n, d = x.shape
    tm = seg.tile(512)
    return pl.pallas_call(
        _proj_h_ctx_kernel,
        out_shape=[jax.ShapeDtypeStruct((n, w.shape[1]), BF16) for w in ws],
        grid=(n // tm,),
        in_specs=[pl.BlockSpec((tm, d), lambda j: (j, 0)), seg.mod_spec(tm, d, 1), _full(g, 1)]
                 + [_full(w, 1) for w in ws],
        out_specs=[pl.BlockSpec((tm, w.shape[1]), lambda j: (j, 0)) for w in ws],
        compiler_params=_cp("arbitrary"),
        name="hgrn_proj_ctx",
    )(x, mod, g, *ws)


HGRN_HEAD_GROUP = 4


def _hgrn_prologue(f_ref, bias_ref, lb_ref, reverse, d):
    lb = lb_ref[d:d + 1, :]
    f = lb + (1.0 - lb) * _sigmoid(f_ref[...].astype(F32) + bias_ref[d:d + 1, :])
    return f, _cumsum_time(jnp.log(f), _cumsum_mat(CHUNK, reverse))


def _hgrn_heads(q_ref, v_ref, pro, h_ref, s_scr, reverse, d, heads):
    L = CHUNK
    f, a_all = pro
    r = lax.broadcasted_iota(jnp.int32, (L, L), 0)
    c = lax.broadcasted_iota(jnp.int32, (L, L), 1)
    causal = (c >= r) if reverse else (c <= r)
    same64 = (r // 64) == (c // 64)
    diag32 = ((r // 32) == (c // 32)) & causal
    t = lax.broadcasted_iota(jnp.int32, (L, 1), 0)
    if reverse:
        late0, late1 = t < 64, (t % 64) < 32
        a0, a1 = (64,), (32, 96)
        a2 = (31, 63, 95, 127)
        last = 0
    else:
        late0, late1 = t >= 64, (t % 64) >= 32
        a0, a1 = (63,), (31, 95)
        a2 = (0, 32, 64, 96)
        last = L - 1
    staged = []
    for h in heads:
        sl = slice(h * H_DK, (h + 1) * H_DK)
        a = a_all[:, sl]
        q = q_ref[:, sl].astype(F32)
        k = 1.0 - f[:, sl]
        anc0 = jnp.broadcast_to(a[a0[0]:a0[0] + 1, :], (L, H_DK))
        anc1 = jnp.where(t < 64, a[a1[0]:a1[0] + 1, :], a[a1[1]:a1[1] + 1, :])
        anc2 = jnp.where(t < 32, a[a2[0]:a2[0] + 1, :],
                         jnp.where(t < 64, a[a2[1]:a2[1] + 1, :],
                                   jnp.where(t < 96, a[a2[2]:a2[2] + 1, :], a[a2[3]:a2[3] + 1, :])))

        def level(anc, late):
            qs = q * jnp.exp(a - anc)
            ks = k * jnp.exp(anc - a)
            if late is not None:
                qs = jnp.where(late, qs, 0.0)
                ks = jnp.where(late, 0.0, ks)
            return lax.dot_general(qs.astype(BF16), ks.astype(BF16), NT_DIMS, preferred_element_type=F32)

        s = jnp.where(diag32, level(anc2, None), jnp.where(same64, level(anc1, late1), level(anc0, late0)))
        qd = (q * jnp.exp(a)).astype(BF16)
        a_last = a[last:last + 1, :]
        kd = (k * jnp.exp(a_last - a)).astype(BF16)
        staged.append((s.astype(BF16), qd, kd, jnp.exp(a_last)))
    for h, (s, qd, kd, dec) in zip(heads, staged):
        v = v_ref[:, h * H_DV:(h + 1) * H_DV]
        st = s_scr[d, h]
        o = (jnp.dot(s, v, preferred_element_type=F32)
             + lax.dot_general(qd, st.astype(BF16), NT_DIMS, preferred_element_type=F32))
        h_ref[:, h * H_DV:(h + 1) * H_DV] = o.astype(BF16)
        s_scr[d, h] = st * dec + lax.dot_general(v, kd, TN_DIMS, preferred_element_type=F32)


def _hgrn_scan_kernel(qf, vf, ff, qb, vb, fb, bias_ref, lb_ref, s0_ref, hf_ref, hb_ref, st_ref, s_scr):
    n = pl.program_id(1)

    @pl.when(n == 0)
    def _():
        s_scr[...] = s0_ref[0]

    pro_f = _hgrn_prologue(ff, bias_ref, lb_ref, False, 0)
    pro_b = _hgrn_prologue(fb, bias_ref, lb_ref, True, 1)
    for h0 in range(0, H_HEADS, HGRN_HEAD_GROUP):
        heads = range(h0, h0 + HGRN_HEAD_GROUP)
        _hgrn_heads(qf, vf, pro_f, hf_ref, s_scr, False, 0, heads)
        _hgrn_heads(qb, vb, pro_b, hb_ref, s_scr, True, 1, heads)

    @pl.when(n == pl.num_programs(1) - 1)
    def _():
        st_ref[0] = s_scr[...]


def _hgrn_scan(q, v, f_f, f_b, bias, lb, s0, seg):
    L = CHUNK
    assert seg.seqlen % L == 0
    n = seg.seqlen // L
    fwd = lambda b, i: (b * n + i, 0)
    bwd = lambda b, i: (b * n + (n - 1 - i), 0)
    rows_, w = q.shape
    st = lambda b, i: (b, 0, 0, 0, 0)
    return pl.pallas_call(
        _hgrn_scan_kernel,
        out_shape=[jax.ShapeDtypeStruct((rows_, w), BF16), jax.ShapeDtypeStruct((rows_, w), BF16),
                   jax.ShapeDtypeStruct(s0.shape, F32)],
        grid=(seg.batch, n),
        in_specs=[pl.BlockSpec((L, w), fwd), pl.BlockSpec((L, w), fwd), pl.BlockSpec((L, w), fwd),
                  pl.BlockSpec((L, w), bwd), pl.BlockSpec((L, w), bwd), pl.BlockSpec((L, w), bwd),
                  _full(bias, 2), _full(lb, 2), pl.BlockSpec((1,) + s0.shape[1:], st)],
        out_specs=[pl.BlockSpec((L, w), fwd), pl.BlockSpec((L, w), bwd),
                   pl.BlockSpec((1,) + s0.shape[1:], st)],
        scratch_shapes=[pltpu.VMEM(s0.shape[1:], F32)],
        compiler_params=_cp("arbitrary", "arbitrary"),
        name="hgrn_scan",
    )(q, v, f_f, q, v, f_b, bias, lb, s0)


def _mix_out_col_kernel(hf_ref, hb_ref, o_ref, x_ref, mod_ref, gh_ref, w_ref, g1_ref, out_ref,
                        *, n_heads, rows_per_col):
    mod = mod_ref[0]
    y = _head_out(hf_ref[...], hb_ref[...], o_ref[...], gh_ref[...], w_ref[...], n_heads)
    upd = mod[2:3] * (_rms(y) * g1_ref[...])
    upd = jnp.swapaxes(upd.reshape(SUBLANES, rows_per_col, upd.shape[1]), 0, 1)
    out_ref[...] = x_ref[...] + upd


def _mix_out_col(hf, hb, o, x, seg, mod, gh, w, g1, n_heads):
    n, d = x.shape
    grid_rows = seg.seqlen // GRID_W
    tcol = SUBLANES * grid_rows
    ncb = GRID_W // SUBLANES
    x3 = x.reshape(n // GRID_W, GRID_W, d)
    hmap = lambda b, c: (b * ncb + c, 0)
    xmap = lambda b, c: (b, c, 0)
    consts = (gh, w, g1)
    out = pl.pallas_call(
        functools.partial(_mix_out_col_kernel, n_heads=n_heads, rows_per_col=grid_rows),
        out_shape=jax.ShapeDtypeStruct(x3.shape, F32),
        grid=(seg.batch, ncb),
        in_specs=[pl.BlockSpec((tcol, hf.shape[1]), hmap), pl.BlockSpec((tcol, hf.shape[1]), hmap),
                  pl.BlockSpec((tcol, o.shape[1]), hmap),
                  pl.BlockSpec((grid_rows, SUBLANES, d), xmap),
                  pl.BlockSpec((1, N_MOD, d), lambda b, c: (b, 0, 0))] + [_full(a, 2) for a in consts],
        out_specs=pl.BlockSpec((grid_rows, SUBLANES, d), xmap),
        input_output_aliases={3: 0},
        compiler_params=_cp("arbitrary", "arbitrary"),
        name="mix_out_col",
    )(hf, hb, o, x3, mod, *consts)
    return out.reshape(n, d)


def kernel(x, c, ctx, c_ctx, mod_w, mod_b, norm_g, m_w_in, m_b_gate, m_w_conv, m_g_head, m_w_out,
           h_w_in, h_b_f, h_lb_raw, h_g_head, h_w_out, f_w_gu, f_w_down,
           e_w_router, e_b_router, e_w_gu, e_w_down):
    batch, seq, d = x.shape
    ctx_len = ctx.shape[1]
    depth = mod_w.shape[0]
    assert seq % GRID_W == 0 and seq // GRID_W == CHUNK and ctx_len % CHUNK == 0
    assert batch < MOD_ROWS and d % LANES == 0
    lat_seg = _Seg(batch, seq, False)
    ctx_seg = _Seg(batch, ctx_len, True)

    c_all = jnp.zeros((MOD_ROWS, d), F32).at[:batch].set(c).at[batch].set(c_ctx)
    mod_all = _modulation(c_all, mod_w, mod_b).reshape(depth, MOD_ROWS, N_MOD, d)

    lb_all = jax.nn.softmax(h_lb_raw.astype(F32), axis=0)
    lb_all = jnp.cumsum(lb_all, axis=0) - lb_all[0]

    xl = x.reshape(batch * seq, d)
    xc = ctx.reshape(batch * ctx_len, d)
    row = lambda v: v.reshape(1, -1).astype(F32)

    for i in range(depth):
        j = i // 2
        with_ctx = i < depth - 1
        mod = mod_all[i]
        g0, g1, g2, g3 = (row(norm_g[i, s]) for s in range(4))
        if i % 2 == 0:
            qk_w = 2 * M_HEADS * M_DK
            v_w = M_HEADS * M_DV
            w_in = m_w_in[j]
            wqk = w_in[:, :qk_w].astype(BF16)
            wv = w_in[:, qk_w:qk_w + v_w].astype(BF16)
            wo = w_in[:, qk_w + v_w:qk_w + 2 * v_w].astype(BF16)
            ng = 4 * M_HEADS
            wg = jnp.zeros((d, LANES), F32).at[:, :ng].set(w_in[:, qk_w + 2 * v_w:]).astype(BF16)
            gb = jnp.zeros((1, LANES), F32).at[0, :ng].set(m_b_gate[j].reshape(-1))
            col = jnp.arange(LANES)
            gm = (((col // M_HEADS) % 2 == 1) & (col < ng)).astype(F32).reshape(1, LANES)
            pw = (mod, g0, wqk, wv, wo, wg, m_w_conv[j].astype(F32), gb, gm)
            qc, kc, vc, oc, gc = _proj_m(xc, ctx_seg, *pw)
            ql, kl, vl, ol, gl = _proj_m(xl, lat_seg, *pw)
            s0 = jnp.zeros((batch, 2, M_HEADS // 2, LANES, 2 * M_DV), F32)
            m0 = jnp.zeros((batch, 2 * M_HEADS, LANES), F32)
            hcf, hcb, s1, m1 = _mlstm_scan(qc, kc, vc, gc, s0, m0, ctx_seg)
            hlf, hlb, _, _ = _mlstm_scan(ql, kl, vl, gl, s1, m1, lat_seg)
            gh = row(m_g_head[j])
            wout = m_w_out[j].astype(BF16)
            xl = _mix_out(hlf, hlb, ol, xl, lat_seg, mod, gh, wout, g1, M_HEADS, in_place=i > 0)
            if with_ctx:
                xc = _mix_out(hcf, hcb, oc, xc, ctx_seg, mod, gh, wout, g1, M_HEADS, in_place=i > 0)
        else:
            kw = H_HEADS * H_DK
            w_in = h_w_in[j].astype(BF16)
            ws = [w_in[:, s * kw:(s + 1) * kw] for s in range(5)]
            qc, vc, ffc, fbc, gc = _proj_h_ctx(xc, ctx_seg, mod, g0, ws)
            ql, vl, ffl, fbl, gl = _proj_h_lat(xl, lat_seg, mod, g0, ws)
            bias = h_b_f[j].astype(F32)
            lb = lb_all[j]
            s0 = jnp.zeros((batch, 2, H_HEADS, H_DV, H_DK), F32)
            hcf, hcb, s1 = _hgrn_scan(qc, vc, ffc, fbc, bias, lb, s0, ctx_seg)
            hlf, hlb, _ = _hgrn_scan(ql, vl, ffl, fbl, bias, lb, s1, lat_seg)
            gh = row(h_g_head[j])
            wout = h_w_out[j].astype(BF16)
            xl = _mix_out_col(hlf, hlb, gl, xl, lat_seg, mod, gh, wout, g1, H_HEADS)
            if with_ctx:
                xc = _mix_out(hcf, hcb, gc, xc, ctx_seg, mod, gh, wout, g1, H_HEADS)
        streams = [(xl, lat_seg)] + ([(xc, ctx_seg)] if with_ctx else [])
        outs = []
        if i % 2 == 0:
            wgu, wd = f_w_gu[j].astype(BF16), f_w_down[j].astype(BF16)
            for xs, seg in streams:
                outs.append(_ffn(xs, seg, mod, g2, wgu, wd, g3, th=256))
        else:
            wr = jnp.zeros((d, LANES), F32).at[:, :N_EXPERTS].set(e_w_router[j])
            br = jnp.zeros((1, LANES), F32).at[0, :N_EXPERTS].set(e_b_router[j])
            wgu, wd = e_w_gu[j].astype(BF16), e_w_down[j].astype(BF16)
            outs.append(_moe_routed(xl, lat_seg, mod, g2, wr, br, wgu, wd, g3, th=256))
            if with_ctx:
                u, comb = _router(xc, ctx_seg, mod, g2, wr, br)
                outs.append(_moe(u, comb, xc, ctx_seg, mod, wgu, wd, g3, th=256))
        xl = outs[0]
        if with_ctx:
            xc = outs[1]
    return xl.reshape(batch, seq, d)
```

```python
import functools

import jax
import jax.numpy as jnp
from jax import lax
from jax.experimental import pallas as pl
from jax.experimental.pallas import tpu as pltpu

F32 = jnp.float32
BF16 = jnp.bfloat16
EPS = 1e-6
NEG = -1e30

GRID_W = 64
N_MOD = 6
M_HEADS, M_DK, M_DV = 8, 64, 128
H_HEADS, H_DK, H_DV = 8, 128, 128
N_EXPERTS = 8
LANES = 128
SUBLANES = 8
CHUNK = 128
MOD_ROWS = 16
VMEM_LIMIT = 56 * 1024 * 1024

HI = lax.Precision.HIGHEST
NT_DIMS = (((1,), (1,)), ((), ()))
TN_DIMS = (((0,), (0,)), ((), ()))


def _cp(*sem):
    return pltpu.CompilerParams(dimension_semantics=sem, vmem_limit_bytes=VMEM_LIMIT)


def _full(a, nargs):
    zeros = (0,) * a.ndim
    return pl.BlockSpec(a.shape, lambda *_: zeros)


def _sigmoid(x):
    return 1.0 / (1.0 + jnp.exp(-x))


def _silu(x):
    return x * _sigmoid(x)


def _log_sigmoid(x):
    return jnp.minimum(x, 0.0) - jnp.log(1.0 + jnp.exp(-jnp.abs(x)))


def _rms(x):
    return x * lax.rsqrt(jnp.mean(x * x, axis=-1, keepdims=True) + EPS)


def _normmod(x, g, mod, s):
    return _rms(x) * (g * (1.0 + mod[s + 1:s + 2])) + mod[s:s + 1]


def _split2(x):
    hi = x.astype(BF16)
    lo = (x - hi.astype(F32)).astype(BF16)
    return hi, lo


def _cumsum_mat(n, reverse):
    r = lax.broadcasted_iota(jnp.int32, (n, n), 0)
    c = lax.broadcasted_iota(jnp.int32, (n, n), 1)
    return jnp.where((c >= r) if reverse else (c <= r), 1.0, 0.0).astype(BF16)


def _cumsum_time(x, tri):
    hi, lo = _split2(x)
    return (jnp.dot(tri, hi, preferred_element_type=F32)
            + jnp.dot(tri, lo, preferred_element_type=F32))


class _Seg:
    def __init__(self, batch, seqlen, is_ctx):
        self.batch, self.seqlen, self.is_ctx = batch, seqlen, is_ctx
        self.n = batch * seqlen

    def tile(self, pref):
        tm = min(pref, self.n if self.is_ctx else self.seqlen)
        assert self.n % tm == 0 and (self.seqlen % tm == 0 or tm % self.seqlen == 0)
        return tm

    def mod_spec(self, tm, d, nargs):
        if self.is_ctx:
            row = lambda j: self.batch
        else:
            row = lambda j: (j * tm) // self.seqlen
        if nargs == 1:
            return pl.BlockSpec((1, N_MOD, d), lambda j: (row(j), 0, 0))
        if nargs == 2:
            return pl.BlockSpec((1, N_MOD, d), lambda j, k: (row(j), 0, 0))
        return pl.BlockSpec((1, N_MOD, d), lambda j, e, k: (row(j), 0, 0))


def _mod_kernel(c_ref, w_ref, b_ref, o_ref):
    c = c_ref[...]
    o_ref[0] = jnp.dot(_silu(c), w_ref[0], preferred_element_type=F32, precision=HI) + b_ref[0]


def _modulation(c_all, mod_w, mod_b):
    depth, d, n = mod_w.shape
    tn = n // 4
    return pl.pallas_call(
        _mod_kernel,
        out_shape=jax.ShapeDtypeStruct((depth, MOD_ROWS, n), F32),
        grid=(depth, n // tn),
        in_specs=[pl.BlockSpec((MOD_ROWS, d), lambda i, j: (0, 0)),
                  pl.BlockSpec((1, d, tn), lambda i, j: (i, 0, j)),
                  pl.BlockSpec((1, 1, tn), lambda i, j: (i, 0, j))],
        out_specs=pl.BlockSpec((1, MOD_ROWS, tn), lambda i, j: (i, 0, j)),
        compiler_params=_cp("arbitrary", "arbitrary"),
        name="modulation",
    )(c_all, mod_w, mod_b.reshape(depth, 1, n))


def _proj_m_kernel(xp_ref, x_ref, xn_ref, mod_ref, g_ref, wqk_ref, wv_ref, wo_ref, wg_ref,
                   cw_ref, gb_ref, gm_ref, q_ref, kt_ref, v_ref, o_ref, gt_ref, *, tm, seqlen):
    j = pl.program_id(0)
    mod = mod_ref[0]
    g = g_ref[...]
    u = _normmod(x_ref[...], g, mod, 0).astype(BF16)
    uh = _normmod(jnp.concatenate([xp_ref[...], xn_ref[...]], axis=0), g, mod, 0).astype(BF16)
    wqk = wqk_ref[...]
    z = jnp.dot(u, wqk, preferred_element_type=F32)
    zh = jnp.dot(uh, wqk, preferred_element_type=F32)
    local = lax.broadcasted_iota(jnp.int32, (tm, 1), 0)
    pos = lax.rem(j * tm + local, seqlen)
    zp = jnp.where(local == 0, zh[SUBLANES - 1:SUBLANES], pltpu.roll(z, 1, 0))
    zn = jnp.where(local == tm - 1, zh[SUBLANES:SUBLANES + 1], pltpu.roll(z, tm - 1, 0))
    zp = jnp.where(pos == 0, 0.0, zp)
    zn = jnp.where(pos == seqlen - 1, 0.0, zn)
    cw = cw_ref[...]
    a = _silu(cw[0:1] * zp + cw[1:2] * z + cw[2:3] * zn)
    half = a.shape[1] // 2
    q_ref[...] = a[:, :half].astype(BF16)
    kt_ref[...] = (a[:, half:] * (M_DK ** -0.5)).T.astype(BF16)
    v_ref[...] = jnp.dot(u, wv_ref[...], preferred_element_type=F32).astype(BF16)
    o_ref[...] = jnp.dot(u, wo_ref[...], preferred_element_type=F32).astype(BF16)
    zg = jnp.dot(u, wg_ref[...], preferred_element_type=F32) + gb_ref[...]
    gt_ref[...] = jnp.where(gm_ref[...] > 0.5, _log_sigmoid(zg), zg)


def _proj_m(x, seg, mod, g, wqk, wv, wo, wg, cw, gb, gm):
    n, d = x.shape
    tm = seg.tile(512)
    nb8 = n // SUBLANES
    r8 = tm // SUBLANES
    kern = functools.partial(_proj_m_kernel, tm=tm, seqlen=seg.seqlen)
    nqk = wqk.shape[1]
    consts = (g, wqk, wv, wo, wg, cw, gb, gm)
    return pl.pallas_call(
        kern,
        out_shape=[jax.ShapeDtypeStruct((n, nqk // 2), BF16),
                   jax.ShapeDtypeStruct((nqk // 2, n), BF16),
                   jax.ShapeDtypeStruct((n, wv.shape[1]), BF16),
                   jax.ShapeDtypeStruct((n, wo.shape[1]), BF16),
                   jax.ShapeDtypeStruct((n, LANES), F32)],
        grid=(n // tm,),
        in_specs=[pl.BlockSpec((SUBLANES, d), lambda j: (jnp.maximum(j * r8 - 1, 0), 0)),
                  pl.BlockSpec((tm, d), lambda j: (j, 0)),
                  pl.BlockSpec((SUBLANES, d), lambda j: (jnp.minimum((j + 1) * r8, nb8 - 1), 0)),
                  seg.mod_spec(tm, d, 1)] + [_full(a, 1) for a in consts],
        out_specs=[pl.BlockSpec((tm, nqk // 2), lambda j: (j, 0)),
                   pl.BlockSpec((nqk // 2, tm), lambda j: (0, j)),
                   pl.BlockSpec((tm, wv.shape[1]), lambda j: (j, 0)),
                   pl.BlockSpec((tm, wo.shape[1]), lambda j: (j, 0)),
                   pl.BlockSpec((tm, LANES), lambda j: (j, 0))],
        compiler_params=_cp("arbitrary"),
        name="mlstm_proj",
    )(x, x, x, mod, *consts)


MLSTM_PAIR_GROUP = 2


def _mlstm_gates(g_ref, m_prev, reverse, d):
    L = CHUNK
    g = g_ref[...]
    b = _cumsum_time(g, _cumsum_mat(L, reverse))
    gT, bT = g.T, b.T
    c_ig = 2 * M_HEADS * d
    b_rows = bT[c_ig + M_HEADS:c_ig + 2 * M_HEADS, :]
    beta = gT[c_ig:c_ig + M_HEADS, :] - b_rows
    lane = lax.broadcasted_iota(jnp.int32, (1, L), 1)
    cm = beta
    k = 1
    while k < L:
        if reverse:
            cm = jnp.maximum(cm, jnp.where(lane < L - k, pltpu.roll(cm, L - k, 1), NEG))
        else:
            cm = jnp.maximum(cm, jnp.where(lane >= k, pltpu.roll(cm, k, 1), NEG))
        k *= 2
    last = 0 if reverse else L - 1
    mu = jnp.maximum(cm, m_prev)
    mu_last = mu[:, last:last + 1]
    rows = jnp.concatenate([beta, jnp.exp(beta - mu_last)], axis=0)
    cols = jnp.concatenate([mu, jnp.exp(m_prev - mu), jnp.exp(-(b_rows + mu)),
                            jnp.zeros((LANES - 3 * M_HEADS, L), F32)], axis=0).T
    half = lax.broadcasted_iota(jnp.int32, (1, LANES), 1) < LANES // 2
    small = jnp.where(half, jnp.exp(m_prev - mu_last), b_rows[:, last:last + 1] + mu_last)
    return rows, cols, small


def _mlstm_heads(q_ref, kt_ref, v_ref, rows_g, cols, small, h_ref, s_scr, reverse, d, pairs):
    L = CHUNK
    r = lax.broadcasted_iota(jnp.int32, (L, L), 0)
    c = lax.broadcasted_iota(jnp.int32, (L, L), 1)
    causal = (c >= r) if reverse else (c <= r)
    lane = lax.broadcasted_iota(jnp.int32, (1, LANES), 1)
    ones = jnp.ones((L, M_DV), BF16)
    staged = []
    for p in pairs:
        q_p = q_ref[:, p * LANES:(p + 1) * LANES]
        kt_p = kt_ref[p * LANES:(p + 1) * LANES, :]
        for e in range(2):
            h = 2 * p + e
            q_e = jnp.where((lane // M_DK) == e, q_p, jnp.zeros_like(q_p))
            qk = jnp.dot(q_e, kt_p, preferred_element_type=F32)
            s = jnp.where(causal, qk * jnp.exp(rows_g[h:h + 1, :] - cols[:, h:h + 1]), 0.0).astype(BF16)
            ktw = (kt_p[e * M_DK:(e + 1) * M_DK, :].astype(F32)
                   * rows_g[M_HEADS + h:M_HEADS + h + 1, :]).astype(BF16)
            staged.append((p, e, q_e, s, ktw))
    for p, e, q_e, s, ktw in staged:
        h = 2 * p + e
        rows = slice(e * M_DK, (e + 1) * M_DK)
        vp = jnp.concatenate([v_ref[:, h * M_DV:(h + 1) * M_DV], ones], axis=1)
        w_inter = cols[:, M_HEADS + h:M_HEADS + h + 1]
        floor = cols[:, 2 * M_HEADS + h:2 * M_HEADS + h + 1]
        tot = (jnp.dot(s, vp, preferred_element_type=F32)
               + w_inter * jnp.dot(q_e, s_scr[d, p].astype(BF16), preferred_element_type=F32))
        den = jnp.maximum(jnp.abs(tot[:, M_DV:]), floor)
        h_ref[:, h * M_DV:(h + 1) * M_DV] = (tot[:, :M_DV] / den).astype(BF16)
        s_scr[d, p, rows, :] = (small[h:h + 1, 0:1] * s_scr[d, p, rows, :]
                                + jnp.dot(ktw, vp, preferred_element_type=F32))


def _mlstm_scan_kernel(qf, kf, vf, gf, gf_next, qb, kb, vb, gb, gb_next, s0_ref, m0_ref,
                       hf_ref, hb_ref, st_ref, mt_ref, s_scr, m_scr, rows_scr, cols_scr, small_scr):
    n = pl.program_id(1)
    H = M_HEADS
    g_cur, g_next = (gf, gb), (gf_next, gb_next)

    @pl.when(n == 0)
    def _():
        s_scr[...] = s0_ref[0]
        for d in range(2):
            rows_scr[d], cols_scr[d], small_scr[d] = _mlstm_gates(
                g_cur[d], m0_ref[0, d * H:(d + 1) * H, 0:1], d == 1, d)

    cur = [(rows_scr[d], cols_scr[d], small_scr[d]) for d in range(2)]
    m_new = [cur[d][2][:, LANES // 2:LANES // 2 + 1] for d in range(2)]
    m_scr[...] = jnp.broadcast_to(jnp.concatenate(m_new, axis=0), m_scr.shape)
    for d in range(2):
        rows_scr[d], cols_scr[d], small_scr[d] = _mlstm_gates(g_next[d], m_new[d], d == 1, d)
    for p0 in range(0, M_HEADS // 2, MLSTM_PAIR_GROUP):
        pairs = range(p0, p0 + MLSTM_PAIR_GROUP)
        _mlstm_heads(qf, kf, vf, *cur[0], hf_ref, s_scr, False, 0, pairs)
        _mlstm_heads(qb, kb, vb, *cur[1], hb_ref, s_scr, True, 1, pairs)

    @pl.when(n == pl.num_programs(1) - 1)
    def _():
        st_ref[0] = s_scr[...]
        mt_ref[0] = m_scr[...]


def _mlstm_scan(q, kt, v, gates, s0, m0, seg):
    L = CHUNK
    assert seg.seqlen % L == 0
    n = seg.seqlen // L
    fwd = lambda b, i: (b * n + i, 0)
    bwd = lambda b, i: (b * n + (n - 1 - i), 0)
    fwd_next = lambda b, i: (b * n + jnp.minimum(i + 1, n - 1), 0)
    bwd_next = lambda b, i: (b * n + jnp.maximum(n - 2 - i, 0), 0)
    fwd_t = lambda b, i: (0, b * n + i)
    bwd_t = lambda b, i: (0, b * n + (n - 1 - i))
    dq, dv = q.shape[1], v.shape[1]
    st = lambda b, i: (b, 0, 0, 0, 0)
    mt = lambda b, i: (b, 0, 0)
    return pl.pallas_call(
        _mlstm_scan_kernel,
        out_shape=[jax.ShapeDtypeStruct((seg.n, dv), BF16), jax.ShapeDtypeStruct((seg.n, dv), BF16),
                   jax.ShapeDtypeStruct(s0.shape, F32), jax.ShapeDtypeStruct(m0.shape, F32)],
        grid=(seg.batch, n),
        in_specs=[pl.BlockSpec((L, dq), fwd), pl.BlockSpec((dq, L), fwd_t), pl.BlockSpec((L, dv), fwd),
                  pl.BlockSpec((L, LANES), fwd), pl.BlockSpec((L, LANES), fwd_next),
                  pl.BlockSpec((L, dq), bwd), pl.BlockSpec((dq, L), bwd_t), pl.BlockSpec((L, dv), bwd),
                  pl.BlockSpec((L, LANES), bwd), pl.BlockSpec((L, LANES), bwd_next),
                  pl.BlockSpec((1,) + s0.shape[1:], st), pl.BlockSpec((1,) + m0.shape[1:], mt)],
        out_specs=[pl.BlockSpec((L, dv), fwd), pl.BlockSpec((L, dv), bwd),
                   pl.BlockSpec((1,) + s0.shape[1:], st), pl.BlockSpec((1,) + m0.shape[1:], mt)],
        scratch_shapes=[pltpu.VMEM(s0.shape[1:], F32), pltpu.VMEM(m0.shape[1:], F32),
                        pltpu.VMEM((2, 2 * M_HEADS, L), F32), pltpu.VMEM((2, L, LANES), F32),
                        pltpu.VMEM((2, M_HEADS, LANES), F32)],
        compiler_params=_cp("arbitrary", "arbitrary"),
        name="mlstm_scan",
    )(q, kt, v, gates, gates, q, kt, v, gates, gates, s0, m0)


def _head_out(hf, hb, gate_pre, gh, w, n_heads):
    h = hf.astype(F32) + hb.astype(F32)
    dv = h.shape[1] // n_heads
    parts = [_rms(h[:, i * dv:(i + 1) * dv]) for i in range(n_heads)]
    hn = jnp.concatenate(parts, axis=1) * gh * _sigmoid(gate_pre.astype(F32))
    return jnp.dot(hn.astype(BF16), w, preferred_element_type=F32)


def _mix_out_kernel(hf_ref, hb_ref, o_ref, x_ref, mod_ref, gh_ref, w_ref, g1_ref, out_ref, *, n_heads):
    mod = mod_ref[0]
    y = _head_out(hf_ref[...], hb_ref[...], o_ref[...], gh_ref[...], w_ref[...], n_heads)
    out_ref[...] = x_ref[...] + mod[2:3] * (_rms(y) * g1_ref[...])


def _mix_out(hf, hb, o, x, seg, mod, gh, w, g1, n_heads, in_place=True):
    n, d = x.shape
    tm = seg.tile(512)
    rmap = lambda j: (j, 0)
    consts = (gh, w, g1)
    return pl.pallas_call(
        functools.partial(_mix_out_kernel, n_heads=n_heads),
        out_shape=jax.ShapeDtypeStruct(x.shape, F32),
        grid=(n // tm,),
        in_specs=[pl.BlockSpec((tm, hf.shape[1]), rmap), pl.BlockSpec((tm, hf.shape[1]), rmap),
                  pl.BlockSpec((tm, o.shape[1]), rmap), pl.BlockSpec((tm, d), rmap),
                  seg.mod_spec(tm, d, 1)] + [_full(a, 1) for a in consts],
        out_specs=pl.BlockSpec((tm, d), rmap),
        input_output_aliases={3: 0} if in_place else {},
        compiler_params=_cp("arbitrary"),
        name="mix_out",
    )(hf, hb, o, x, mod, *consts)


def _swiglu_acc(u_scr, wgu_ref, wd_ref, acc_scr, th):
    hd = wd_ref.shape[0]
    assert hd % th == 0 and hd % LANES == 0

    def chunk(k, carry):
        off = pl.multiple_of(k * th, th)
        u = u_scr[...]
        gg = jnp.dot(u, wgu_ref[:, pl.ds(off, th)], preferred_element_type=F32)
        up = jnp.dot(u, wgu_ref[:, pl.ds(pl.multiple_of(hd + off, LANES), th)], preferred_element_type=F32)
        hh = (_silu(gg) * up).astype(BF16)
        acc_scr[...] += jnp.dot(hh, wd_ref[pl.ds(off, th), :], preferred_element_type=F32)
        return carry

    acc_scr[...] = jnp.zeros_like(acc_scr)
    lax.fori_loop(0, hd // th, chunk, 0)


def _ffn_kernel(x_ref, mod_ref, g2_ref, wgu_ref, wd_ref, g3_ref, out_ref, u_scr, acc_scr, *, th):
    u_scr[...] = _normmod(x_ref[...], g2_ref[...], mod_ref[0], 3).astype(BF16)
    _swiglu_acc(u_scr, wgu_ref, wd_ref, acc_scr, th)
    mod = mod_ref[0]
    out_ref[...] = x_ref[...] + mod[5:6] * (_rms(acc_scr[...]) * g3_ref[...])


def _resident(a, index_map):
    return pl.BlockSpec(a, index_map, pipeline_mode=pl.Buffered(1))


def _ffn(x, seg, mod, g2, wgu, wd, g3, th):
    n, d = x.shape
    tm = seg.tile(1024)
    return pl.pallas_call(
        functools.partial(_ffn_kernel, th=th),
        out_shape=jax.ShapeDtypeStruct((n, d), F32),
        grid=(n // tm,),
        in_specs=[pl.BlockSpec((tm, d), lambda i: (i, 0)),
                  seg.mod_spec(tm, d, 1),
                  _full(g2, 1),
                  _resident(wgu.shape, lambda i: (0, 0)),
                  _resident(wd.shape, lambda i: (0, 0)),
                  _full(g3, 1)],
        out_specs=pl.BlockSpec((tm, d), lambda i: (i, 0)),
        scratch_shapes=[pltpu.VMEM((tm, d), BF16), pltpu.VMEM((tm, d), F32)],
        input_output_aliases={0: 0},
        compiler_params=_cp("arbitrary"),
        name="ffn",
    )(x, mod, g2, wgu, wd, g3)


def _router_kernel(x_ref, mod_ref, g2_ref, wr_ref, br_ref, u_ref, comb_ref):
    u = _normmod(x_ref[...], g2_ref[...], mod_ref[0], 3)
    u_ref[...] = u.astype(BF16)
    lane = lax.broadcasted_iota(jnp.int32, (1, LANES), 1)
    logits = jnp.dot(u, wr_ref[...], preferred_element_type=F32, precision=HI) + br_ref[...]
    logits = jnp.where(lane < N_EXPERTS, logits, NEG)
    m1 = jnp.max(logits, axis=1, keepdims=True)
    i1 = jnp.min(jnp.where(logits == m1, lane, LANES), axis=1, keepdims=True)
    rest = jnp.where(lane == i1, NEG, logits)
    m2 = jnp.max(rest, axis=1, keepdims=True)
    i2 = jnp.min(jnp.where(rest == m2, lane, LANES), axis=1, keepdims=True)
    e2 = jnp.exp(m2 - m1)
    w1 = 1.0 / (1.0 + e2)
    comb_ref[...] = jnp.where(lane == i1, w1, 0.0) + jnp.where(lane == i2, e2 * w1, 0.0)


def _router(x, seg, mod, g2, wr, br):
    n, d = x.shape
    tm = seg.tile(1024)
    consts = (g2, wr, br)
    return pl.pallas_call(
        _router_kernel,
        out_shape=[jax.ShapeDtypeStruct((n, d), BF16), jax.ShapeDtypeStruct((n, LANES), F32)],
        grid=(n // tm,),
        in_specs=[pl.BlockSpec((tm, d), lambda i: (i, 0)), seg.mod_spec(tm, d, 1)]
                 + [_full(a, 1) for a in consts],
        out_specs=[pl.BlockSpec((tm, d), lambda i: (i, 0)), pl.BlockSpec((tm, LANES), lambda i: (i, 0))],
        compiler_params=_cp("arbitrary"),
        name="router",
    )(x, mod, *consts)


def _moe_kernel(u_ref, comb_ref, x_ref, mod_ref, wg_ref, wu_ref, wd_ref, g3_ref, out_ref, acc_scr):
    e = pl.program_id(1)
    kk = pl.program_id(2)

    @pl.when((e == 0) & (kk == 0))
    def _():
        acc_scr[...] = jnp.zeros_like(acc_scr)

    u = u_ref[...]
    gg = jnp.dot(u, wg_ref[0], preferred_element_type=F32)
    up = jnp.dot(u, wu_ref[0], preferred_element_type=F32)
    hh = (_silu(gg) * up).astype(BF16)
    lane = lax.broadcasted_iota(jnp.int32, (1, LANES), 1)
    cw = jnp.sum(jnp.where(lane == e, comb_ref[...], 0.0), axis=1, keepdims=True)
    acc_scr[...] += cw * jnp.dot(hh, wd_ref[0], preferred_element_type=F32)

    @pl.when((e == pl.num_programs(1) - 1) & (kk == pl.num_programs(2) - 1))
    def _():
        mod = mod_ref[0]
        out_ref[...] = x_ref[...] + mod[5:6] * (_rms(acc_scr[...]) * g3_ref[...])


def _moe(u, comb, x, seg, mod, wgu, wd, g3, th):
    n, d = x.shape
    tm = seg.tile(1024)
    ne, hd, _ = wd.shape
    nk = hd // th
    return pl.pallas_call(
        _moe_kernel,
        out_shape=jax.ShapeDtypeStruct((n, d), F32),
        grid=(n // tm, ne, nk),
        in_specs=[pl.BlockSpec((tm, d), lambda i, e, k: (i, 0)),
                  pl.BlockSpec((tm, LANES), lambda i, e, k: (i, 0)),
                  pl.BlockSpec((tm, d), lambda i, e, k: (i, 0)),
                  seg.mod_spec(tm, d, 3),
                  pl.BlockSpec((1, d, th), lambda i, e, k: (e, 0, k)),
                  pl.BlockSpec((1, d, th), lambda i, e, k: (e, 0, nk + k)),
                  pl.BlockSpec((1, th, d), lambda i, e, k: (e, k, 0)),
                  _full(g3, 3)],
        out_specs=pl.BlockSpec((tm, d), lambda i, e, k: (i, 0)),
        scratch_shapes=[pltpu.VMEM((tm, d), F32)],
        input_output_aliases={2: 0},
        compiler_params=_cp("arbitrary", "arbitrary", "arbitrary"),
        name="moe",
    )(u, comb, x, mod, wgu, wgu, wd, g3)


ROW_TILE = 1024
META_E, META_RANK, META_W = 0, 2, 4


def _route_kernel(x_ref, mod_ref, g2_ref, wr_ref, br_ref, meta_ref, cnt_ref, tri_scr, run_scr):
    i = pl.program_id(0)
    tm = x_ref.shape[0]

    @pl.when(i == 0)
    def _():
        r = lax.broadcasted_iota(jnp.int32, (tm, tm), 0)
        c = lax.broadcasted_iota(jnp.int32, (tm, tm), 1)
        tri_scr[...] = jnp.where(c < r, 1.0, 0.0).astype(BF16)
        run_scr[...] = jnp.zeros_like(run_scr)

    u = _normmod(x_ref[...], g2_ref[...], mod_ref[0], 3)
    lane = lax.broadcasted_iota(jnp.int32, (1, LANES), 1)
    logits = jnp.dot(u, wr_ref[...], preferred_element_type=F32, precision=HI) + br_ref[...]
    logits = jnp.where(lane < N_EXPERTS, logits, NEG)
    m1 = jnp.max(logits, axis=1, keepdims=True)
    i1 = jnp.min(jnp.where(logits == m1, lane, LANES), axis=1, keepdims=True)
    rest = jnp.where(lane == i1, NEG, logits)
    m2 = jnp.max(rest, axis=1, keepdims=True)
    i2 = jnp.min(jnp.where(rest == m2, lane, LANES), axis=1, keepdims=True)
    e2 = jnp.exp(m2 - m1)
    w1 = 1.0 / (1.0 + e2)
    sel = ((lane == i1) | (lane == i2))
    onehot = jnp.where(sel, 1.0, 0.0)
    before = jnp.dot(tri_scr[...], onehot.astype(BF16), preferred_element_type=F32) + run_scr[...]
    rank1 = jnp.sum(jnp.where(lane == i1, before, 0.0), axis=1, keepdims=True)
    rank2 = jnp.sum(jnp.where(lane == i2, before, 0.0), axis=1, keepdims=True)
    run_scr[...] = run_scr[...] + jnp.sum(onehot, axis=0, keepdims=True)
    vals = (i1.astype(F32), i2.astype(F32), rank1, rank2, w1, e2 * w1)
    meta = jnp.zeros((tm, LANES), F32)
    for col, val in enumerate(vals):
        meta = jnp.where(lane == col, val, meta)
    meta_ref[...] = meta
    cnt_ref[...] = jnp.broadcast_to(run_scr[...], cnt_ref.shape)


def _route(x, seg, mod, g2, wr, br):
    n, d = x.shape
    tm = seg.tile(1024)
    consts = (g2, wr, br)
    return pl.pallas_call(
        _route_kernel,
        out_shape=[jax.ShapeDtypeStruct((n, LANES), F32),
                   jax.ShapeDtypeStruct((SUBLANES, LANES), F32)],
        grid=(n // tm,),
        in_specs=[pl.BlockSpec((tm, d), lambda i: (i, 0)), seg.mod_spec(tm, d, 1)]
                 + [_full(a, 1) for a in consts],
        out_specs=[pl.BlockSpec((tm, LANES), lambda i: (i, 0)),
                   pl.BlockSpec((SUBLANES, LANES), lambda i: (0, 0))],
        scratch_shapes=[pltpu.VMEM((tm, tm), BF16), pltpu.VMEM((1, LANES), F32)],
        compiler_params=_cp("arbitrary"),
        name="route",
    )(x, mod, *consts)


def _drain(src_ref, dst_ref, sem, count):
    def body(i, carry):
        pltpu.make_async_copy(src_ref, dst_ref, sem).wait()
        return carry

    lax.fori_loop(0, count, body, 0, unroll=8)


def _row_tile(buf, t):
    return buf.at[pl.ds(pl.multiple_of(t * SUBLANES, SUBLANES), SUBLANES), :]


def _dispatch_kernel(dst_ref, pad_ref, x_ref, mod_ref, g2_ref, xs_ref, ubuf, sem):
    j = pl.program_id(0)
    tm = x_ref.shape[0]
    u = _normmod(x_ref[...], g2_ref[...], mod_ref[0], 3)
    for s in range(SUBLANES):
        ubuf[pl.ds(s, tm, stride=SUBLANES), :] = u[:, s * LANES:(s + 1) * LANES]

    def issue(t, carry):
        pltpu.make_async_copy(_row_tile(ubuf, t), xs_ref.at[dst_ref[0, 0, t]], sem).start(priority=0)
        pltpu.make_async_copy(_row_tile(ubuf, t), xs_ref.at[dst_ref[0, 0, tm + t]], sem).start(priority=1)
        return carry

    lax.fori_loop(0, tm, issue, 0, unroll=8)
    npad = pad_ref.shape[2]
    first = _row_tile(ubuf, 0)

    @pl.when(j == 0)
    def _():
        def issue_pad(p, carry):
            pltpu.make_async_copy(first, xs_ref.at[pad_ref[0, 0, p]], sem).start()
            return carry

        lax.fori_loop(0, npad, issue_pad, 0, unroll=8)
        _drain(first, xs_ref.at[0], sem, npad)

    _drain(first, xs_ref.at[0], sem, 2 * tm)


def _dispatch(x, seg, mod, g2, dest, pad_dst, n_rows):
    n, d = x.shape
    tm = seg.tile(512)
    steps = n // tm
    dst = dest.reshape(steps, tm, 2).transpose(0, 2, 1).reshape(steps, 1, 2 * tm)
    npad = pad_dst.shape[0]
    return pl.pallas_call(
        _dispatch_kernel,
        out_shape=jax.ShapeDtypeStruct((n_rows, SUBLANES, d // SUBLANES), F32),
        grid=(steps,),
        in_specs=[pl.BlockSpec((1, 1, 2 * tm), lambda j: (j, 0, 0), memory_space=pltpu.SMEM),
                  pl.BlockSpec((1, 1, npad), lambda j: (0, 0, 0), memory_space=pltpu.SMEM),
                  pl.BlockSpec((tm, d), lambda j: (j, 0)), seg.mod_spec(tm, d, 1), _full(g2, 1)],
        out_specs=pl.BlockSpec(memory_space=pl.ANY),
        scratch_shapes=[pltpu.VMEM((tm * SUBLANES, d // SUBLANES), F32), pltpu.SemaphoreType.DMA(())],
        compiler_params=_cp("arbitrary"),
        name="dispatch",
    )(dst, pad_dst.reshape(1, 1, npad), x, mod, g2)


def _experts_kernel(te_ref, used_ref, xs_ref, wgu_ref, wd_ref, y_ref, u_scr, acc_scr, *, th):
    i = pl.program_id(0)

    @pl.when(i < used_ref[0])
    def _():
        tr = u_scr.shape[0]
        for s in range(SUBLANES):
            u_scr[:, s * LANES:(s + 1) * LANES] = xs_ref[pl.ds(s, tr, stride=SUBLANES), :].astype(BF16)
        _swiglu_acc(u_scr, wgu_ref.at[0], wd_ref.at[0], acc_scr, th)
        y = acc_scr[...]
        for s in range(SUBLANES):
            y_ref[pl.ds(s, tr, stride=SUBLANES), :] = y[:, s * LANES:(s + 1) * LANES]


def _experts(xs3, tile_expert, used, wgu, wd, th):
    tr = ROW_TILE
    n_rows = xs3.shape[0]
    ne, hd, d = wd.shape
    tiles = tile_expert.shape[0]
    assert tiles * tr <= n_rows
    row = lambda i, te, us: (jnp.minimum(i, us[0] - 1), 0)
    grid_spec = pltpu.PrefetchScalarGridSpec(
        num_scalar_prefetch=2,
        grid=(tiles,),
        in_specs=[pl.BlockSpec((tr * SUBLANES, LANES), row),
                  _resident((1,) + wgu.shape[1:], lambda i, te, us: (te[i], 0, 0)),
                  _resident((1,) + wd.shape[1:], lambda i, te, us: (te[i], 0, 0))],
        out_specs=pl.BlockSpec((tr * SUBLANES, LANES), row),
        scratch_shapes=[pltpu.VMEM((tr, d), BF16), pltpu.VMEM((tr, d), F32)],
    )
    y = pl.pallas_call(
        functools.partial(_experts_kernel, th=th),
        out_shape=jax.ShapeDtypeStruct((tiles * tr * SUBLANES, LANES), F32),
        grid_spec=grid_spec,
        compiler_params=_cp("arbitrary"),
        name="experts",
    )(tile_expert, used, xs3.reshape(n_rows * SUBLANES, LANES), wgu, wd)
    return y.reshape(tiles * tr, SUBLANES, LANES)


def _combine_kernel(src_ref, y_ref, meta_ref, x_ref, mod_ref, g3_ref, out_ref, ybuf, sem):
    tm = x_ref.shape[0]

    def issue(t, carry):
        pltpu.make_async_copy(y_ref.at[src_ref[0, 0, t]], _row_tile(ybuf.at[0], t), sem).start(priority=0)
        pltpu.make_async_copy(y_ref.at[src_ref[0, 0, tm + t]], _row_tile(ybuf.at[1], t), sem).start(priority=1)
        return carry

    lax.fori_loop(0, tm, issue, 0, unroll=8)
    _drain(y_ref.at[0], _row_tile(ybuf.at[0], 0), sem, 2 * tm)
    meta = meta_ref[...]
    w1 = meta[:, META_W:META_W + 1]
    w2 = meta[:, META_W + 1:META_W + 2]
    ys = [w1 * ybuf[0, pl.ds(s, tm, stride=SUBLANES), :] + w2 * ybuf[1, pl.ds(s, tm, stride=SUBLANES), :]
          for s in range(SUBLANES)]
    ss = ys[0] * ys[0]
    for y in ys[1:]:
        ss = ss + y * y
    d = x_ref.shape[1]
    r = lax.rsqrt(jnp.sum(ss, axis=1, keepdims=True) / d + EPS)
    mod = mod_ref[0]
    for s, y in enumerate(ys):
        sl = slice(s * LANES, (s + 1) * LANES)
        out_ref[:, sl] = x_ref[:, sl] + mod[5:6, sl] * (y * r * g3_ref[:, sl])


def _combine(y3, dest, meta, x, seg, mod, g3):
    n, d = x.shape
    tm = seg.tile(512)
    steps = n // tm
    src = dest.reshape(steps, tm, 2).transpose(0, 2, 1).reshape(steps, 1, 2 * tm)
    return pl.pallas_call(
        _combine_kernel,
        out_shape=jax.ShapeDtypeStruct((n, d), F32),
        grid=(steps,),
        in_specs=[pl.BlockSpec((1, 1, 2 * tm), lambda j: (j, 0, 0), memory_space=pltpu.SMEM),
                  pl.BlockSpec(memory_space=pl.ANY),
                  pl.BlockSpec((tm, LANES), lambda j: (j, 0)),
                  pl.BlockSpec((tm, d), lambda j: (j, 0)),
                  seg.mod_spec(tm, d, 1), _full(g3, 1)],
        out_specs=pl.BlockSpec((tm, d), lambda j: (j, 0)),
        scratch_shapes=[pltpu.VMEM((2, tm * SUBLANES, d // SUBLANES), F32), pltpu.SemaphoreType.DMA(())],
        input_output_aliases={3: 0},
        compiler_params=_cp("arbitrary"),
        name="combine",
    )(src, y3, meta, x, mod, g3)


def _moe_routed(x, seg, mod, g2, wr, br, wgu, wd, g3, th):
    n, d = x.shape
    assert d == SUBLANES * LANES
    tr = ROW_TILE
    meta, cnt = _route(x, seg, mod, g2, wr, br)
    counts = cnt[0, :N_EXPERTS].astype(jnp.int32)
    sizes = ((counts + tr - 1) // tr) * tr
    ends = jnp.cumsum(sizes)
    offs = ends - sizes
    mi = meta[:, :META_W].astype(jnp.int32)
    eidx = mi[:, META_E:META_E + 2]
    dest = jnp.sum(jnp.where(eidx[:, :, None] == jnp.arange(N_EXPERTS)[None, None, :],
                             offs[None, None, :], 0), axis=2) + mi[:, META_RANK:META_RANK + 2]
    dest = dest.astype(jnp.int32)
    tiles = (2 * n) // tr + N_EXPERTS
    n_rows = tiles * tr
    used = (ends[-1] // tr).astype(jnp.int32).reshape(1)
    tile_start = jnp.arange(tiles, dtype=jnp.int32) * tr
    tile_expert = jnp.sum(tile_start[:, None] >= ends[None, :], axis=1).astype(jnp.int32)
    last_expert = jnp.sum(jnp.maximum(ends[-1] - tr, 0) >= ends).astype(jnp.int32)
    tile_expert = jnp.where(tile_start < ends[-1], tile_expert, last_expert)
    pad = jnp.arange(N_EXPERTS * tr, dtype=jnp.int32).reshape(N_EXPERTS, tr)
    pad_row = offs[:, None] + counts[:, None] + jnp.arange(tr, dtype=jnp.int32)[None, :]
    pad_dst = jnp.where(pad_row < ends[:, None], pad_row, n_rows + pad).reshape(-1).astype(jnp.int32)
    xs3 = _dispatch(x, seg, mod, g2, dest, pad_dst, n_rows + N_EXPERTS * tr)
    y3 = _experts(xs3, tile_expert, used, wgu, wd, th)
    return _combine(y3, dest, meta, x, seg, mod, g3)


def _proj_h_body(u, w_refs, out_refs):
    for w_ref, o_ref in zip(w_refs, out_refs):
        o_ref[...] = jnp.dot(u, w_ref[...], preferred_element_type=F32).astype(BF16)


def _proj_h_lat_kernel(x_ref, mod_ref, g_ref, *refs, rows_per_col):
    n = len(refs) // 2
    w_refs, out_refs = refs[:n], refs[n:]
    x = jnp.swapaxes(x_ref[...], 0, 1)
    x = x.reshape(SUBLANES * rows_per_col, x.shape[2])
    u = _normmod(x, g_ref[...], mod_ref[0], 0).astype(BF16)
    _proj_h_body(u, w_refs, out_refs)


def _proj_h_ctx_kernel(x_ref, mod_ref, g_ref, *refs):
    n = len(refs) // 2
    u = _normmod(x_ref[...], g_ref[...], mod_ref[0], 0).astype(BF16)
    _proj_h_body(u, refs[:n], refs[n:])


def _proj_h_lat(x, seg, mod, g, ws):
    n, d = x.shape
    grid_rows = seg.seqlen // GRID_W
    x3 = x.reshape(n // GRID_W, GRID_W, d)
    tcol = SUBLANES * grid_rows
    ncb = GRID_W // SUBLANES
    return pl.pallas_call(
        functools.partial(_proj_h_lat_kernel, rows_per_col=grid_rows),
        out_shape=[jax.ShapeDtypeStruct((n, w.shape[1]), BF16) for w in ws],
        grid=(seg.batch, ncb),
        in_specs=[pl.BlockSpec((grid_rows, SUBLANES, d), lambda b, c: (b, c, 0)),
                  pl.BlockSpec((1, N_MOD, d), lambda b, c: (b, 0, 0)), _full(g, 2)]
                 + [_full(w, 2) for w in ws],
        out_specs=[pl.BlockSpec((tcol, w.shape[1]), lambda b, c: (b * ncb + c, 0)) for w in ws],
        compiler_params=_cp("arbitrary", "arbitrary"),
        name="hgrn_proj_lat",
    )(x3, mod, g, *ws)


def _proj_h_ctx(x, seg, mod, g, ws):
    n, d = x.shape
    tm = seg.tile(512)
    return pl.pallas_call(
        _proj_h_ctx_kernel,
        out_shape=[jax.ShapeDtypeStruct((n, w.shape[1]), BF16) for w in ws],
        grid=(n // tm,),
        in_specs=[pl.BlockSpec((tm, d), lambda j: (j, 0)), seg.mod_spec(tm, d, 1), _full(g, 1)]
                 + [_full(w, 1) for w in ws],
        out_specs=[pl.BlockSpec((tm, w.shape[1]), lambda j: (j, 0)) for w in ws],
        compiler_params=_cp("arbitrary"),
        name="hgrn_proj_ctx",
    )(x, mod, g, *ws)


HGRN_HEAD_GROUP = 4


def _hgrn_prologue(f_ref, bias_ref, lb_ref, reverse, d):
    lb = lb_ref[d:d + 1, :]
    f = lb + (1.0 - lb) * _sigmoid(f_ref[...].astype(F32) + bias_ref[d:d + 1, :])
    return f, _cumsum_time(jnp.log(f), _cumsum_mat(CHUNK, reverse))


def _hgrn_heads(q_ref, v_ref, pro, h_ref, s_scr, reverse, d, heads):
    L = CHUNK
    f, a_all = pro
    r = lax.broadcasted_iota(jnp.int32, (L, L), 0)
    c = lax.broadcasted_iota(jnp.int32, (L, L), 1)
    causal = (c >= r) if reverse else (c <= r)
    same64 = (r // 64) == (c // 64)
    diag32 = ((r // 32) == (c // 32)) & causal
    t = lax.broadcasted_iota(jnp.int32, (L, 1), 0)
    if reverse:
        late0, late1 = t < 64, (t % 64) < 32
        a0, a1 = (64,), (32, 96)
        a2 = (31, 63, 95, 127)
        last = 0
    else:
        late0, late1 = t >= 64, (t % 64) >= 32
        a0, a1 = (63,), (31, 95)
        a2 = (0, 32, 64, 96)
        last = L - 1
    staged = []
    for h in heads:
        sl = slice(h * H_DK, (h + 1) * H_DK)
        a = a_all[:, sl]
        q = q_ref[:, sl].astype(F32)
        k = 1.0 - f[:, sl]
        anc0 = jnp.broadcast_to(a[a0[0]:a0[0] + 1, :], (L, H_DK))
        anc1 = jnp.where(t < 64, a[a1[0]:a1[0] + 1, :], a[a1[1]:a1[1] + 1, :])
        anc2 = jnp.where(t < 32, a[a2[0]:a2[0] + 1, :],
                         jnp.where(t < 64, a[a2[1]:a2[1] + 1, :],
                                   jnp.where(t < 96, a[a2[2]:a2[2] + 1, :], a[a2[3]:a2[3] + 1, :])))

        def level(anc, late):
            qs = q * jnp.exp(a - anc)
            ks = k * jnp.exp(anc - a)
            if late is not None:
                qs = jnp.where(late, qs, 0.0)
                ks = jnp.where(late, 0.0, ks)
            return lax.dot_general(qs.astype(BF16), ks.astype(BF16), NT_DIMS, preferred_element_type=F32)

        s = jnp.where(diag32, level(anc2, None), jnp.where(same64, level(anc1, late1), level(anc0, late0)))
        qd = (q * jnp.exp(a)).astype(BF16)
        a_last = a[last:last + 1, :]
        kd = (k * jnp.exp(a_last - a)).astype(BF16)
        staged.append((s.astype(BF16), qd, kd, jnp.exp(a_last)))
    for h, (s, qd, kd, dec) in zip(heads, staged):
        v = v_ref[:, h * H_DV:(h + 1) * H_DV]
        st = s_scr[d, h]
        o = (jnp.dot(s, v, preferred_element_type=F32)
             + lax.dot_general(qd, st.astype(BF16), NT_DIMS, preferred_element_type=F32))
        h_ref[:, h * H_DV:(h + 1) * H_DV] = o.astype(BF16)
        s_scr[d, h] = st * dec + lax.dot_general(v, kd, TN_DIMS, preferred_element_type=F32)


def _hgrn_scan_kernel(qf, vf, ff, qb, vb, fb, bias_ref, lb_ref, s0_ref, hf_ref, hb_ref, st_ref, s_scr):
    n = pl.program_id(1)

    @pl.when(n == 0)
    def _():
        s_scr[...] = s0_ref[0]

    pro_f = _hgrn_prologue(ff, bias_ref, lb_ref, False, 0)
    pro_b = _hgrn_prologue(fb, bias_ref, lb_ref, True, 1)
    for h0 in range(0, H_HEADS, HGRN_HEAD_GROUP):
        heads = range(h0, h0 + HGRN_HEAD_GROUP)
        _hgrn_heads(qf, vf, pro_f, hf_ref, s_scr, False, 0, heads)
        _hgrn_heads(qb, vb, pro_b, hb_ref, s_scr, True, 1, heads)

    @pl.when(n == pl.num_programs(1) - 1)
    def _():
        st_ref[0] = s_scr[...]


def _hgrn_scan(q, v, f_f, f_b, bias, lb, s0, seg):
    L = CHUNK
    assert seg.seqlen % L == 0
    n = seg.seqlen // L
    fwd = lambda b, i: (b * n + i, 0)
    bwd = lambda b, i: (b * n + (n - 1 - i), 0)
    rows_, w = q.shape
    st = lambda b, i: (b, 0, 0, 0, 0)
    return pl.pallas_call(
        _hgrn_scan_kernel,
        out_shape=[jax.ShapeDtypeStruct((rows_, w), BF16), jax.ShapeDtypeStruct((rows_, w), BF16),
                   jax.ShapeDtypeStruct(s0.shape, F32)],
        grid=(seg.batch, n),
        in_specs=[pl.BlockSpec((L, w), fwd), pl.BlockSpec((L, w), fwd), pl.BlockSpec((L, w), fwd),
                  pl.BlockSpec((L, w), bwd), pl.BlockSpec((L, w), bwd), pl.BlockSpec((L, w), bwd),
                  _full(bias, 2), _full(lb, 2), pl.BlockSpec((1,) + s0.shape[1:], st)],
        out_specs=[pl.BlockSpec((L, w), fwd), pl.BlockSpec((L, w), bwd),
                   pl.BlockSpec((1,) + s0.shape[1:], st)],
        scratch_shapes=[pltpu.VMEM(s0.shape[1:], F32)],
        compiler_params=_cp("arbitrary", "arbitrary"),
        name="hgrn_scan",
    )(q, v, f_f, q, v, f_b, bias, lb, s0)


def _mix_out_col_kernel(hf_ref, hb_ref, o_ref, x_ref, mod_ref, gh_ref, w_ref, g1_ref, out_ref,
                        *, n_heads, rows_per_col):
    mod = mod_ref[0]
    y = _head_out(hf_ref[...], hb_ref[...], o_ref[...], gh_ref[...], w_ref[...], n_heads)
    upd = mod[2:3] * (_rms(y) * g1_ref[...])
    upd = jnp.swapaxes(upd.reshape(SUBLANES, rows_per_col, upd.shape[1]), 0, 1)
    out_ref[...] = x_ref[...] + upd


def _mix_out_col(hf, hb, o, x, seg, mod, gh, w, g1, n_heads):
    n, d = x.shape
    grid_rows = seg.seqlen // GRID_W
    tcol = SUBLANES * grid_rows
    ncb = GRID_W // SUBLANES
    x3 = x.reshape(n // GRID_W, GRID_W, d)
    hmap = lambda b, c: (b * ncb + c, 0)
    xmap = lambda b, c: (b, c, 0)
    consts = (gh, w, g1)
    out = pl.pallas_call(
        functools.partial(_mix_out_col_kernel, n_heads=n_heads, rows_per_col=grid_rows),
        out_shape=jax.ShapeDtypeStruct(x3.shape, F32),
        grid=(seg.batch, ncb),
        in_specs=[pl.BlockSpec((tcol, hf.shape[1]), hmap), pl.BlockSpec((tcol, hf.shape[1]), hmap),
                  pl.BlockSpec((tcol, o.shape[1]), hmap),
                  pl.BlockSpec((grid_rows, SUBLANES, d), xmap),
                  pl.BlockSpec((1, N_MOD, d), lambda b, c: (b, 0, 0))] + [_full(a, 2) for a in consts],
        out_specs=pl.BlockSpec((grid_rows, SUBLANES, d), xmap),
        input_output_aliases={3: 0},
        compiler_params=_cp("arbitrary", "arbitrary"),
        name="mix_out_col",
    )(hf, hb, o, x3, mod, *consts)
    return out.reshape(n, d)


def kernel(x, c, ctx, c_ctx, mod_w, mod_b, norm_g, m_w_in, m_b_gate, m_w_conv, m_g_head, m_w_out,
           h_w_in, h_b_f, h_lb_raw, h_g_head, h_w_out, f_w_gu, f_w_down,
           e_w_router, e_b_router, e_w_gu, e_w_down):
    batch, seq, d = x.shape
    ctx_len = ctx.shape[1]
    depth = mod_w.shape[0]
    assert seq % GRID_W == 0 and seq // GRID_W == CHUNK and ctx_len % CHUNK == 0
    assert batch < MOD_ROWS and d % LANES == 0
    lat_seg = _Seg(batch, seq, False)
    ctx_seg = _Seg(batch, ctx_len, True)

    c_all = jnp.zeros((MOD_ROWS, d), F32).at[:batch].set(c).at[batch].set(c_ctx)
    mod_all = _modulation(c_all, mod_w, mod_b).reshape(depth, MOD_ROWS, N_MOD, d)

    lb_all = jax.nn.softmax(h_lb_raw.astype(F32), axis=0)
    lb_all = jnp.cumsum(lb_all, axis=0) - lb_all[0]

    xl = x.reshape(batch * seq, d)
    xc = ctx.reshape(batch * ctx_len, d)
    row = lambda v: v.reshape(1, -1).astype(F32)

    for i in range(depth):
        j = i // 2
        with_ctx = i < depth - 1
        mod = mod_all[i]
        g0, g1, g2, g3 = (row(norm_g[i, s]) for s in range(4))
        if i % 2 == 0:
            qk_w = 2 * M_HEADS * M_DK
            v_w = M_HEADS * M_DV
            w_in = m_w_in[j]
            wqk = w_in[:, :qk_w].astype(BF16)
            wv = w_in[:, qk_w:qk_w + v_w].astype(BF16)
            wo = w_in[:, qk_w + v_w:qk_w + 2 * v_w].astype(BF16)
            ng = 4 * M_HEADS
            wg = jnp.zeros((d, LANES), F32).at[:, :ng].set(w_in[:, qk_w + 2 * v_w:]).astype(BF16)
            gb = jnp.zeros((1, LANES), F32).at[0, :ng].set(m_b_gate[j].reshape(-1))
            col = jnp.arange(LANES)
            gm = (((col // M_HEADS) % 2 == 1) & (col < ng)).astype(F32).reshape(1, LANES)
            pw = (mod, g0, wqk, wv, wo, wg, m_w_conv[j].astype(F32), gb, gm)
            qc, kc, vc, oc, gc = _proj_m(xc, ctx_seg, *pw)
            ql, kl, vl, ol, gl = _proj_m(xl, lat_seg, *pw)
            s0 = jnp.zeros((batch, 2, M_HEADS // 2, LANES, 2 * M_DV), F32)
            m0 = jnp.zeros((batch, 2 * M_HEADS, LANES), F32)
            hcf, hcb, s1, m1 = _mlstm_scan(qc, kc, vc, gc, s0, m0, ctx_seg)
            hlf, hlb, _, _ = _mlstm_scan(ql, kl, vl, gl, s1, m1, lat_seg)
            gh = row(m_g_head[j])
            wout = m_w_out[j].astype(BF16)
            xl = _mix_out(hlf, hlb, ol, xl, lat_seg, mod, gh, wout, g1, M_HEADS, in_place=i > 0)
            if with_ctx:
                xc = _mix_out(hcf, hcb, oc, xc, ctx_seg, mod, gh, wout, g1, M_HEADS, in_place=i > 0)
        else:
            kw = H_HEADS * H_DK
            w_in = h_w_in[j].astype(BF16)
            ws = [w_in[:, s * kw:(s + 1) * kw] for s in range(5)]
            qc, vc, ffc, fbc, gc = _proj_h_ctx(xc, ctx_seg, mod, g0, ws)
            ql, vl, ffl, fbl, gl = _proj_h_lat(xl, lat_seg, mod, g0, ws)
            bias = h_b_f[j].astype(F32)
            lb = lb_all[j]
            s0 = jnp.zeros((batch, 2, H_HEADS, H_DV, H_DK), F32)
            hcf, hcb, s1 = _hgrn_scan(qc, vc, ffc, fbc, bias, lb, s0, ctx_seg)
            hlf, hlb, _ = _hgrn_scan(ql, vl, ffl, fbl, bias, lb, s1, lat_seg)
            gh = row(h_g_head[j])
            wout = h_w_out[j].astype(BF16)
            xl = _mix_out_col(hlf, hlb, gl, xl, lat_seg, mod, gh, wout, g1, H_HEADS)
            if with_ctx:
                xc = _mix_out(hcf, hcb, gc, xc, ctx_seg, mod, gh, wout, g1, H_HEADS)
        streams = [(xl, lat_seg)] + ([(xc, ctx_seg)] if with_ctx else [])
        outs = []
        if i % 2 == 0:
            wgu, wd = f_w_gu[j].astype(BF16), f_w_down[j].astype(BF16)
            for xs, seg in streams:
                outs.append(_ffn(xs, seg, mod, g2, wgu, wd, g3, th=256))
        else:
            wr = jnp.zeros((d, LANES), F32).at[:, :N_EXPERTS].set(e_w_router[j])
            br = jnp.zeros((1, LANES), F32).at[0, :N_EXPERTS].set(e_b_router[j])
            wgu, wd = e_w_gu[j].astype(BF16), e_w_down[j].astype(BF16)
            outs.append(_moe_routed(xl, lat_seg, mod, g2, wr, br, wgu, wd, g3, th=256))
            if with_ctx:
                u, comb = _router(xc, ctx_seg, mod, g2, wr, br)
                outs.append(_moe(u, comb, xc, ctx_seg, mod, wgu, wd, g3, th=256))
        xl = outs[0]
        if with_ctx:
            xc = outs[1]
    return xl.reshape(batch, seq, d)
```

```python
import functools

import jax
import jax.numpy as jnp
from jax import lax
from jax.experimental import pallas as pl
from jax.experimental.pallas import tpu as pltpu

F32 = jnp.float32
BF16 = jnp.bfloat16
EPS = 1e-6
NEG = -1e30

GRID_W = 64
N_MOD = 6
M_HEADS, M_DK, M_DV = 8, 64, 128
H_HEADS, H_DK, H_DV = 8, 128, 128
N_EXPERTS = 8
LANES = 128
SUBLANES = 8
CHUNK = 128
MOD_ROWS = 16
VMEM_LIMIT = 56 * 1024 * 1024

HI = lax.Precision.HIGHEST
NT_DIMS = (((1,), (1,)), ((), ()))
TN_DIMS = (((0,), (0,)), ((), ()))


def _cp(*sem):
    return pltpu.CompilerParams(dimension_semantics=sem, vmem_limit_bytes=VMEM_LIMIT)


def _full(a, nargs):
    zeros = (0,) * a.ndim
    return pl.BlockSpec(a.shape, lambda *_: zeros)


def _sigmoid(x):
    return 1.0 / (1.0 + jnp.exp(-x))


def _silu(x):
    return x * _sigmoid(x)


def _log_sigmoid(x):
    return jnp.minimum(x, 0.0) - jnp.log(1.0 + jnp.exp(-jnp.abs(x)))


def _rms(x):
    return x * lax.rsqrt(jnp.mean(x * x, axis=-1, keepdims=True) + EPS)


def _normmod(x, g, mod, s):
    return _rms(x) * (g * (1.0 + mod[s + 1:s + 2])) + mod[s:s + 1]


def _split2(x):
    hi = x.astype(BF16)
    lo = (x - hi.astype(F32)).astype(BF16)
    return hi, lo


def _cumsum_mat(n, reverse):
    r = lax.broadcasted_iota(jnp.int32, (n, n), 0)
    c = lax.broadcasted_iota(jnp.int32, (n, n), 1)
    return jnp.where((c >= r) if reverse else (c <= r), 1.0, 0.0).astype(BF16)


def _cumsum_time(x, tri):
    hi, lo = _split2(x)
    return (jnp.dot(tri, hi, preferred_element_type=F32)
            + jnp.dot(tri, lo, preferred_element_type=F32))


class _Seg:
    def __init__(self, batch, seqlen, is_ctx):
        self.batch, self.seqlen, self.is_ctx = batch, seqlen, is_ctx
        self.n = batch * seqlen

    def tile(self, pref):
        tm = min(pref, self.n if self.is_ctx else self.seqlen)
        assert self.n % tm == 0 and (self.seqlen % tm == 0 or tm % self.seqlen == 0)
        return tm

    def mod_spec(self, tm, d, nargs):
        if self.is_ctx:
            row = lambda j: self.batch
        else:
            row = lambda j: (j * tm) // self.seqlen
        if nargs == 1:
            return pl.BlockSpec((1, N_MOD, d), lambda j: (row(j), 0, 0))
        if nargs == 2:
            return pl.BlockSpec((1, N_MOD, d), lambda j, k: (row(j), 0, 0))
        return pl.BlockSpec((1, N_MOD, d), lambda j, e, k: (row(j), 0, 0))


def _mod_kernel(c_ref, w_ref, b_ref, o_ref):
    c = c_ref[...]
    o_ref[0] = jnp.dot(_silu(c), w_ref[0], preferred_element_type=F32, precision=HI) + b_ref[0]


def _modulation(c_all, mod_w, mod_b):
    depth, d, n = mod_w.shape
    tn = n // 4
    return pl.pallas_call(
        _mod_kernel,
        out_shape=jax.ShapeDtypeStruct((depth, MOD_ROWS, n), F32),
        grid=(depth, n // tn),
        in_specs=[pl.BlockSpec((MOD_ROWS, d), lambda i, j: (0, 0)),
                  pl.BlockSpec((1, d, tn), lambda i, j: (i, 0, j)),
                  pl.BlockSpec((1, 1, tn), lambda i, j: (i, 0, j))],
        out_specs=pl.BlockSpec((1, MOD_ROWS, tn), lambda i, j: (i, 0, j)),
        compiler_params=_cp("arbitrary", "arbitrary"),
        name="modulation",
    )(c_all, mod_w, mod_b.reshape(depth, 1, n))


def _proj_m_kernel(xp_ref, x_ref, xn_ref, mod_ref, g_ref, wqk_ref, wv_ref, wo_ref, wg_ref,
                   cw_ref, gb_ref, gm_ref, q_ref, kt_ref, v_ref, o_ref, gt_ref, *, tm, seqlen):
    j = pl.program_id(0)
    mod = mod_ref[0]
    g = g_ref[...]
    u = _normmod(x_ref[...], g, mod, 0).astype(BF16)
    uh = _normmod(jnp.concatenate([xp_ref[...], xn_ref[...]], axis=0), g, mod, 0).astype(BF16)
    wqk = wqk_ref[...]
    z = jnp.dot(u, wqk, preferred_element_type=F32)
    zh = jnp.dot(uh, wqk, preferred_element_type=F32)
    local = lax.broadcasted_iota(jnp.int32, (tm, 1), 0)
    pos = lax.rem(j * tm + local, seqlen)
    zp = jnp.where(local == 0, zh[SUBLANES - 1:SUBLANES], pltpu.roll(z, 1, 0))
    zn = jnp.where(local == tm - 1, zh[SUBLANES:SUBLANES + 1], pltpu.roll(z, tm - 1, 0))
    zp = jnp.where(pos == 0, 0.0, zp)
    zn = jnp.where(pos == seqlen - 1, 0.0, zn)
    cw = cw_ref[...]
    a = _silu(cw[0:1] * zp + cw[1:2] * z + cw[2:3] * zn)
    half = a.shape[1] // 2
    q_ref[...] = a[:, :half].astype(BF16)
    kt_ref[...] = (a[:, half:] * (M_DK ** -0.5)).T.astype(BF16)
    v_ref[...] = jnp.dot(u, wv_ref[...], preferred_element_type=F32).astype(BF16)
    o_ref[...] = jnp.dot(u, wo_ref[...], preferred_element_type=F32).astype(BF16)
    zg = jnp.dot(u, wg_ref[...], preferred_element_type=F32) + gb_ref[...]
    gt_ref[...] = jnp.where(gm_ref[...] > 0.5, _log_sigmoid(zg), zg)


def _proj_m(x, seg, mod, g, wqk, wv, wo, wg, cw, gb, gm):
    n, d = x.shape
    tm = seg.tile(512)
    nb8 = n // SUBLANES
    r8 = tm // SUBLANES
    kern = functools.partial(_proj_m_kernel, tm=tm, seqlen=seg.seqlen)
    nqk = wqk.shape[1]
    consts = (g, wqk, wv, wo, wg, cw, gb, gm)
    return pl.pallas_call(
        kern,
        out_shape=[jax.ShapeDtypeStruct((n, nqk // 2), BF16),
                   jax.ShapeDtypeStruct((nqk // 2, n), BF16),
                   jax.ShapeDtypeStruct((n, wv.shape[1]), BF16),
                   jax.ShapeDtypeStruct((n, wo.shape[1]), BF16),
                   jax.ShapeDtypeStruct((n, LANES), F32)],
        grid=(n // tm,),
        in_specs=[pl.BlockSpec((SUBLANES, d), lambda j: (jnp.maximum(j * r8 - 1, 0), 0)),
                  pl.BlockSpec((tm, d), lambda j: (j, 0)),
                  pl.BlockSpec((SUBLANES, d), lambda j: (jnp.minimum((j + 1) * r8, nb8 - 1), 0)),
                  seg.mod_spec(tm, d, 1)] + [_full(a, 1) for a in consts],
        out_specs=[pl.BlockSpec((tm, nqk // 2), lambda j: (j, 0)),
                   pl.BlockSpec((nqk // 2, tm), lambda j: (0, j)),
                   pl.BlockSpec((tm, wv.shape[1]), lambda j: (j, 0)),
                   pl.BlockSpec((tm, wo.shape[1]), lambda j: (j, 0)),
                   pl.BlockSpec((tm, LANES), lambda j: (j, 0))],
        compiler_params=_cp("arbitrary"),
        name="mlstm_proj",
    )(x, x, x, mod, *consts)


MLSTM_PAIR_GROUP = 2


def _mlstm_gates(g_ref, m_prev, reverse, d):
    L = CHUNK
    g = g_ref[...]
    b = _cumsum_time(g, _cumsum_mat(L, reverse))
    gT, bT = g.T, b.T
    c_ig = 2 * M_HEADS * d
    b_rows = bT[c_ig + M_HEADS:c_ig + 2 * M_HEADS, :]
    beta = gT[c_ig:c_ig + M_HEADS, :] - b_rows
    lane = lax.broadcasted_iota(jnp.int32, (1, L), 1)
    cm = beta
    k = 1
    while k < L:
        if reverse:
            cm = jnp.maximum(cm, jnp.where(lane < L - k, pltpu.roll(cm, L - k, 1), NEG))
        else:
            cm = jnp.maximum(cm, jnp.where(lane >= k, pltpu.roll(cm, k, 1), NEG))
        k *= 2
    last = 0 if reverse else L - 1
    mu = jnp.maximum(cm, m_prev)
    mu_last = mu[:, last:last + 1]
    rows = jnp.concatenate([beta, jnp.exp(beta - mu_last)], axis=0)
    cols = jnp.concatenate([mu, jnp.exp(m_prev - mu), jnp.exp(-(b_rows + mu)),
                            jnp.zeros((LANES - 3 * M_HEADS, L), F32)], axis=0).T
    half = lax.broadcasted_iota(jnp.int32, (1, LANES), 1) < LANES // 2
    small = jnp.where(half, jnp.exp(m_prev - mu_last), b_rows[:, last:last + 1] + mu_last)
    return rows, cols, small


def _mlstm_heads(q_ref, kt_ref, v_ref, rows_g, cols, small, h_ref, s_scr, reverse, d, pairs):
    L = CHUNK
    r = lax.broadcasted_iota(jnp.int32, (L, L), 0)
    c = lax.broadcasted_iota(jnp.int32, (L, L), 1)
    causal = (c >= r) if reverse else (c <= r)
    lane = lax.broadcasted_iota(jnp.int32, (1, LANES), 1)
    ones = jnp.ones((L, M_DV), BF16)
    staged = []
    for p in pairs:
        q_p = q_ref[:, p * LANES:(p + 1) * LANES]
        kt_p = kt_ref[p * LANES:(p + 1) * LANES, :]
        for e in range(2):
            h = 2 * p + e
            q_e = jnp.where((lane // M_DK) == e, q_p, jnp.zeros_like(q_p))
            qk = jnp.dot(q_e, kt_p, preferred_element_type=F32)
            s = jnp.where(causal, qk * jnp.exp(rows_g[h:h + 1, :] - cols[:, h:h + 1]), 0.0).astype(BF16)
            ktw = (kt_p[e * M_DK:(e + 1) * M_DK, :].astype(F32)
                   * rows_g[M_HEADS + h:M_HEADS + h + 1, :]).astype(BF16)
            staged.append((p, e, q_e, s, ktw))
    for p, e, q_e, s, ktw in staged:
        h = 2 * p + e
        rows = slice(e * M_DK, (e + 1) * M_DK)
        vp = jnp.concatenate([v_ref[:, h * M_DV:(h + 1) * M_DV], ones], axis=1)
        w_inter = cols[:, M_HEADS + h:M_HEADS + h + 1]
        floor = cols[:, 2 * M_HEADS + h:2 * M_HEADS + h + 1]
        tot = (jnp.dot(s, vp, preferred_element_type=F32)
               + w_inter * jnp.dot(q_e, s_scr[d, p].astype(BF16), preferred_element_type=F32))
        den = jnp.maximum(jnp.abs(tot[:, M_DV:]), floor)
        h_ref[:, h * M_DV:(h + 1) * M_DV] = (tot[:, :M_DV] / den).astype(BF16)
        s_scr[d, p, rows, :] = (small[h:h + 1, 0:1] * s_scr[d, p, rows, :]
                                + jnp.dot(ktw, vp, preferred_element_type=F32))


def _mlstm_scan_kernel(qf, kf, vf, gf, gf_next, qb, kb, vb, gb, gb_next, s0_ref, m0_ref,
                       hf_ref, hb_ref, st_ref, mt_ref, s_scr, m_scr, rows_scr, cols_scr, small_scr):
    n = pl.program_id(1)
    H = M_HEADS
    g_cur, g_next = (gf, gb), (gf_next, gb_next)

    @pl.when(n == 0)
    def _():
        s_scr[...] = s0_ref[0]
        for d in range(2):
            rows_scr[d], cols_scr[d], small_scr[d] = _mlstm_gates(
                g_cur[d], m0_ref[0, d * H:(d + 1) * H, 0:1], d == 1, d)

    cur = [(rows_scr[d], cols_scr[d], small_scr[d]) for d in range(2)]
    m_new = [cur[d][2][:, LANES // 2:LANES // 2 + 1] for d in range(2)]
    m_scr[...] = jnp.broadcast_to(jnp.concatenate(m_new, axis=0), m_scr.shape)
    for d in range(2):
        rows_scr[d], cols_scr[d], small_scr[d] = _mlstm_gates(g_next[d], m_new[d], d == 1, d)
    for p0 in range(0, M_HEADS // 2, MLSTM_PAIR_GROUP):
        pairs = range(p0, p0 + MLSTM_PAIR_GROUP)
        _mlstm_heads(qf, kf, vf, *cur[0], hf_ref, s_scr, False, 0, pairs)
        _mlstm_heads(qb, kb, vb, *cur[1], hb_ref, s_scr, True, 1, pairs)

    @pl.when(n == pl.num_programs(1) - 1)
    def _():
        st_ref[0] = s_scr[...]
        mt_ref[0] = m_scr[...]


def _mlstm_scan(q, kt, v, gates, s0, m0, seg):
    L = CHUNK
    assert seg.seqlen % L == 0
    n = seg.seqlen // L
    fwd = lambda b, i: (b * n + i, 0)
    bwd = lambda b, i: (b * n + (n - 1 - i), 0)
    fwd_next = lambda b, i: (b * n + jnp.minimum(i + 1, n - 1), 0)
    bwd_next = lambda b, i: (b * n + jnp.maximum(n - 2 - i, 0), 0)
    fwd_t = lambda b, i: (0, b * n + i)
    bwd_t = lambda b, i: (0, b * n + (n - 1 - i))
    dq, dv = q.shape[1], v.shape[1]
    st = lambda b, i: (b, 0, 0, 0, 0)
    mt = lambda b, i: (b, 0, 0)
    return pl.pallas_call(
        _mlstm_scan_kernel,
        out_shape=[jax.ShapeDtypeStruct((seg.n, dv), BF16), jax.ShapeDtypeStruct((seg.n, dv), BF16),
                   jax.ShapeDtypeStruct(s0.shape, F32), jax.ShapeDtypeStruct(m0.shape, F32)],
        grid=(seg.batch, n),
        in_specs=[pl.BlockSpec((L, dq), fwd), pl.BlockSpec((dq, L), fwd_t), pl.BlockSpec((L, dv), fwd),
                  pl.BlockSpec((L, LANES), fwd), pl.BlockSpec((L, LANES), fwd_next),
                  pl.BlockSpec((L, dq), bwd), pl.BlockSpec((dq, L), bwd_t), pl.BlockSpec((L, dv), bwd),
                  pl.BlockSpec((L, LANES), bwd), pl.BlockSpec((L, LANES), bwd_next),
                  pl.BlockSpec((1,) + s0.shape[1:], st), pl.BlockSpec((1,) + m0.shape[1:], mt)],
        out_specs=[pl.BlockSpec((L, dv), fwd), pl.BlockSpec((L, dv), bwd),
                   pl.BlockSpec((1,) + s0.shape[1:], st), pl.BlockSpec((1,) + m0.shape[1:], mt)],
        scratch_shapes=[pltpu.VMEM(s0.shape[1:], F32), pltpu.VMEM(m0.shape[1:], F32),
                        pltpu.VMEM((2, 2 * M_HEADS, L), F32), pltpu.VMEM((2, L, LANES), F32),
                        pltpu.VMEM((2, M_HEADS, LANES), F32)],
        compiler_params=_cp("arbitrary", "arbitrary"),
        name="mlstm_scan",
    )(q, kt, v, gates, gates, q, kt, v, gates, gates, s0, m0)


def _head_out(hf, hb, gate_pre, gh, w, n_heads):
    h = hf.astype(F32) + hb.astype(F32)
    dv = h.shape[1] // n_heads
    parts = [_rms(h[:, i * dv:(i + 1) * dv]) for i in range(n_heads)]
    hn = jnp.concatenate(parts, axis=1) * gh * _sigmoid(gate_pre.astype(F32))
    return jnp.dot(hn.astype(BF16), w, preferred_element_type=F32)


def _mix_out_kernel(hf_ref, hb_ref, o_ref, x_ref, mod_ref, gh_ref, w_ref, g1_ref, out_ref, *, n_heads):
    mod = mod_ref[0]
    y = _head_out(hf_ref[...], hb_ref[...], o_ref[...], gh_ref[...], w_ref[...], n_heads)
    out_ref[...] = x_ref[...] + mod[2:3] * (_rms(y) * g1_ref[...])


def _mix_out(hf, hb, o, x, seg, mod, gh, w, g1, n_heads, in_place=True):
    n, d = x.shape
    tm = seg.tile(512)
    rmap = lambda j: (j, 0)
    consts = (gh, w, g1)
    return pl.pallas_call(
        functools.partial(_mix_out_kernel, n_heads=n_heads),
        out_shape=jax.ShapeDtypeStruct(x.shape, F32),
        grid=(n // tm,),
        in_specs=[pl.BlockSpec((tm, hf.shape[1]), rmap), pl.BlockSpec((tm, hf.shape[1]), rmap),
                  pl.BlockSpec((tm, o.shape[1]), rmap), pl.BlockSpec((tm, d), rmap),
                  seg.mod_spec(tm, d, 1)] + [_full(a, 1) for a in consts],
        out_specs=pl.BlockSpec((tm, d), rmap),
        input_output_aliases={3: 0} if in_place else {},
        compiler_params=_cp("arbitrary"),
        name="mix_out",
    )(hf, hb, o, x, mod, *consts)


def _swiglu_acc(u_scr, wgu_ref, wd_ref, acc_scr, th):
    hd = wd_ref.shape[0]
    assert hd % th == 0 and hd % LANES == 0

    def chunk(k, carry):
        off = pl.multiple_of(k * th, th)
        u = u_scr[...]
        gg = jnp.dot(u, wgu_ref[:, pl.ds(off, th)], preferred_element_type=F32)
        up = jnp.dot(u, wgu_ref[:, pl.ds(pl.multiple_of(hd + off, LANES), th)], preferred_element_type=F32)
        hh = (_silu(gg) * up).astype(BF16)
        acc_scr[...] += jnp.dot(hh, wd_ref[pl.ds(off, th), :], preferred_element_type=F32)
        return carry

    acc_scr[...] = jnp.zeros_like(acc_scr)
    lax.fori_loop(0, hd // th, chunk, 0)


def _ffn_kernel(x_ref, mod_ref, g2_ref, wgu_ref, wd_ref, g3_ref, out_ref, u_scr, acc_scr, *, th):
    u_scr[...] = _normmod(x_ref[...], g2_ref[...], mod_ref[0], 3).astype(BF16)
    _swiglu_acc(u_scr, wgu_ref, wd_ref, acc_scr, th)
    mod = mod_ref[0]
    out_ref[...] = x_ref[...] + mod[5:6] * (_rms(acc_scr[...]) * g3_ref[...])


def _resident(a, index_map):
    return pl.BlockSpec(a, index_map, pipeline_mode=pl.Buffered(1))


def _ffn(x, seg, mod, g2, wgu, wd, g3, th):
    n, d = x.shape
    tm = seg.tile(1024)
    return pl.pallas_call(
        functools.partial(_ffn_kernel, th=th),
        out_shape=jax.ShapeDtypeStruct((n, d), F32),
        grid=(n // tm,),
        in_specs=[pl.BlockSpec((tm, d), lambda i: (i, 0)),
                  seg.mod_spec(tm, d, 1),
                  _full(g2, 1),
                  _resident(wgu.shape, lambda i: (0, 0)),
                  _resident(wd.shape, lambda i: (0, 0)),
                  _full(g3, 1)],
        out_specs=pl.BlockSpec((tm, d), lambda i: (i, 0)),
        scratch_shapes=[pltpu.VMEM((tm, d), BF16), pltpu.VMEM((tm, d), F32)],
        input_output_aliases={0: 0},
        compiler_params=_cp("arbitrary"),
        name="ffn",
    )(x, mod, g2, wgu, wd, g3)


def _router_kernel(x_ref, mod_ref, g2_ref, wr_ref, br_ref, u_ref, comb_ref):
    u = _normmod(x_ref[...], g2_ref[...], mod_ref[0], 3)
    u_ref[...] = u.astype(BF16)
    lane = lax.broadcasted_iota(jnp.int32, (1, LANES), 1)
    logits = jnp.dot(u, wr_ref[...], preferred_element_type=F32, precision=HI) + br_ref[...]
    logits = jnp.where(lane < N_EXPERTS, logits, NEG)
    m1 = jnp.max(logits, axis=1, keepdims=True)
    i1 = jnp.min(jnp.where(logits == m1, lane, LANES), axis=1, keepdims=True)
    rest = jnp.where(lane == i1, NEG, logits)
    m2 = jnp.max(rest, axis=1, keepdims=True)
    i2 = jnp.min(jnp.where(rest == m2, lane, LANES), axis=1, keepdims=True)
    e2 = jnp.exp(m2 - m1)
    w1 = 1.0 / (1.0 + e2)
    comb_ref[...] = jnp.where(lane == i1, w1, 0.0) + jnp.where(lane == i2, e2 * w1, 0.0)


def _router(x, seg, mod, g2, wr, br):
    n, d = x.shape
    tm = seg.tile(1024)
    consts = (g2, wr, br)
    return pl.pallas_call(
        _router_kernel,
        out_shape=[jax.ShapeDtypeStruct((n, d), BF16), jax.ShapeDtypeStruct((n, LANES), F32)],
        grid=(n // tm,),
        in_specs=[pl.BlockSpec((tm, d), lambda i: (i, 0)), seg.mod_spec(tm, d, 1)]
                 + [_full(a, 1) for a in consts],
        out_specs=[pl.BlockSpec((tm, d), lambda i: (i, 0)), pl.BlockSpec((tm, LANES), lambda i: (i, 0))],
        compiler_params=_cp("arbitrary"),
        name="router",
    )(x, mod, *consts)


def _moe_kernel(u_ref, comb_ref, x_ref, mod_ref, wg_ref, wu_ref, wd_ref, g3_ref, out_ref, acc_scr):
    e = pl.program_id(1)
    kk = pl.program_id(2)

    @pl.when((e == 0) & (kk == 0))
    def _():
        acc_scr[...] = jnp.zeros_like(acc_scr)

    u = u_ref[...]
    gg = jnp.dot(u, wg_ref[0], preferred_element_type=F32)
    up = jnp.dot(u, wu_ref[0], preferred_element_type=F32)
    hh = (_silu(gg) * up).astype(BF16)
    lane = lax.broadcasted_iota(jnp.int32, (1, LANES), 1)
    cw = jnp.sum(jnp.where(lane == e, comb_ref[...], 0.0), axis=1, keepdims=True)
    acc_scr[...] += cw * jnp.dot(hh, wd_ref[0], preferred_element_type=F32)

    @pl.when((e == pl.num_programs(1) - 1) & (kk == pl.num_programs(2) - 1))
    def _():
        mod = mod_ref[0]
        out_ref[...] = x_ref[...] + mod[5:6] * (_rms(acc_scr[...]) * g3_ref[...])


def _moe(u, comb, x, seg, mod, wgu, wd, g3, th):
    n, d = x.shape
    tm = seg.tile(1024)
    ne, hd, _ = wd.shape
    nk = hd // th
    return pl.pallas_call(
        _moe_kernel,
        out_shape=jax.ShapeDtypeStruct((n, d), F32),
        grid=(n // tm, ne, nk),
        in_specs=[pl.BlockSpec((tm, d), lambda i, e, k: (i, 0)),
                  pl.BlockSpec((tm, LANES), lambda i, e, k: (i, 0)),
                  pl.BlockSpec((tm, d), lambda i, e, k: (i, 0)),
                  seg.mod_spec(tm, d, 3),
                  pl.BlockSpec((1, d, th), lambda i, e, k: (e, 0, k)),
                  pl.BlockSpec((1, d, th), lambda i, e, k: (e, 0, nk + k)),
                  pl.BlockSpec((1, th, d), lambda i, e, k: (e, k, 0)),
                  _full(g3, 3)],
        out_specs=pl.BlockSpec((tm, d), lambda i, e, k: (i, 0)),
        scratch_shapes=[pltpu.VMEM((tm, d), F32)],
        input_output_aliases={2: 0},
        compiler_params=_cp("arbitrary", "arbitrary", "arbitrary"),
        name="moe",
    )(u, comb, x, mod, wgu, wgu, wd, g3)


ROW_TILE = 1024
META_E, META_RANK, META_W = 0, 2, 4


def _route_kernel(x_ref, mod_ref, g2_ref, wr_ref, br_ref, meta_ref, cnt_ref, tri_scr, run_scr):
    i = pl.program_id(0)
    tm = x_ref.shape[0]

    @pl.when(i == 0)
    def _():
        r = lax.broadcasted_iota(jnp.int32, (tm, tm), 0)
        c = lax.broadcasted_iota(jnp.int32, (tm, tm), 1)
        tri_scr[...] = jnp.where(c < r, 1.0, 0.0).astype(BF16)
        run_scr[...] = jnp.zeros_like(run_scr)

    u = _normmod(x_ref[...], g2_ref[...], mod_ref[0], 3)
    lane = lax.broadcasted_iota(jnp.int32, (1, LANES), 1)
    u_hi, u_lo = _split2(u)
    w_hi, w_lo = _split2(wr_ref[...])
    logits = (jnp.dot(u_hi, w_hi, preferred_element_type=F32) + jnp.dot(u_lo, w_hi, preferred_element_type=F32)
              + jnp.dot(u_hi, w_lo, preferred_element_type=F32)) + br_ref[...]
    logits = jnp.where(lane < N_EXPERTS, logits, NEG)
    m1 = jnp.max(logits, axis=1, keepdims=True)
    i1 = jnp.min(jnp.where(logits == m1, lane, LANES), axis=1, keepdims=True)
    rest = jnp.where(lane == i1, NEG, logits)
    m2 = jnp.max(rest, axis=1, keepdims=True)
    i2 = jnp.min(jnp.where(rest == m2, lane, LANES), axis=1, keepdims=True)
    e2 = jnp.exp(m2 - m1)
    w1 = 1.0 / (1.0 + e2)
    sel = ((lane == i1) | (lane == i2))
    onehot = jnp.where(sel, 1.0, 0.0)
    before = jnp.dot(tri_scr[...], onehot.astype(BF16), preferred_element_type=F32) + run_scr[...]
    rank1 = jnp.sum(jnp.where(lane == i1, before, 0.0), axis=1, keepdims=True)
    rank2 = jnp.sum(jnp.where(lane == i2, before, 0.0), axis=1, keepdims=True)
    run_scr[...] = run_scr[...] + jnp.sum(onehot, axis=0, keepdims=True)
    vals = (i1.astype(F32), i2.astype(F32), rank1, rank2, w1, e2 * w1)
    meta = jnp.zeros((tm, LANES), F32)
    for col, val in enumerate(vals):
        meta = jnp.where(lane == col, val, meta)
    meta_ref[...] = meta
    cnt_ref[...] = jnp.broadcast_to(run_scr[...], cnt_ref.shape)


def _route(x, seg, mod, g2, wr, br):
    n, d = x.shape
    tm = seg.tile(1024)
    consts = (g2, wr, br)
    return pl.pallas_call(
        _route_kernel,
        out_shape=[jax.ShapeDtypeStruct((n, LANES), F32),
                   jax.ShapeDtypeStruct((SUBLANES, LANES), F32)],
        grid=(n // tm,),
        in_specs=[pl.BlockSpec((tm, d), lambda i: (i, 0)), seg.mod_spec(tm, d, 1)]
                 + [_full(a, 1) for a in consts],
        out_specs=[pl.BlockSpec((tm, LANES), lambda i: (i, 0)),
                   pl.BlockSpec((SUBLANES, LANES), lambda i: (0, 0))],
        scratch_shapes=[pltpu.VMEM((tm, tm), BF16), pltpu.VMEM((1, LANES), F32)],
        compiler_params=_cp("arbitrary"),
        name="route",
    )(x, mod, *consts)


def _drain(src_ref, dst_ref, sem, count):
    def body(i, carry):
        pltpu.make_async_copy(src_ref, dst_ref, sem).wait()
        return carry

    lax.fori_loop(0, count, body, 0, unroll=8)


def _row_tile(buf, t):
    return buf.at[pl.ds(pl.multiple_of(t * SUBLANES, SUBLANES), SUBLANES), :]


def _dispatch_kernel(dst_ref, pad_ref, x_ref, mod_ref, g2_ref, xs_ref, ubuf, sems):
    j = pl.program_id(0)
    tm = x_ref.shape[0]
    slot = lax.rem(j, 2)
    buf = ubuf.at[slot]
    sem = sems.at[slot]
    u = _normmod(x_ref[...], g2_ref[...], mod_ref[0], 3)
    for s in range(SUBLANES):
        ubuf[slot, pl.ds(s, tm, stride=SUBLANES), :] = u[:, s * LANES:(s + 1) * LANES]

    def issue(t, carry):
        pltpu.make_async_copy(_row_tile(buf, t), xs_ref.at[dst_ref[0, 0, t]], sem).start(priority=0)
        pltpu.make_async_copy(_row_tile(buf, t), xs_ref.at[dst_ref[0, 0, tm + t]], sem).start(priority=1)
        return carry

    lax.fori_loop(0, tm, issue, 0, unroll=8)
    npad = pad_ref.shape[2]
    first = _row_tile(buf, 0)

    @pl.when(j == 0)
    def _():
        def issue_pad(p, carry):
            pltpu.make_async_copy(first, xs_ref.at[pad_ref[0, 0, p]], sem).start()
            return carry

        lax.fori_loop(0, npad, issue_pad, 0, unroll=8)
        _drain(first, xs_ref.at[0], sem, npad)

    @pl.when(j > 0)
    def _():
        _drain(_row_tile(ubuf.at[1 - slot], 0), xs_ref.at[0], sems.at[1 - slot], 2 * tm)

    @pl.when(j == pl.num_programs(0) - 1)
    def _():
        _drain(first, xs_ref.at[0], sem, 2 * tm)


def _dispatch(x, seg, mod, g2, dest, pad_dst, n_rows):
    n, d = x.shape
    tm = seg.tile(512)
    steps = n // tm
    dst = dest.reshape(steps, tm, 2).transpose(0, 2, 1).reshape(steps, 1, 2 * tm)
    npad = pad_dst.shape[0]
    return pl.pallas_call(
        _dispatch_kernel,
        out_shape=jax.ShapeDtypeStruct((n_rows, SUBLANES, d // SUBLANES), F32),
        grid=(steps,),
        in_specs=[pl.BlockSpec((1, 1, 2 * tm), lambda j: (j, 0, 0), memory_space=pltpu.SMEM),
                  pl.BlockSpec((1, 1, npad), lambda j: (0, 0, 0), memory_space=pltpu.SMEM),
                  pl.BlockSpec((tm, d), lambda j: (j, 0)), seg.mod_spec(tm, d, 1), _full(g2, 1)],
        out_specs=pl.BlockSpec(memory_space=pl.ANY),
        scratch_shapes=[pltpu.VMEM((2, tm * SUBLANES, d // SUBLANES), F32), pltpu.SemaphoreType.DMA((2,))],
        compiler_params=_cp("arbitrary"),
        name="dispatch",
    )(dst, pad_dst.reshape(1, 1, npad), x, mod, g2)


def _experts_kernel(te_ref, used_ref, xs_ref, wgu_ref, wd_ref, y_ref, u_scr, acc_scr, *, th):
    i = pl.program_id(0)

    @pl.when(i < used_ref[0])
    def _():
        tr = u_scr.shape[0]
        for s in range(SUBLANES):
            u_scr[:, s * LANES:(s + 1) * LANES] = xs_ref[pl.ds(s, tr, stride=SUBLANES), :].astype(BF16)
        _swiglu_acc(u_scr, wgu_ref.at[0], wd_ref.at[0], acc_scr, th)
        y = acc_scr[...]
        for s in range(SUBLANES):
            y_ref[pl.ds(s, tr, stride=SUBLANES), :] = y[:, s * LANES:(s + 1) * LANES]


def _experts(xs3, tile_expert, used, wgu, wd, th):
    tr = ROW_TILE
    n_rows = xs3.shape[0]
    ne, hd, d = wd.shape
    tiles = tile_expert.shape[0]
    assert tiles * tr <= n_rows
    row = lambda i, te, us: (jnp.minimum(i, us[0] - 1), 0)
    grid_spec = pltpu.PrefetchScalarGridSpec(
        num_scalar_prefetch=2,
        grid=(tiles,),
        in_specs=[pl.BlockSpec((tr * SUBLANES, LANES), row),
                  _resident((1,) + wgu.shape[1:], lambda i, te, us: (te[i], 0, 0)),
                  _resident((1,) + wd.shape[1:], lambda i, te, us: (te[i], 0, 0))],
        out_specs=pl.BlockSpec((tr * SUBLANES, LANES), row),
        scratch_shapes=[pltpu.VMEM((tr, d), BF16), pltpu.VMEM((tr, d), F32)],
    )
    y = pl.pallas_call(
        functools.partial(_experts_kernel, th=th),
        out_shape=jax.ShapeDtypeStruct((tiles * tr * SUBLANES, LANES), F32),
        grid_spec=grid_spec,
        compiler_params=_cp("arbitrary"),
        name="experts",
    )(tile_expert, used, xs3.reshape(n_rows * SUBLANES, LANES), wgu, wd)
    return y.reshape(tiles * tr, SUBLANES, LANES)


def _combine_kernel(src_ref, nxt_ref, y_ref, meta_ref, x_ref, mod_ref, g3_ref, out_ref, ybuf, sems):
    j = pl.program_id(0)
    tm = x_ref.shape[0]
    slot = lax.rem(j, 2)

    def gather(idx_ref, sl):
        def issue(t, carry):
            pltpu.make_async_copy(y_ref.at[idx_ref[0, 0, t]], _row_tile(ybuf.at[sl, 0], t),
                                  sems.at[sl]).start(priority=0)
            pltpu.make_async_copy(y_ref.at[idx_ref[0, 0, tm + t]], _row_tile(ybuf.at[sl, 1], t),
                                  sems.at[sl]).start(priority=1)
            return carry

        lax.fori_loop(0, tm, issue, 0, unroll=8)

    @pl.when(j == 0)
    def _():
        gather(src_ref, 0)

    @pl.when(j + 1 < pl.num_programs(0))
    def _():
        gather(nxt_ref, 1 - slot)

    _drain(y_ref.at[0], _row_tile(ybuf.at[slot, 0], 0), sems.at[slot], 2 * tm)
    meta = meta_ref[...]
    w1 = meta[:, META_W:META_W + 1]
    w2 = meta[:, META_W + 1:META_W + 2]
    ys = [w1 * ybuf[slot, 0, pl.ds(s, tm, stride=SUBLANES), :]
          + w2 * ybuf[slot, 1, pl.ds(s, tm, stride=SUBLANES), :] for s in range(SUBLANES)]
    ss = ys[0] * ys[0]
    for y in ys[1:]:
        ss = ss + y * y
    d = x_ref.shape[1]
    r = lax.rsqrt(jnp.sum(ss, axis=1, keepdims=True) / d + EPS)
    mod = mod_ref[0]
    for s, y in enumerate(ys):
        sl = slice(s * LANES, (s + 1) * LANES)
        out_ref[:, sl] = x_ref[:, sl] + mod[5:6, sl] * (y * r * g3_ref[:, sl])


def _combine(y3, dest, meta, x, seg, mod, g3):
    n, d = x.shape
    tm = seg.tile(512)
    steps = n // tm
    src = dest.reshape(steps, tm, 2).transpose(0, 2, 1).reshape(steps, 1, 2 * tm)
    return pl.pallas_call(
        _combine_kernel,
        out_shape=jax.ShapeDtypeStruct((n, d), F32),
        grid=(steps,),
        in_specs=[pl.BlockSpec((1, 1, 2 * tm), lambda j: (j, 0, 0), memory_space=pltpu.SMEM),
                  pl.BlockSpec((1, 1, 2 * tm), lambda j: (jnp.minimum(j + 1, steps - 1), 0, 0),
                               memory_space=pltpu.SMEM),
                  pl.BlockSpec(memory_space=pl.ANY),
                  pl.BlockSpec((tm, LANES), lambda j: (j, 0)),
                  pl.BlockSpec((tm, d), lambda j: (j, 0)),
                  seg.mod_spec(tm, d, 1), _full(g3, 1)],
        out_specs=pl.BlockSpec((tm, d), lambda j: (j, 0)),
        scratch_shapes=[pltpu.VMEM((2, 2, tm * SUBLANES, d // SUBLANES), F32), pltpu.SemaphoreType.DMA((2,))],
        input_output_aliases={4: 0},
        compiler_params=_cp("arbitrary"),
        name="combine",
    )(src, src, y3, meta, x, mod, g3)


def _moe_routed(x, seg, mod, g2, wr, br, wgu, wd, g3, th):
    n, d = x.shape
    assert d == SUBLANES * LANES
    tr = ROW_TILE
    meta, cnt = _route(x, seg, mod, g2, wr, br)
    counts = cnt[0, :N_EXPERTS].astype(jnp.int32)
    sizes = ((counts + tr - 1) // tr) * tr
    ends = jnp.cumsum(sizes)
    offs = ends - sizes
    mi = meta[:, :META_W].astype(jnp.int32)
    eidx = mi[:, META_E:META_E + 2]
    dest = jnp.sum(jnp.where(eidx[:, :, None] == jnp.arange(N_EXPERTS)[None, None, :],
                             offs[None, None, :], 0), axis=2) + mi[:, META_RANK:META_RANK + 2]
    dest = dest.astype(jnp.int32)
    tiles = (2 * n) // tr + N_EXPERTS
    n_rows = tiles * tr
    used = (ends[-1] // tr).astype(jnp.int32).reshape(1)
    tile_start = jnp.arange(tiles, dtype=jnp.int32) * tr
    tile_expert = jnp.sum(tile_start[:, None] >= ends[None, :], axis=1).astype(jnp.int32)
    last_expert = jnp.sum(jnp.maximum(ends[-1] - tr, 0) >= ends).astype(jnp.int32)
    tile_expert = jnp.where(tile_start < ends[-1], tile_expert, last_expert)
    pad = jnp.arange(N_EXPERTS * tr, dtype=jnp.int32).reshape(N_EXPERTS, tr)
    pad_row = offs[:, None] + counts[:, None] + jnp.arange(tr, dtype=jnp.int32)[None, :]
    pad_dst = jnp.where(pad_row < ends[:, None], pad_row, n_rows + pad).reshape(-1).astype(jnp.int32)
    xs3 = _dispatch(x, seg, mod, g2, dest, pad_dst, n_rows + N_EXPERTS * tr)
    y3 = _experts(xs3, tile_expert, used, wgu, wd, th)
    return _combine(y3, dest, meta, x, seg, mod, g3)


def _proj_h_body(u, w_refs, out_refs):
    for w_ref, o_ref in zip(w_refs, out_refs):
        o_ref[...] = jnp.dot(u, w_ref[...], preferred_element_type=F32).astype(BF16)


def _proj_h_lat_kernel(x_ref, mod_ref, g_ref, *refs, rows_per_col):
    n = len(refs) // 2
    w_refs, out_refs = refs[:n], refs[n:]
    x = jnp.swapaxes(x_ref[...], 0, 1)
    x = x.reshape(SUBLANES * rows_per_col, x.shape[2])
    u = _normmod(x, g_ref[...], mod_ref[0], 0).astype(BF16)
    _proj_h_body(u, w_refs, out_refs)


def _proj_h_ctx_kernel(x_ref, mod_ref, g_ref, *refs):
    n = len(refs) // 2
    u = _normmod(x_ref[...], g_ref[...], mod_ref[0], 0).astype(BF16)
    _proj_h_body(u, refs[:n], refs[n:])


def _proj_h_lat(x, seg, mod, g, ws):
    n, d = x.shape
    grid_rows = seg.seqlen // GRID_W
    x3 = x.reshape(n // GRID_W, GRID_W, d)
    tcol = SUBLANES * grid_rows
    ncb = GRID_W // SUBLANES
    return pl.pallas_call(
        functools.partial(_proj_h_lat_kernel, rows_per_col=grid_rows),
        out_shape=[jax.ShapeDtypeStruct((n, w.shape[1]), BF16) for w in ws],
        grid=(seg.batch, ncb),
        in_specs=[pl.BlockSpec((grid_rows, SUBLANES, d), lambda b, c: (b, c, 0)),
                  pl.BlockSpec((1, N_MOD, d), lambda b, c: (b, 0, 0)), _full(g, 2)]
                 + [_full(w, 2) for w in ws],
        out_specs=[pl.BlockSpec((tcol, w.shape[1]), lambda b, c: (b * ncb + c, 0)) for w in ws],
        compiler_params=_cp("arbitrary", "arbitrary"),
        name="hgrn_proj_lat",
    )(x3, mod, g, *ws)


def _proj_h_ctx(x, seg, mod, g, ws):
    n, d = x.shape
    tm = seg.tile(512)
    return pl.pallas_call(
        _proj_h_ctx_kernel,
        out_shape=[jax.ShapeDtypeStruct((n, w.shape[1]), BF16) for w in ws],
        grid=(n // tm,),
        in_specs=[pl.BlockSpec((tm, d), lambda j: (j, 0)), seg.mod_spec(tm, d, 1), _full(g, 1)]
                 + [_full(w, 1) for w in ws],
        out_specs=[pl.BlockSpec((tm, w.shape[1]), lambda j: (j, 0)) for w in ws],
        compiler_params=_cp("arbitrary"),
        name="hgrn_proj_ctx",
    )(x, mod, g, *ws)


HGRN_HEAD_GROUP = 4


def _hgrn_prologue(f_ref, bias_ref, lb_ref, reverse, d):
    lb = lb_ref[d:d + 1, :]
    f = lb + (1.0 - lb) * _sigmoid(f_ref[...].astype(F32) + bias_ref[d:d + 1, :])
    return f, _cumsum_time(jnp.log(f), _cumsum_mat(CHUNK, reverse))


def _hgrn_heads(q_ref, v_ref, pro, h_ref, s_scr, reverse, d, heads):
    L = CHUNK
    f, a_all = pro
    r = lax.broadcasted_iota(jnp.int32, (L, L), 0)
    c = lax.broadcasted_iota(jnp.int32, (L, L), 1)
    causal = (c >= r) if reverse else (c <= r)
    same64 = (r // 64) == (c // 64)
    diag32 = ((r // 32) == (c // 32)) & causal
    t = lax.broadcasted_iota(jnp.int32, (L, 1), 0)
    if reverse:
        late0, late1 = t < 64, (t % 64) < 32
        a0, a1 = (64,), (32, 96)
        a2 = (31, 63, 95, 127)
        last = 0
    else:
        late0, late1 = t >= 64, (t % 64) >= 32
        a0, a1 = (63,), (31, 95)
        a2 = (0, 32, 64, 96)
        last = L - 1
    staged = []
    for h in heads:
        sl = slice(h * H_DK, (h + 1) * H_DK)
        a = a_all[:, sl]
        q = q_ref[:, sl].astype(F32)
        k = 1.0 - f[:, sl]
        anc0 = jnp.broadcast_to(a[a0[0]:a0[0] + 1, :], (L, H_DK))
        anc1 = jnp.where(t < 64, a[a1[0]:a1[0] + 1, :], a[a1[1]:a1[1] + 1, :])
        anc2 = jnp.where(t < 32, a[a2[0]:a2[0] + 1, :],
                         jnp.where(t < 64, a[a2[1]:a2[1] + 1, :],
                                   jnp.where(t < 96, a[a2[2]:a2[2] + 1, :], a[a2[3]:a2[3] + 1, :])))

        def level(anc, late):
            qs = q * jnp.exp(a - anc)
            ks = k * jnp.exp(anc - a)
            if late is not None:
                qs = jnp.where(late, qs, 0.0)
                ks = jnp.where(late, 0.0, ks)
            return lax.dot_general(qs.astype(BF16), ks.astype(BF16), NT_DIMS, preferred_element_type=F32)

        s = jnp.where(diag32, level(anc2, None), jnp.where(same64, level(anc1, late1), level(anc0, late0)))
        qd = (q * jnp.exp(a)).astype(BF16)
        a_last = a[last:last + 1, :]
        kd = (k * jnp.exp(a_last - a)).astype(BF16)
        staged.append((s.astype(BF16), qd, kd, jnp.exp(a_last)))
    for h, (s, qd, kd, dec) in zip(heads, staged):
        v = v_ref[:, h * H_DV:(h + 1) * H_DV]
        st = s_scr[d, h]
        o = (jnp.dot(s, v, preferred_element_type=F32)
             + lax.dot_general(qd, st.astype(BF16), NT_DIMS, preferred_element_type=F32))
        h_ref[:, h * H_DV:(h + 1) * H_DV] = o.astype(BF16)
        s_scr[d, h] = st * dec + lax.dot_general(v, kd, TN_DIMS, preferred_element_type=F32)


def _hgrn_scan_kernel(qf, vf, ff, qb, vb, fb, bias_ref, lb_ref, s0_ref, hf_ref, hb_ref, st_ref, s_scr):
    n = pl.program_id(1)

    @pl.when(n == 0)
    def _():
        s_scr[...] = s0_ref[0]

    pro_f = _hgrn_prologue(ff, bias_ref, lb_ref, False, 0)
    pro_b = _hgrn_prologue(fb, bias_ref, lb_ref, True, 1)
    for h0 in range(0, H_HEADS, HGRN_HEAD_GROUP):
        heads = range(h0, h0 + HGRN_HEAD_GROUP)
        _hgrn_heads(qf, vf, pro_f, hf_ref, s_scr, False, 0, heads)
        _hgrn_heads(qb, vb, pro_b, hb_ref, s_scr, True, 1, heads)

    @pl.when(n == pl.num_programs(1) - 1)
    def _():
        st_ref[0] = s_scr[...]


def _hgrn_scan(q, v, f_f, f_b, bias, lb, s0, seg):
    L = CHUNK
    assert seg.seqlen % L == 0
    n = seg.seqlen // L
    fwd = lambda b, i: (b * n + i, 0)
    bwd = lambda b, i: (b * n + (n - 1 - i), 0)
    rows_, w = q.shape
    st = lambda b, i: (b, 0, 0, 0, 0)
    return pl.pallas_call(
        _hgrn_scan_kernel,
        out_shape=[jax.ShapeDtypeStruct((rows_, w), BF16), jax.ShapeDtypeStruct((rows_, w), BF16),
                   jax.ShapeDtypeStruct(s0.shape, F32)],
        grid=(seg.batch, n),
        in_specs=[pl.BlockSpec((L, w), fwd), pl.BlockSpec((L, w), fwd), pl.BlockSpec((L, w), fwd),
                  pl.BlockSpec((L, w), bwd), pl.BlockSpec((L, w), bwd), pl.BlockSpec((L, w), bwd),
                  _full(bias, 2), _full(lb, 2), pl.BlockSpec((1,) + s0.shape[1:], st)],
        out_specs=[pl.BlockSpec((L, w), fwd), pl.BlockSpec((L, w), bwd),
                   pl.BlockSpec((1,) + s0.shape[1:], st)],
        scratch_shapes=[pltpu.VMEM(s0.shape[1:], F32)],
        compiler_params=_cp("arbitrary", "arbitrary"),
        name="hgrn_scan",
    )(q, v, f_f, q, v, f_b, bias, lb, s0)


def _mix_out_col_kernel(hf_ref, hb_ref, o_ref, x_ref, mod_ref, gh_ref, w_ref, g1_ref, out_ref,
                        *, n_heads, rows_per_col):
    mod = mod_ref[0]
    y = _head_out(hf_ref[...], hb_ref[...], o_ref[...], gh_ref[...], w_ref[...], n_heads)
    upd = mod[2:3] * (_rms(y) * g1_ref[...])
    upd = jnp.swapaxes(upd.reshape(SUBLANES, rows_per_col, upd.shape[1]), 0, 1)
    out_ref[...] = x_ref[...] + upd


def _mix_out_col(hf, hb, o, x, seg, mod, gh, w, g1, n_heads):
    n, d = x.shape
    grid_rows = seg.seqlen // GRID_W
    tcol = SUBLANES * grid_rows
    ncb = GRID_W // SUBLANES
    x3 = x.reshape(n // GRID_W, GRID_W, d)
    hmap = lambda b, c: (b * ncb + c, 0)
    xmap = lambda b, c: (b, c, 0)
    consts = (gh, w, g1)
    out = pl.pallas_call(
        functools.partial(_mix_out_col_kernel, n_heads=n_heads, rows_per_col=grid_rows),
        out_shape=jax.ShapeDtypeStruct(x3.shape, F32),
        grid=(seg.batch, ncb),
        in_specs=[pl.BlockSpec((tcol, hf.shape[1]), hmap), pl.BlockSpec((tcol, hf.shape[1]), hmap),
                  pl.BlockSpec((tcol, o.shape[1]), hmap),
                  pl.BlockSpec((grid_rows, SUBLANES, d), xmap),
                  pl.BlockSpec((1, N_MOD, d), lambda b, c: (b, 0, 0))] + [_full(a, 2) for a in consts],
        out_specs=pl.BlockSpec((grid_rows, SUBLANES, d), xmap),
        input_output_aliases={3: 0},
        compiler_params=_cp("arbitrary", "arbitrary"),
        name="mix_out_col",
    )(hf, hb, o, x3, mod, *consts)
    return out.reshape(n, d)


def kernel(x, c, ctx, c_ctx, mod_w, mod_b, norm_g, m_w_in, m_b_gate, m_w_conv, m_g_head, m_w_out,
           h_w_in, h_b_f, h_lb_raw, h_g_head, h_w_out, f_w_gu, f_w_down,
           e_w_router, e_b_router, e_w_gu, e_w_down):
    batch, seq, d = x.shape
    ctx_len = ctx.shape[1]
    depth = mod_w.shape[0]
    assert seq % GRID_W == 0 and seq // GRID_W == CHUNK and ctx_len % CHUNK == 0
    assert batch < MOD_ROWS and d % LANES == 0
    lat_seg = _Seg(batch, seq, False)
    ctx_seg = _Seg(batch, ctx_len, True)

    c_all = jnp.zeros((MOD_ROWS, d), F32).at[:batch].set(c).at[batch].set(c_ctx)
    mod_all = _modulation(c_all, mod_w, mod_b).reshape(depth, MOD_ROWS, N_MOD, d)

    lb_all = jax.nn.softmax(h_lb_raw.astype(F32), axis=0)
    lb_all = jnp.cumsum(lb_all, axis=0) - lb_all[0]

    xl = x.reshape(batch * seq, d)
    xc = ctx.reshape(batch * ctx_len, d)
    row = lambda v: v.reshape(1, -1).astype(F32)

    for i in range(depth):
        j = i // 2
        with_ctx = i < depth - 1
        mod = mod_all[i]
        g0, g1, g2, g3 = (row(norm_g[i, s]) for s in range(4))
        if i % 2 == 0:
            qk_w = 2 * M_HEADS * M_DK
            v_w = M_HEADS * M_DV
            w_in = m_w_in[j]
            wqk = w_in[:, :qk_w].astype(BF16)
            wv = w_in[:, qk_w:qk_w + v_w].astype(BF16)
            wo = w_in[:, qk_w + v_w:qk_w + 2 * v_w].astype(BF16)
            ng = 4 * M_HEADS
            wg = jnp.zeros((d, LANES), F32).at[:, :ng].set(w_in[:, qk_w + 2 * v_w:]).astype(BF16)
            gb = jnp.zeros((1, LANES), F32).at[0, :ng].set(m_b_gate[j].reshape(-1))
            col = jnp.arange(LANES)
            gm = (((col // M_HEADS) % 2 == 1) & (col < ng)).astype(F32).reshape(1, LANES)
            pw = (mod, g0, wqk, wv, wo, wg, m_w_conv[j].astype(F32), gb, gm)
            qc, kc, vc, oc, gc = _proj_m(xc, ctx_seg, *pw)
            ql, kl, vl, ol, gl = _proj_m(xl, lat_seg, *pw)
            s0 = jnp.zeros((batch, 2, M_HEADS // 2, LANES, 2 * M_DV), F32)
            m0 = jnp.zeros((batch, 2 * M_HEADS, LANES), F32)
            hcf, hcb, s1, m1 = _mlstm_scan(qc, kc, vc, gc, s0, m0, ctx_seg)
            hlf, hlb, _, _ = _mlstm_scan(ql, kl, vl, gl, s1, m1, lat_seg)
            gh = row(m_g_head[j])
            wout = m_w_out[j].astype(BF16)
            xl = _mix_out(hlf, hlb, ol, xl, lat_seg, mod, gh, wout, g1, M_HEADS, in_place=i > 0)
            if with_ctx:
                xc = _mix_out(hcf, hcb, oc, xc, ctx_seg, mod, gh, wout, g1, M_HEADS, in_place=i > 0)
        else:
            kw = H_HEADS * H_DK
            w_in = h_w_in[j].astype(BF16)
            ws = [w_in[:, s * kw:(s + 1) * kw] for s in range(5)]
            qc, vc, ffc, fbc, gc = _proj_h_ctx(xc, ctx_seg, mod, g0, ws)
            ql, vl, ffl, fbl, gl = _proj_h_lat(xl, lat_seg, mod, g0, ws)
            bias = h_b_f[j].astype(F32)
            lb = lb_all[j]
            s0 = jnp.zeros((batch, 2, H_HEADS, H_DV, H_DK), F32)
            hcf, hcb, s1 = _hgrn_scan(qc, vc, ffc, fbc, bias, lb, s0, ctx_seg)
            hlf, hlb, _ = _hgrn_scan(ql, vl, ffl, fbl, bias, lb, s1, lat_seg)
            gh = row(h_g_head[j])
            wout = h_w_out[j].astype(BF16)
            xl = _mix_out_col(hlf, hlb, gl, xl, lat_seg, mod, gh, wout, g1, H_HEADS)
            if with_ctx:
                xc = _mix_out(hcf, hcb, gc, xc, ctx_seg, mod, gh, wout, g1, H_HEADS)
        streams = [(xl, lat_seg)] + ([(xc, ctx_seg)] if with_ctx else [])
        outs = []
        if i % 2 == 0:
            wgu, wd = f_w_gu[j].astype(BF16), f_w_down[j].astype(BF16)
            for xs, seg in streams:
                outs.append(_ffn(xs, seg, mod, g2, wgu, wd, g3, th=256))
        else:
            wr = jnp.zeros((d, LANES), F32).at[:, :N_EXPERTS].set(e_w_router[j])
            br = jnp.zeros((1, LANES), F32).at[0, :N_EXPERTS].set(e_b_router[j])
            wgu, wd = e_w_gu[j].astype(BF16), e_w_down[j].astype(BF16)
            outs.append(_moe_routed(xl, lat_seg, mod, g2, wr, br, wgu, wd, g3, th=256))
            if with_ctx:
                u, comb = _router(xc, ctx_seg, mod, g2, wr, br)
                outs.append(_moe(u, comb, xc, ctx_seg, mod, wgu, wd, g3, th=256))
        xl = outs[0]
        if with_ctx:
            xc = outs[1]
    return xl.reshape(batch, seq, d)
```

```python
import functools

import jax
import jax.numpy as jnp
from jax import lax
from jax.experimental import pallas as pl
from jax.experimental.pallas import tpu as pltpu

F32 = jnp.float32
BF16 = jnp.bfloat16
EPS = 1e-6
NEG = -1e30

GRID_W = 64
N_MOD = 6
M_HEADS, M_DK, M_DV = 8, 64, 128
H_HEADS, H_DK, H_DV = 8, 128, 128
N_EXPERTS = 8
LANES = 128
SUBLANES = 8
CHUNK = 128
MOD_ROWS = 16
VMEM_LIMIT = 56 * 1024 * 1024

HI = lax.Precision.HIGHEST
NT_DIMS = (((1,), (1,)), ((), ()))
TN_DIMS = (((0,), (0,)), ((), ()))


def _cp(*sem):
    return pltpu.CompilerParams(dimension_semantics=sem, vmem_limit_bytes=VMEM_LIMIT)


def _full(a, nargs):
    zeros = (0,) * a.ndim
    return pl.BlockSpec(a.shape, lambda *_: zeros)


def _sigmoid(x):
    return 1.0 / (1.0 + jnp.exp(-x))


def _silu(x):
    return x * _sigmoid(x)


def _log_sigmoid(x):
    return jnp.minimum(x, 0.0) - jnp.log(1.0 + jnp.exp(-jnp.abs(x)))


def _rms(x):
    return x * lax.rsqrt(jnp.mean(x * x, axis=-1, keepdims=True) + EPS)


def _normmod(x, g, mod, s):
    return _rms(x) * (g * (1.0 + mod[s + 1:s + 2])) + mod[s:s + 1]


def _split2(x):
    hi = x.astype(BF16)
    lo = (x - hi.astype(F32)).astype(BF16)
    return hi, lo


def _cumsum_mat(n, reverse):
    r = lax.broadcasted_iota(jnp.int32, (n, n), 0)
    c = lax.broadcasted_iota(jnp.int32, (n, n), 1)
    return jnp.where((c >= r) if reverse else (c <= r), 1.0, 0.0).astype(BF16)


def _cumsum_time(x, tri):
    hi, lo = _split2(x)
    return (jnp.dot(tri, hi, preferred_element_type=F32)
            + jnp.dot(tri, lo, preferred_element_type=F32))


class _Seg:
    def __init__(self, batch, seqlen, is_ctx):
        self.batch, self.seqlen, self.is_ctx = batch, seqlen, is_ctx
        self.n = batch * seqlen

    def tile(self, pref):
        tm = min(pref, self.n if self.is_ctx else self.seqlen)
        assert self.n % tm == 0 and (self.seqlen % tm == 0 or tm % self.seqlen == 0)
        return tm

    def mod_spec(self, tm, d):
        if self.is_ctx:
            return pl.BlockSpec((1, N_MOD, d), lambda j: (self.batch, 0, 0))
        return pl.BlockSpec((1, N_MOD, d), lambda j: ((j * tm) // self.seqlen, 0, 0))


def _mod_kernel(c_ref, w_ref, b_ref, o_ref):
    c = c_ref[...]
    o_ref[0] = jnp.dot(_silu(c), w_ref[0], preferred_element_type=F32, precision=HI) + b_ref[0]


def _modulation(c_all, mod_w, mod_b):
    depth, d, n = mod_w.shape
    tn = n // 4
    return pl.pallas_call(
        _mod_kernel,
        out_shape=jax.ShapeDtypeStruct((depth, MOD_ROWS, n), F32),
        grid=(depth, n // tn),
        in_specs=[pl.BlockSpec((MOD_ROWS, d), lambda i, j: (0, 0)),
                  pl.BlockSpec((1, d, tn), lambda i, j: (i, 0, j)),
                  pl.BlockSpec((1, 1, tn), lambda i, j: (i, 0, j))],
        out_specs=pl.BlockSpec((1, MOD_ROWS, tn), lambda i, j: (i, 0, j)),
        compiler_params=_cp("arbitrary", "arbitrary"),
        name="modulation",
    )(c_all, mod_w, mod_b.reshape(depth, 1, n))


def _proj_m_kernel(xp_ref, x_ref, xn_ref, mod_ref, g_ref, wqk_ref, wv_ref, wo_ref, wg_ref,
                   cw_ref, gb_ref, gm_ref, q_ref, kt_ref, v_ref, o_ref, gt_ref, *, tm, seqlen):
    j = pl.program_id(0)
    mod = mod_ref[0]
    g = g_ref[...]
    u = _normmod(x_ref[...], g, mod, 0).astype(BF16)
    uh = _normmod(jnp.concatenate([xp_ref[...], xn_ref[...]], axis=0), g, mod, 0).astype(BF16)
    wqk = wqk_ref[...]
    z = jnp.dot(u, wqk, preferred_element_type=F32)
    zh = jnp.dot(uh, wqk, preferred_element_type=F32)
    local = lax.broadcasted_iota(jnp.int32, (tm, 1), 0)
    pos = lax.rem(j * tm + local, seqlen)
    zp = jnp.where(local == 0, zh[SUBLANES - 1:SUBLANES], pltpu.roll(z, 1, 0))
    zn = jnp.where(local == tm - 1, zh[SUBLANES:SUBLANES + 1], pltpu.roll(z, tm - 1, 0))
    zp = jnp.where(pos == 0, 0.0, zp)
    zn = jnp.where(pos == seqlen - 1, 0.0, zn)
    cw = cw_ref[...]
    a = _silu(cw[0:1] * zp + cw[1:2] * z + cw[2:3] * zn)
    half = a.shape[1] // 2
    q_ref[...] = a[:, :half].astype(BF16)
    kt_ref[...] = (a[:, half:] * (M_DK ** -0.5)).T.astype(BF16)
    v_ref[...] = jnp.dot(u, wv_ref[...], preferred_element_type=F32).astype(BF16)
    o_ref[...] = jnp.dot(u, wo_ref[...], preferred_element_type=F32).astype(BF16)
    zg = jnp.dot(u, wg_ref[...], preferred_element_type=F32) + gb_ref[...]
    gt_ref[...] = jnp.where(gm_ref[...] > 0.5, _log_sigmoid(zg), zg)


def _proj_m(x, seg, mod, g, wqk, wv, wo, wg, cw, gb, gm):
    n, d = x.shape
    tm = seg.tile(512)
    nb8 = n // SUBLANES
    r8 = tm // SUBLANES
    kern = functools.partial(_proj_m_kernel, tm=tm, seqlen=seg.seqlen)
    nqk = wqk.shape[1]
    consts = (g, wqk, wv, wo, wg, cw, gb, gm)
    return pl.pallas_call(
        kern,
        out_shape=[jax.ShapeDtypeStruct((n, nqk // 2), BF16),
                   jax.ShapeDtypeStruct((nqk // 2, n), BF16),
                   jax.ShapeDtypeStruct((n, wv.shape[1]), BF16),
                   jax.ShapeDtypeStruct((n, wo.shape[1]), BF16),
                   jax.ShapeDtypeStruct((n, LANES), F32)],
        grid=(n // tm,),
        in_specs=[pl.BlockSpec((SUBLANES, d), lambda j: (jnp.maximum(j * r8 - 1, 0), 0)),
                  pl.BlockSpec((tm, d), lambda j: (j, 0)),
                  pl.BlockSpec((SUBLANES, d), lambda j: (jnp.minimum((j + 1) * r8, nb8 - 1), 0)),
                  seg.mod_spec(tm, d)] + [_full(a, 1) for a in consts],
        out_specs=[pl.BlockSpec((tm, nqk // 2), lambda j: (j, 0)),
                   pl.BlockSpec((nqk // 2, tm), lambda j: (0, j)),
                   pl.BlockSpec((tm, wv.shape[1]), lambda j: (j, 0)),
                   pl.BlockSpec((tm, wo.shape[1]), lambda j: (j, 0)),
                   pl.BlockSpec((tm, LANES), lambda j: (j, 0))],
        compiler_params=_cp("arbitrary"),
        name="mlstm_proj",
    )(x, x, x, mod, *consts)


MLSTM_PAIR_GROUP = 4


def _mlstm_gates(g_ref, m_prev, reverse, d):
    L = CHUNK
    g = g_ref[...]
    b = _cumsum_time(g, _cumsum_mat(L, reverse))
    gT, bT = g.T, b.T
    c_ig = 2 * M_HEADS * d
    b_rows = bT[c_ig + M_HEADS:c_ig + 2 * M_HEADS, :]
    beta = gT[c_ig:c_ig + M_HEADS, :] - b_rows
    lane = lax.broadcasted_iota(jnp.int32, (1, L), 1)
    cm = beta
    k = 1
    while k < L:
        if reverse:
            cm = jnp.maximum(cm, jnp.where(lane < L - k, pltpu.roll(cm, L - k, 1), NEG))
        else:
            cm = jnp.maximum(cm, jnp.where(lane >= k, pltpu.roll(cm, k, 1), NEG))
        k *= 2
    last = 0 if reverse else L - 1
    mu = jnp.maximum(cm, m_prev)
    mu_last = mu[:, last:last + 1]
    rows = jnp.concatenate([beta, jnp.exp(beta - mu_last)], axis=0)
    cols = jnp.concatenate([mu, jnp.exp(m_prev - mu), jnp.exp(-(b_rows + mu)),
                            jnp.zeros((LANES - 3 * M_HEADS, L), F32)], axis=0).T
    half = lax.broadcasted_iota(jnp.int32, (1, LANES), 1) < LANES // 2
    small = jnp.where(half, jnp.exp(m_prev - mu_last), b_rows[:, last:last + 1] + mu_last)
    return rows, cols, small


def _mlstm_heads(q_ref, kt_ref, v_ref, rows_g, cols, small, h_ref, s_scr, reverse, d, pairs):
    L = CHUNK
    r = lax.broadcasted_iota(jnp.int32, (L, L), 0)
    c = lax.broadcasted_iota(jnp.int32, (L, L), 1)
    causal = (c >= r) if reverse else (c <= r)
    lane = lax.broadcasted_iota(jnp.int32, (1, LANES), 1)
    ones = jnp.ones((L, M_DV), BF16)
    staged = []
    for p in pairs:
        q_p = q_ref[:, p * LANES:(p + 1) * LANES]
        kt_p = kt_ref[p * LANES:(p + 1) * LANES, :]
        for e in range(2):
            h = 2 * p + e
            q_e = jnp.where((lane // M_DK) == e, q_p, jnp.zeros_like(q_p))
            qk = jnp.dot(q_e, kt_p, preferred_element_type=F32)
            s = jnp.where(causal, qk * jnp.exp(rows_g[h:h + 1, :] - cols[:, h:h + 1]), 0.0).astype(BF16)
            ktw = (kt_p[e * M_DK:(e + 1) * M_DK, :].astype(F32)
                   * rows_g[M_HEADS + h:M_HEADS + h + 1, :]).astype(BF16)
            staged.append((p, e, q_e, s, ktw))
    for p, e, q_e, s, ktw in staged:
        h = 2 * p + e
        rows = slice(e * M_DK, (e + 1) * M_DK)
        vp = jnp.concatenate([v_ref[:, h * M_DV:(h + 1) * M_DV], ones], axis=1)
        w_inter = cols[:, M_HEADS + h:M_HEADS + h + 1]
        floor = cols[:, 2 * M_HEADS + h:2 * M_HEADS + h + 1]
        tot = (jnp.dot(s, vp, preferred_element_type=F32)
               + w_inter * jnp.dot(q_e, s_scr[d, p].astype(BF16), preferred_element_type=F32))
        den = jnp.maximum(jnp.abs(tot[:, M_DV:]), floor)
        h_ref[:, h * M_DV:(h + 1) * M_DV] = (tot[:, :M_DV] / den).astype(BF16)
        s_scr[d, p, rows, :] = (small[h:h + 1, 0:1] * s_scr[d, p, rows, :]
                                + jnp.dot(ktw, vp, preferred_element_type=F32))


def _mlstm_scan_kernel(qf, kf, vf, gf, gf_next, qb, kb, vb, gb, gb_next, s0_ref, m0_ref,
                       hf_ref, hb_ref, st_ref, mt_ref, s_scr, m_scr, rows_scr, cols_scr, small_scr):
    n = pl.program_id(1)
    H = M_HEADS
    g_cur, g_next = (gf, gb), (gf_next, gb_next)

    @pl.when(n == 0)
    def _():
        s_scr[...] = s0_ref[0]
        for d in range(2):
            rows_scr[d], cols_scr[d], small_scr[d] = _mlstm_gates(
                g_cur[d], m0_ref[0, d * H:(d + 1) * H, 0:1], d == 1, d)

    cur = [(rows_scr[d], cols_scr[d], small_scr[d]) for d in range(2)]
    m_new = [cur[d][2][:, LANES // 2:LANES // 2 + 1] for d in range(2)]
    m_scr[...] = jnp.broadcast_to(jnp.concatenate(m_new, axis=0), m_scr.shape)
    for d in range(2):
        rows_scr[d], cols_scr[d], small_scr[d] = _mlstm_gates(g_next[d], m_new[d], d == 1, d)
    for p0 in range(0, M_HEADS // 2, MLSTM_PAIR_GROUP):
        pairs = range(p0, p0 + MLSTM_PAIR_GROUP)
        _mlstm_heads(qf, kf, vf, *cur[0], hf_ref, s_scr, False, 0, pairs)
        _mlstm_heads(qb, kb, vb, *cur[1], hb_ref, s_scr, True, 1, pairs)

    @pl.when(n == pl.num_programs(1) - 1)
    def _():
        st_ref[0] = s_scr[...]
        mt_ref[0] = m_scr[...]


def _mlstm_scan(q, kt, v, gates, s0, m0, seg):
    L = CHUNK
    assert seg.seqlen % L == 0
    n = seg.seqlen // L
    fwd = lambda b, i: (b * n + i, 0)
    bwd = lambda b, i: (b * n + (n - 1 - i), 0)
    fwd_next = lambda b, i: (b * n + jnp.minimum(i + 1, n - 1), 0)
    bwd_next = lambda b, i: (b * n + jnp.maximum(n - 2 - i, 0), 0)
    fwd_t = lambda b, i: (0, b * n + i)
    bwd_t = lambda b, i: (0, b * n + (n - 1 - i))
    dq, dv = q.shape[1], v.shape[1]
    st = lambda b, i: (b, 0, 0, 0, 0)
    mt = lambda b, i: (b, 0, 0)
    return pl.pallas_call(
        _mlstm_scan_kernel,
        out_shape=[jax.ShapeDtypeStruct((seg.n, dv), BF16), jax.ShapeDtypeStruct((seg.n, dv), BF16),
                   jax.ShapeDtypeStruct(s0.shape, F32), jax.ShapeDtypeStruct(m0.shape, F32)],
        grid=(seg.batch, n),
        in_specs=[pl.BlockSpec((L, dq), fwd), pl.BlockSpec((dq, L), fwd_t), pl.BlockSpec((L, dv), fwd),
                  pl.BlockSpec((L, LANES), fwd), pl.BlockSpec((L, LANES), fwd_next),
                  pl.BlockSpec((L, dq), bwd), pl.BlockSpec((dq, L), bwd_t), pl.BlockSpec((L, dv), bwd),
                  pl.BlockSpec((L, LANES), bwd), pl.BlockSpec((L, LANES), bwd_next),
                  pl.BlockSpec((1,) + s0.shape[1:], st), pl.BlockSpec((1,) + m0.shape[1:], mt)],
        out_specs=[pl.BlockSpec((L, dv), fwd), pl.BlockSpec((L, dv), bwd),
                   pl.BlockSpec((1,) + s0.shape[1:], st), pl.BlockSpec((1,) + m0.shape[1:], mt)],
        scratch_shapes=[pltpu.VMEM(s0.shape[1:], F32), pltpu.VMEM(m0.shape[1:], F32),
                        pltpu.VMEM((2, 2 * M_HEADS, L), F32), pltpu.VMEM((2, L, LANES), F32),
                        pltpu.VMEM((2, M_HEADS, LANES), F32)],
        compiler_params=_cp("arbitrary", "arbitrary"),
        name="mlstm_scan",
    )(q, kt, v, gates, gates, q, kt, v, gates, gates, s0, m0)


def _head_out(hf, hb, gate_pre, gh, w, n_heads):
    h = hf.astype(F32) + hb.astype(F32)
    dv = h.shape[1] // n_heads
    parts = [_rms(h[:, i * dv:(i + 1) * dv]) for i in range(n_heads)]
    hn = jnp.concatenate(parts, axis=1) * gh * _sigmoid(gate_pre.astype(F32))
    return jnp.dot(hn.astype(BF16), w, preferred_element_type=F32)


def _mix_out_kernel(hf_ref, hb_ref, o_ref, x_ref, mod_ref, gh_ref, w_ref, g1_ref, out_ref, *, n_heads):
    mod = mod_ref[0]
    y = _head_out(hf_ref[...], hb_ref[...], o_ref[...], gh_ref[...], w_ref[...], n_heads)
    out_ref[...] = x_ref[...] + mod[2:3] * (_rms(y) * g1_ref[...])


def _mix_out(hf, hb, o, x, seg, mod, gh, w, g1, n_heads, in_place=True):
    n, d = x.shape
    tm = seg.tile(512)
    rmap = lambda j: (j, 0)
    consts = (gh, w, g1)
    return pl.pallas_call(
        functools.partial(_mix_out_kernel, n_heads=n_heads),
        out_shape=jax.ShapeDtypeStruct(x.shape, F32),
        grid=(n // tm,),
        in_specs=[pl.BlockSpec((tm, hf.shape[1]), rmap), pl.BlockSpec((tm, hf.shape[1]), rmap),
                  pl.BlockSpec((tm, o.shape[1]), rmap), pl.BlockSpec((tm, d), rmap),
                  seg.mod_spec(tm, d)] + [_full(a, 1) for a in consts],
        out_specs=pl.BlockSpec((tm, d), rmap),
        input_output_aliases={3: 0} if in_place else {},
        compiler_params=_cp("arbitrary"),
        name="mix_out",
    )(hf, hb, o, x, mod, *consts)


def _swiglu_acc(u_scr, wgu_ref, wd_ref, acc_scr, th):
    hd = wd_ref.shape[0]
    assert hd % th == 0 and hd % LANES == 0

    def chunk(k, carry):
        off = pl.multiple_of(k * th, th)
        u = u_scr[...]
        gg = jnp.dot(u, wgu_ref[:, pl.ds(off, th)], preferred_element_type=F32)
        up = jnp.dot(u, wgu_ref[:, pl.ds(pl.multiple_of(hd + off, LANES), th)], preferred_element_type=F32)
        hh = (_silu(gg) * up).astype(BF16)
        acc_scr[...] += jnp.dot(hh, wd_ref[pl.ds(off, th), :], preferred_element_type=F32)
        return carry

    acc_scr[...] = jnp.zeros_like(acc_scr)
    lax.fori_loop(0, hd // th, chunk, 0)


def _ffn_kernel(x_ref, mod_ref, g2_ref, wgu_ref, wd_ref, g3_ref, out_ref, u_scr, acc_scr, *, th):
    u_scr[...] = _normmod(x_ref[...], g2_ref[...], mod_ref[0], 3).astype(BF16)
    _swiglu_acc(u_scr, wgu_ref, wd_ref, acc_scr, th)
    mod = mod_ref[0]
    out_ref[...] = x_ref[...] + mod[5:6] * (_rms(acc_scr[...]) * g3_ref[...])


def _resident(a, index_map):
    return pl.BlockSpec(a, index_map, pipeline_mode=pl.Buffered(1))


def _ffn(x, seg, mod, g2, wgu, wd, g3, th):
    n, d = x.shape
    tm = seg.tile(1024)
    return pl.pallas_call(
        functools.partial(_ffn_kernel, th=th),
        out_shape=jax.ShapeDtypeStruct((n, d), F32),
        grid=(n // tm,),
        in_specs=[pl.BlockSpec((tm, d), lambda i: (i, 0)),
                  seg.mod_spec(tm, d),
                  _full(g2, 1),
                  _resident(wgu.shape, lambda i: (0, 0)),
                  _resident(wd.shape, lambda i: (0, 0)),
                  _full(g3, 1)],
        out_specs=pl.BlockSpec((tm, d), lambda i: (i, 0)),
        scratch_shapes=[pltpu.VMEM((tm, d), BF16), pltpu.VMEM((tm, d), F32)],
        input_output_aliases={0: 0},
        compiler_params=_cp("arbitrary"),
        name="ffn",
    )(x, mod, g2, wgu, wd, g3)


META_E, META_RANK, META_W = 0, 2, 4


def _route_kernel(x_ref, mod_ref, g2_ref, wr_ref, br_ref, meta_ref, cnt_ref, tri_scr, run_scr):
    i = pl.program_id(0)
    tm = x_ref.shape[0]

    @pl.when(i == 0)
    def _():
        r = lax.broadcasted_iota(jnp.int32, (tm, tm), 0)
        c = lax.broadcasted_iota(jnp.int32, (tm, tm), 1)
        tri_scr[...] = jnp.where(c < r, 1.0, 0.0).astype(BF16)
        run_scr[...] = jnp.zeros_like(run_scr)

    u = _normmod(x_ref[...], g2_ref[...], mod_ref[0], 3)
    lane = lax.broadcasted_iota(jnp.int32, (1, LANES), 1)
    u_hi, u_lo = _split2(u)
    w_hi, w_lo = _split2(wr_ref[...])
    logits = (jnp.dot(u_hi, w_hi, preferred_element_type=F32) + jnp.dot(u_lo, w_hi, preferred_element_type=F32)
              + jnp.dot(u_hi, w_lo, preferred_element_type=F32)) + br_ref[...]
    logits = jnp.where(lane < N_EXPERTS, logits, NEG)
    m1 = jnp.max(logits, axis=1, keepdims=True)
    i1 = jnp.min(jnp.where(logits == m1, lane, LANES), axis=1, keepdims=True)
    rest = jnp.where(lane == i1, NEG, logits)
    m2 = jnp.max(rest, axis=1, keepdims=True)
    i2 = jnp.min(jnp.where(rest == m2, lane, LANES), axis=1, keepdims=True)
    e2 = jnp.exp(m2 - m1)
    w1 = 1.0 / (1.0 + e2)
    sel = ((lane == i1) | (lane == i2))
    onehot = jnp.where(sel, 1.0, 0.0)
    before = jnp.dot(tri_scr[...], onehot.astype(BF16), preferred_element_type=F32) + run_scr[...]
    rank1 = jnp.sum(jnp.where(lane == i1, before, 0.0), axis=1, keepdims=True)
    rank2 = jnp.sum(jnp.where(lane == i2, before, 0.0), axis=1, keepdims=True)
    run_scr[...] = run_scr[...] + jnp.sum(onehot, axis=0, keepdims=True)
    vals = (i1.astype(F32), i2.astype(F32), rank1, rank2, w1, e2 * w1)
    meta = jnp.zeros((tm, LANES), F32)
    for col, val in enumerate(vals):
        meta = jnp.where(lane == col, val, meta)
    meta_ref[...] = meta
    cnt_ref[...] = jnp.broadcast_to(run_scr[...], cnt_ref.shape)


def _route(x, seg, mod, g2, wr, br):
    n, d = x.shape
    tm = seg.tile(1024)
    consts = (g2, wr, br)
    return pl.pallas_call(
        _route_kernel,
        out_shape=[jax.ShapeDtypeStruct((n, LANES), F32),
                   jax.ShapeDtypeStruct((SUBLANES, LANES), F32)],
        grid=(n // tm,),
        in_specs=[pl.BlockSpec((tm, d), lambda i: (i, 0)), seg.mod_spec(tm, d)]
                 + [_full(a, 1) for a in consts],
        out_specs=[pl.BlockSpec((tm, LANES), lambda i: (i, 0)),
                   pl.BlockSpec((SUBLANES, LANES), lambda i: (0, 0))],
        scratch_shapes=[pltpu.VMEM((tm, tm), BF16), pltpu.VMEM((1, LANES), F32)],
        compiler_params=_cp("arbitrary"),
        name="route",
    )(x, mod, *consts)


def _drain(src_ref, dst_ref, sem, count):
    def body(i, carry):
        pltpu.make_async_copy(src_ref, dst_ref, sem).wait()
        return carry

    lax.fori_loop(0, count, body, 0, unroll=8)


def _row_tile(buf, t):
    return buf.at[pl.ds(pl.multiple_of(t * SUBLANES, SUBLANES), SUBLANES), :]


def _dispatch_kernel(dst_ref, pad_ref, x_ref, mod_ref, g2_ref, xs_ref, ubuf, sems):
    j = pl.program_id(0)
    tm = x_ref.shape[0]
    slot = lax.rem(j, 2)
    buf = ubuf.at[slot]
    sem = sems.at[slot]
    u = _normmod(x_ref[...], g2_ref[...], mod_ref[0], 3)
    for s in range(SUBLANES):
        ubuf[slot, pl.ds(s, tm, stride=SUBLANES), :] = u[:, s * LANES:(s + 1) * LANES]

    def issue(t, carry):
        pltpu.make_async_copy(_row_tile(buf, t), xs_ref.at[dst_ref[0, 0, t]], sem).start(priority=0)
        pltpu.make_async_copy(_row_tile(buf, t), xs_ref.at[dst_ref[0, 0, tm + t]], sem).start(priority=1)
        return carry

    lax.fori_loop(0, tm, issue, 0, unroll=8)
    npad = pad_ref.shape[2]
    first = _row_tile(buf, 0)

    @pl.when(j == 0)
    def _():
        def issue_pad(p, carry):
            pltpu.make_async_copy(first, xs_ref.at[pad_ref[0, 0, p]], sem).start()
            return carry

        lax.fori_loop(0, npad, issue_pad, 0, unroll=8)
        _drain(first, xs_ref.at[0], sem, npad)

    @pl.when(j > 0)
    def _():
        _drain(_row_tile(ubuf.at[1 - slot], 0), xs_ref.at[0], sems.at[1 - slot], 2 * tm)

    @pl.when(j == pl.num_programs(0) - 1)
    def _():
        _drain(first, xs_ref.at[0], sem, 2 * tm)


def _dispatch(x, seg, mod, g2, dest, pad_dst, n_rows):
    n, d = x.shape
    tm = seg.tile(512)
    steps = n // tm
    dst = dest.reshape(steps, tm, 2).transpose(0, 2, 1).reshape(steps, 1, 2 * tm)
    npad = pad_dst.shape[0]
    return pl.pallas_call(
        _dispatch_kernel,
        out_shape=jax.ShapeDtypeStruct((n_rows, SUBLANES, d // SUBLANES), F32),
        grid=(steps,),
        in_specs=[pl.BlockSpec((1, 1, 2 * tm), lambda j: (j, 0, 0), memory_space=pltpu.SMEM),
                  pl.BlockSpec((1, 1, npad), lambda j: (0, 0, 0), memory_space=pltpu.SMEM),
                  pl.BlockSpec((tm, d), lambda j: (j, 0)), seg.mod_spec(tm, d), _full(g2, 1)],
        out_specs=pl.BlockSpec(memory_space=pl.ANY),
        scratch_shapes=[pltpu.VMEM((2, tm * SUBLANES, d // SUBLANES), F32), pltpu.SemaphoreType.DMA((2,))],
        compiler_params=_cp("arbitrary"),
        name="dispatch",
    )(dst, pad_dst.reshape(1, 1, npad), x, mod, g2)


def _experts_kernel(te_ref, used_ref, xs_ref, wgu_ref, wd_ref, y_ref, u_scr, acc_scr, *, th):
    i = pl.program_id(0)

    @pl.when(i < used_ref[0])
    def _():
        tr = u_scr.shape[0]
        for s in range(SUBLANES):
            u_scr[:, s * LANES:(s + 1) * LANES] = xs_ref[pl.ds(s, tr, stride=SUBLANES), :].astype(BF16)
        _swiglu_acc(u_scr, wgu_ref.at[0], wd_ref.at[0], acc_scr, th)
        y = acc_scr[...]
        for s in range(SUBLANES):
            y_ref[pl.ds(s, tr, stride=SUBLANES), :] = y[:, s * LANES:(s + 1) * LANES]


def _experts(xs3, tile_expert, used, wgu, wd, th, tr):
    n_rows = xs3.shape[0]
    ne, hd, d = wd.shape
    tiles = tile_expert.shape[0]
    assert tiles * tr <= n_rows
    row = lambda i, te, us: (jnp.minimum(i, us[0] - 1), 0)
    grid_spec = pltpu.PrefetchScalarGridSpec(
        num_scalar_prefetch=2,
        grid=(tiles,),
        in_specs=[pl.BlockSpec((tr * SUBLANES, LANES), row),
                  _resident((1,) + wgu.shape[1:], lambda i, te, us: (te[i], 0, 0)),
                  _resident((1,) + wd.shape[1:], lambda i, te, us: (te[i], 0, 0))],
        out_specs=pl.BlockSpec((tr * SUBLANES, LANES), row),
        scratch_shapes=[pltpu.VMEM((tr, d), BF16), pltpu.VMEM((tr, d), F32)],
    )
    y = pl.pallas_call(
        functools.partial(_experts_kernel, th=th),
        out_shape=jax.ShapeDtypeStruct((tiles * tr * SUBLANES, LANES), F32),
        grid_spec=grid_spec,
        compiler_params=_cp("arbitrary"),
        name="experts",
    )(tile_expert, used, xs3.reshape(n_rows * SUBLANES, LANES), wgu, wd)
    return y.reshape(tiles * tr, SUBLANES, LANES)


def _combine_kernel(src_ref, nxt_ref, y_ref, meta_ref, x_ref, mod_ref, g3_ref, out_ref, ybuf, sems):
    j = pl.program_id(0)
    tm = x_ref.shape[0]
    slot = lax.rem(j, 2)

    def gather(idx_ref, sl):
        def issue(t, carry):
            pltpu.make_async_copy(y_ref.at[idx_ref[0, 0, t]], _row_tile(ybuf.at[sl, 0], t),
                                  sems.at[sl]).start(priority=0)
            pltpu.make_async_copy(y_ref.at[idx_ref[0, 0, tm + t]], _row_tile(ybuf.at[sl, 1], t),
                                  sems.at[sl]).start(priority=1)
            return carry

        lax.fori_loop(0, tm, issue, 0, unroll=8)

    @pl.when(j == 0)
    def _():
        gather(src_ref, 0)

    @pl.when(j + 1 < pl.num_programs(0))
    def _():
        gather(nxt_ref, 1 - slot)

    _drain(y_ref.at[0], _row_tile(ybuf.at[slot, 0], 0), sems.at[slot], 2 * tm)
    meta = meta_ref[...]
    w1 = meta[:, META_W:META_W + 1]
    w2 = meta[:, META_W + 1:META_W + 2]
    ys = [w1 * ybuf[slot, 0, pl.ds(s, tm, stride=SUBLANES), :]
          + w2 * ybuf[slot, 1, pl.ds(s, tm, stride=SUBLANES), :] for s in range(SUBLANES)]
    ss = ys[0] * ys[0]
    for y in ys[1:]:
        ss = ss + y * y
    d = x_ref.shape[1]
    r = lax.rsqrt(jnp.sum(ss, axis=1, keepdims=True) / d + EPS)
    mod = mod_ref[0]
    for s, y in enumerate(ys):
        sl = slice(s * LANES, (s + 1) * LANES)
        out_ref[:, sl] = x_ref[:, sl] + mod[5:6, sl] * (y * r * g3_ref[:, sl])


def _combine(y3, dest, meta, x, seg, mod, g3):
    n, d = x.shape
    tm = seg.tile(512)
    steps = n // tm
    src = dest.reshape(steps, tm, 2).transpose(0, 2, 1).reshape(steps, 1, 2 * tm)
    return pl.pallas_call(
        _combine_kernel,
        out_shape=jax.ShapeDtypeStruct((n, d), F32),
        grid=(steps,),
        in_specs=[pl.BlockSpec((1, 1, 2 * tm), lambda j: (j, 0, 0), memory_space=pltpu.SMEM),
                  pl.BlockSpec((1, 1, 2 * tm), lambda j: (jnp.minimum(j + 1, steps - 1), 0, 0),
                               memory_space=pltpu.SMEM),
                  pl.BlockSpec(memory_space=pl.ANY),
                  pl.BlockSpec((tm, LANES), lambda j: (j, 0)),
                  pl.BlockSpec((tm, d), lambda j: (j, 0)),
                  seg.mod_spec(tm, d), _full(g3, 1)],
        out_specs=pl.BlockSpec((tm, d), lambda j: (j, 0)),
        scratch_shapes=[pltpu.VMEM((2, 2, tm * SUBLANES, d // SUBLANES), F32), pltpu.SemaphoreType.DMA((2,))],
        input_output_aliases={4: 0},
        compiler_params=_cp("arbitrary"),
        name="combine",
    )(src, src, y3, meta, x, mod, g3)


def _moe_routed(x, seg, mod, g2, wr, br, wgu, wd, g3, th):
    n, d = x.shape
    assert d == SUBLANES * LANES
    tr = max(256, min(1024, pl.next_power_of_2((2 * n) // N_EXPERTS) // 2))
    meta, cnt = _route(x, seg, mod, g2, wr, br)
    counts = cnt[0, :N_EXPERTS].astype(jnp.int32)
    sizes = ((counts + tr - 1) // tr) * tr
    ends = jnp.cumsum(sizes)
    offs = ends - sizes
    mi = meta[:, :META_W].astype(jnp.int32)
    eidx = mi[:, META_E:META_E + 2]
    dest = jnp.sum(jnp.where(eidx[:, :, None] == jnp.arange(N_EXPERTS)[None, None, :],
                             offs[None, None, :], 0), axis=2) + mi[:, META_RANK:META_RANK + 2]
    dest = dest.astype(jnp.int32)
    tiles = (2 * n) // tr + N_EXPERTS
    n_rows = tiles * tr
    used = (ends[-1] // tr).astype(jnp.int32).reshape(1)
    tile_start = jnp.arange(tiles, dtype=jnp.int32) * tr
    tile_expert = jnp.sum(tile_start[:, None] >= ends[None, :], axis=1).astype(jnp.int32)
    last_expert = jnp.sum(jnp.maximum(ends[-1] - tr, 0) >= ends).astype(jnp.int32)
    tile_expert = jnp.where(tile_start < ends[-1], tile_expert, last_expert)
    pad = jnp.arange(N_EXPERTS * tr, dtype=jnp.int32).reshape(N_EXPERTS, tr)
    pad_row = offs[:, None] + counts[:, None] + jnp.arange(tr, dtype=jnp.int32)[None, :]
    pad_dst = jnp.where(pad_row < ends[:, None], pad_row, n_rows + pad).reshape(-1).astype(jnp.int32)
    xs3 = _dispatch(x, seg, mod, g2, dest, pad_dst, n_rows + N_EXPERTS * tr)
    y3 = _experts(xs3, tile_expert, used, wgu, wd, th, tr)
    return _combine(y3, dest, meta, x, seg, mod, g3)


def _proj_h_body(u, w_refs, out_refs):
    for w_ref, o_ref in zip(w_refs, out_refs):
        o_ref[...] = jnp.dot(u, w_ref[...], preferred_element_type=F32).astype(BF16)


def _proj_h_lat_kernel(x_ref, mod_ref, g_ref, *refs, rows_per_col):
    n = len(refs) // 2
    w_refs, out_refs = refs[:n], refs[n:]
    x = jnp.swapaxes(x_ref[...], 0, 1)
    x = x.reshape(SUBLANES * rows_per_col, x.shape[2])
    u = _normmod(x, g_ref[...], mod_ref[0], 0).astype(BF16)
    _proj_h_body(u, w_refs, out_refs)


def _proj_h_ctx_kernel(x_ref, mod_ref, g_ref, *refs):
    n = len(refs) // 2
    u = _normmod(x_ref[...], g_ref[...], mod_ref[0], 0).astype(BF16)
    _proj_h_body(u, refs[:n], refs[n:])


def _proj_h_lat(x, seg, mod, g, ws):
    n, d = x.shape
    grid_rows = seg.seqlen // GRID_W
    x3 = x.reshape(n // GRID_W, GRID_W, d)
    tcol = SUBLANES * grid_rows
    ncb = GRID_W // SUBLANES
    return pl.pallas_call(
        functools.partial(_proj_h_lat_kernel, rows_per_col=grid_rows),
        out_shape=[jax.ShapeDtypeStruct((n, w.shape[1]), BF16) for w in ws],
        grid=(seg.batch, ncb),
        in_specs=[pl.BlockSpec((grid_rows, SUBLANES, d), lambda b, c: (b, c, 0)),
                  pl.BlockSpec((1, N_MOD, d), lambda b, c: (b, 0, 0)), _full(g, 2)]
                 + [_full(w, 2) for w in ws],
        out_specs=[pl.BlockSpec((tcol, w.shape[1]), lambda b, c: (b * ncb + c, 0)) for w in ws],
        compiler_params=_cp("arbitrary", "arbitrary"),
        name="hgrn_proj_lat",
    )(x3, mod, g, *ws)


def _proj_h_ctx(x, seg, mod, g, ws):
    n, d = x.shape
    tm = seg.tile(512)
    return pl.pallas_call(
        _proj_h_ctx_kernel,
        out_shape=[jax.ShapeDtypeStruct((n, w.shape[1]), BF16) for w in ws],
        grid=(n // tm,),
        in_specs=[pl.BlockSpec((tm, d), lambda j: (j, 0)), seg.mod_spec(tm, d), _full(g, 1)]
                 + [_full(w, 1) for w in ws],
        out_specs=[pl.BlockSpec((tm, w.shape[1]), lambda j: (j, 0)) for w in ws],
        compiler_params=_cp("arbitrary"),
        name="hgrn_proj_ctx",
    )(x, mod, g, *ws)


HGRN_HEAD_GROUP = 4


def _hgrn_prologue(f_ref, bias_ref, lb_ref, reverse, d):
    lb = lb_ref[d:d + 1, :]
    f = lb + (1.0 - lb) * _sigmoid(f_ref[...].astype(F32) + bias_ref[d:d + 1, :])
    return f, _cumsum_time(jnp.log(f), _cumsum_mat(CHUNK, reverse))


def _hgrn_heads(q_ref, v_ref, pro, h_ref, s_scr, reverse, d, heads):
    L = CHUNK
    f, a_all = pro
    r = lax.broadcasted_iota(jnp.int32, (L, L), 0)
    c = lax.broadcasted_iota(jnp.int32, (L, L), 1)
    causal = (c >= r) if reverse else (c <= r)
    same64 = (r // 64) == (c // 64)
    diag32 = ((r // 32) == (c // 32)) & causal
    t = lax.broadcasted_iota(jnp.int32, (L, 1), 0)
    if reverse:
        late0, late1 = t < 64, (t % 64) < 32
        a0, a1 = (64,), (32, 96)
        a2 = (31, 63, 95, 127)
        last = 0
    else:
        late0, late1 = t >= 64, (t % 64) >= 32
        a0, a1 = (63,), (31, 95)
        a2 = (0, 32, 64, 96)
        last = L - 1
    staged = []
    for h in heads:
        sl = slice(h * H_DK, (h + 1) * H_DK)
        a = a_all[:, sl]
        q = q_ref[:, sl].astype(F32)
        k = 1.0 - f[:, sl]
        anc0 = jnp.broadcast_to(a[a0[0]:a0[0] + 1, :], (L, H_DK))
        anc1 = jnp.where(t < 64, a[a1[0]:a1[0] + 1, :], a[a1[1]:a1[1] + 1, :])
        anc2 = jnp.where(t < 32, a[a2[0]:a2[0] + 1, :],
                         jnp.where(t < 64, a[a2[1]:a2[1] + 1, :],
                                   jnp.where(t < 96, a[a2[2]:a2[2] + 1, :], a[a2[3]:a2[3] + 1, :])))

        def level(anc, late):
            qs = q * jnp.exp(a - anc)
            ks = k * jnp.exp(anc - a)
            if late is not None:
                qs = jnp.where(late, qs, 0.0)
                ks = jnp.where(late, 0.0, ks)
            return lax.dot_general(qs.astype(BF16), ks.astype(BF16), NT_DIMS, preferred_element_type=F32)

        s = jnp.where(diag32, level(anc2, None), jnp.where(same64, level(anc1, late1), level(anc0, late0)))
        qd = (q * jnp.exp(a)).astype(BF16)
        a_last = a[last:last + 1, :]
        kd = (k * jnp.exp(a_last - a)).astype(BF16)
        staged.append((s.astype(BF16), qd, kd, jnp.exp(a_last)))
    for h, (s, qd, kd, dec) in zip(heads, staged):
        v = v_ref[:, h * H_DV:(h + 1) * H_DV]
        st = s_scr[d, h]
        o = (jnp.dot(s, v, preferred_element_type=F32)
             + lax.dot_general(qd, st.astype(BF16), NT_DIMS, preferred_element_type=F32))
        h_ref[:, h * H_DV:(h + 1) * H_DV] = o.astype(BF16)
        s_scr[d, h] = st * dec + lax.dot_general(v, kd, TN_DIMS, preferred_element_type=F32)


def _hgrn_scan_kernel(qf, vf, ff, qb, vb, fb, bias_ref, lb_ref, s0_ref, hf_ref, hb_ref, st_ref, s_scr):
    n = pl.program_id(1)

    @pl.when(n == 0)
    def _():
        s_scr[...] = s0_ref[0]

    pro_f = _hgrn_prologue(ff, bias_ref, lb_ref, False, 0)
    pro_b = _hgrn_prologue(fb, bias_ref, lb_ref, True, 1)
    for h0 in range(0, H_HEADS, HGRN_HEAD_GROUP):
        heads = range(h0, h0 + HGRN_HEAD_GROUP)
        _hgrn_heads(qf, vf, pro_f, hf_ref, s_scr, False, 0, heads)
        _hgrn_heads(qb, vb, pro_b, hb_ref, s_scr, True, 1, heads)

    @pl.when(n == pl.num_programs(1) - 1)
    def _():
        st_ref[0] = s_scr[...]


def _hgrn_scan(q, v, f_f, f_b, bias, lb, s0, seg):
    L = CHUNK
    assert seg.seqlen % L == 0
    n = seg.seqlen // L
    fwd = lambda b, i: (b * n + i, 0)
    bwd = lambda b, i: (b * n + (n - 1 - i), 0)
    rows_, w = q.shape
    st = lambda b, i: (b, 0, 0, 0, 0)
    return pl.pallas_call(
        _hgrn_scan_kernel,
        out_shape=[jax.ShapeDtypeStruct((rows_, w), BF16), jax.ShapeDtypeStruct((rows_, w), BF16),
                   jax.ShapeDtypeStruct(s0.shape, F32)],
        grid=(seg.batch, n),
        in_specs=[pl.BlockSpec((L, w), fwd), pl.BlockSpec((L, w), fwd), pl.BlockSpec((L, w), fwd),
                  pl.BlockSpec((L, w), bwd), pl.BlockSpec((L, w), bwd), pl.BlockSpec((L, w), bwd),
                  _full(bias, 2), _full(lb, 2), pl.BlockSpec((1,) + s0.shape[1:], st)],
        out_specs=[pl.BlockSpec((L, w), fwd), pl.BlockSpec((L, w), bwd),
                   pl.BlockSpec((1,) + s0.shape[1:], st)],
        scratch_shapes=[pltpu.VMEM(s0.shape[1:], F32)],
        compiler_params=_cp("arbitrary", "arbitrary"),
        name="hgrn_scan",
    )(q, v, f_f, q, v, f_b, bias, lb, s0)


def _mix_out_col_kernel(hf_ref, hb_ref, o_ref, x_ref, mod_ref, gh_ref, w_ref, g1_ref, out_ref,
                        *, n_heads, rows_per_col):
    mod = mod_ref[0]
    y = _head_out(hf_ref[...], hb_ref[...], o_ref[...], gh_ref[...], w_ref[...], n_heads)
    upd = mod[2:3] * (_rms(y) * g1_ref[...])
    upd = jnp.swapaxes(upd.reshape(SUBLANES, rows_per_col, upd.shape[1]), 0, 1)
    out_ref[...] = x_ref[...] + upd


def _mix_out_col(hf, hb, o, x, seg, mod, gh, w, g1, n_heads):
    n, d = x.shape
    grid_rows = seg.seqlen // GRID_W
    tcol = SUBLANES * grid_rows
    ncb = GRID_W // SUBLANES
    x3 = x.reshape(n // GRID_W, GRID_W, d)
    hmap = lambda b, c: (b * ncb + c, 0)
    xmap = lambda b, c: (b, c, 0)
    consts = (gh, w, g1)
    out = pl.pallas_call(
        functools.partial(_mix_out_col_kernel, n_heads=n_heads, rows_per_col=grid_rows),
        out_shape=jax.ShapeDtypeStruct(x3.shape, F32),
        grid=(seg.batch, ncb),
        in_specs=[pl.BlockSpec((tcol, hf.shape[1]), hmap), pl.BlockSpec((tcol, hf.shape[1]), hmap),
                  pl.BlockSpec((tcol, o.shape[1]), hmap),
                  pl.BlockSpec((grid_rows, SUBLANES, d), xmap),
                  pl.BlockSpec((1, N_MOD, d), lambda b, c: (b, 0, 0))] + [_full(a, 2) for a in consts],
        out_specs=pl.BlockSpec((grid_rows, SUBLANES, d), xmap),
        input_output_aliases={3: 0},
        compiler_params=_cp("arbitrary", "arbitrary"),
        name="mix_out_col",
    )(hf, hb, o, x3, mod, *consts)
    return out.reshape(n, d)


def kernel(x, c, ctx, c_ctx, mod_w, mod_b, norm_g, m_w_in, m_b_gate, m_w_conv, m_g_head, m_w_out,
           h_w_in, h_b_f, h_lb_raw, h_g_head, h_w_out, f_w_gu, f_w_down,
           e_w_router, e_b_router, e_w_gu, e_w_down):
    batch, seq, d = x.shape
    ctx_len = ctx.shape[1]
    depth = mod_w.shape[0]
    assert seq % GRID_W == 0 and seq // GRID_W == CHUNK and ctx_len % CHUNK == 0
    assert batch < MOD_ROWS and d % LANES == 0
    lat_seg = _Seg(batch, seq, False)
    ctx_seg = _Seg(batch, ctx_len, True)

    c_all = jnp.zeros((MOD_ROWS, d), F32).at[:batch].set(c).at[batch].set(c_ctx)
    mod_all = _modulation(c_all, mod_w, mod_b).reshape(depth, MOD_ROWS, N_MOD, d)

    lb_all = jax.nn.softmax(h_lb_raw.astype(F32), axis=0)
    lb_all = jnp.cumsum(lb_all, axis=0) - lb_all[0]

    xl = x.reshape(batch * seq, d)
    xc = ctx.reshape(batch * ctx_len, d)
    row = lambda v: v.reshape(1, -1).astype(F32)

    for i in range(depth):
        j = i // 2
        with_ctx = i < depth - 1
        mod = mod_all[i]
        g0, g1, g2, g3 = (row(norm_g[i, s]) for s in range(4))
        if i % 2 == 0:
            qk_w = 2 * M_HEADS * M_DK
            v_w = M_HEADS * M_DV
            w_in = m_w_in[j]
            wqk = w_in[:, :qk_w].astype(BF16)
            wv = w_in[:, qk_w:qk_w + v_w].astype(BF16)
            wo = w_in[:, qk_w + v_w:qk_w + 2 * v_w].astype(BF16)
            ng = 4 * M_HEADS
            wg = jnp.zeros((d, LANES), F32).at[:, :ng].set(w_in[:, qk_w + 2 * v_w:]).astype(BF16)
            gb = jnp.zeros((1, LANES), F32).at[0, :ng].set(m_b_gate[j].reshape(-1))
            col = jnp.arange(LANES)
            gm = (((col // M_HEADS) % 2 == 1) & (col < ng)).astype(F32).reshape(1, LANES)
            pw = (mod, g0, wqk, wv, wo, wg, m_w_conv[j].astype(F32), gb, gm)
            qc, kc, vc, oc, gc = _proj_m(xc, ctx_seg, *pw)
            ql, kl, vl, ol, gl = _proj_m(xl, lat_seg, *pw)
            s0 = jnp.zeros((batch, 2, M_HEADS // 2, LANES, 2 * M_DV), F32)
            m0 = jnp.zeros((batch, 2 * M_HEADS, LANES), F32)
            hcf, hcb, s1, m1 = _mlstm_scan(qc, kc, vc, gc, s0, m0, ctx_seg)
            hlf, hlb, _, _ = _mlstm_scan(ql, kl, vl, gl, s1, m1, lat_seg)
            gh = row(m_g_head[j])
            wout = m_w_out[j].astype(BF16)
            xl = _mix_out(hlf, hlb, ol, xl, lat_seg, mod, gh, wout, g1, M_HEADS, in_place=i > 0)
            if with_ctx:
                xc = _mix_out(hcf, hcb, oc, xc, ctx_seg, mod, gh, wout, g1, M_HEADS, in_place=i > 0)
        else:
            kw = H_HEADS * H_DK
            w_in = h_w_in[j].astype(BF16)
            ws = [w_in[:, s * kw:(s + 1) * kw] for s in range(5)]
            qc, vc, ffc, fbc, gc = _proj_h_ctx(xc, ctx_seg, mod, g0, ws)
            ql, vl, ffl, fbl, gl = _proj_h_lat(xl, lat_seg, mod, g0, ws)
            bias = h_b_f[j].astype(F32)
            lb = lb_all[j]
            s0 = jnp.zeros((batch, 2, H_HEADS, H_DV, H_DK), F32)
            hcf, hcb, s1 = _hgrn_scan(qc, vc, ffc, fbc, bias, lb, s0, ctx_seg)
            hlf, hlb, _ = _hgrn_scan(ql, vl, ffl, fbl, bias, lb, s1, lat_seg)
            gh = row(h_g_head[j])
            wout = h_w_out[j].astype(BF16)
            xl = _mix_out_col(hlf, hlb, gl, xl, lat_seg, mod, gh, wout, g1, H_HEADS)
            if with_ctx:
                xc = _mix_out(hcf, hcb, gc, xc, ctx_seg, mod, gh, wout, g1, H_HEADS)
        streams = [(xl, lat_seg)] + ([(xc, ctx_seg)] if with_ctx else [])
        outs = []
        if i % 2 == 0:
            wgu, wd = f_w_gu[j].astype(BF16), f_w_down[j].astype(BF16)
            for xs, seg in streams:
                outs.append(_ffn(xs, seg, mod, g2, wgu, wd, g3, th=256))
        else:
            wr = jnp.zeros((d, LANES), F32).at[:, :N_EXPERTS].set(e_w_router[j])
            br = jnp.zeros((1, LANES), F32).at[0, :N_EXPERTS].set(e_b_router[j])
            wgu, wd = e_w_gu[j].astype(BF16), e_w_down[j].astype(BF16)
            for xs, seg in streams:
                outs.append(_moe_routed(xs, seg, mod, g2, wr, br, wgu, wd, g3, th=256))
        xl = outs[0]
        if with_ctx:
            xc = outs[1]
    return xl.reshape(batch, seq, d)
```

```python
import functools

import jax
import jax.numpy as jnp
from jax import lax
from jax.experimental import pallas as pl
from jax.experimental.pallas import tpu as pltpu

F32 = jnp.float32
BF16 = jnp.bfloat16
EPS = 1e-6
NEG = -1e30

GRID_W = 64
N_MOD = 6
M_HEADS, M_DK, M_DV = 8, 64, 128
H_HEADS, H_DK, H_DV = 8, 128, 128
N_EXPERTS = 8
LANES = 128
SUBLANES = 8
CHUNK = 128
MOD_ROWS = 16
VMEM_LIMIT = 56 * 1024 * 1024

HI = lax.Precision.HIGHEST
NT_DIMS = (((1,), (1,)), ((), ()))
TN_DIMS = (((0,), (0,)), ((), ()))


def _cp(*sem):
    return pltpu.CompilerParams(dimension_semantics=sem, vmem_limit_bytes=VMEM_LIMIT)


def _full(a, nargs):
    zeros = (0,) * a.ndim
    return pl.BlockSpec(a.shape, lambda *_: zeros)


def _sigmoid(x):
    return 0.5 * jnp.tanh(0.5 * x) + 0.5


def _silu(x):
    return x * _sigmoid(x)


def _log_sigmoid(x):
    return jnp.minimum(x, 0.0) - jnp.log(1.0 + jnp.exp(-jnp.abs(x)))


def _rms(x):
    return x * lax.rsqrt(jnp.mean(x * x, axis=-1, keepdims=True) + EPS)


def _normmod(x, g, mod, s):
    return _rms(x) * (g * (1.0 + mod[s + 1:s + 2])) + mod[s:s + 1]


def _split2(x):
    hi = x.astype(BF16)
    lo = (x - hi.astype(F32)).astype(BF16)
    return hi, lo


def _cumsum_mat(n, reverse):
    r = lax.broadcasted_iota(jnp.int32, (n, n), 0)
    c = lax.broadcasted_iota(jnp.int32, (n, n), 1)
    return jnp.where((c >= r) if reverse else (c <= r), 1.0, 0.0).astype(BF16)


def _cumsum_time(x, tri):
    hi, lo = _split2(x)
    return (jnp.dot(tri, hi, preferred_element_type=F32)
            + jnp.dot(tri, lo, preferred_element_type=F32))


class _Seg:
    def __init__(self, batch, seqlen, is_ctx):
        self.batch, self.seqlen, self.is_ctx = batch, seqlen, is_ctx
        self.n = batch * seqlen

    def tile(self, pref):
        tm = min(pref, self.n if self.is_ctx else self.seqlen)
        assert self.n % tm == 0 and (self.seqlen % tm == 0 or tm % self.seqlen == 0)
        return tm

    def mod_spec(self, tm, d):
        if self.is_ctx:
            return pl.BlockSpec((1, N_MOD, d), lambda j: (self.batch, 0, 0))
        return pl.BlockSpec((1, N_MOD, d), lambda j: ((j * tm) // self.seqlen, 0, 0))


def _mod_kernel(c_ref, w_ref, b_ref, o_ref):
    c = c_ref[...]
    o_ref[0] = jnp.dot(_silu(c), w_ref[0], preferred_element_type=F32, precision=HI) + b_ref[0]


def _modulation(c_all, mod_w, mod_b):
    depth, d, n = mod_w.shape
    tn = n // 4
    return pl.pallas_call(
        _mod_kernel,
        out_shape=jax.ShapeDtypeStruct((depth, MOD_ROWS, n), F32),
        grid=(depth, n // tn),
        in_specs=[pl.BlockSpec((MOD_ROWS, d), lambda i, j: (0, 0)),
                  pl.BlockSpec((1, d, tn), lambda i, j: (i, 0, j)),
                  pl.BlockSpec((1, 1, tn), lambda i, j: (i, 0, j))],
        out_specs=pl.BlockSpec((1, MOD_ROWS, tn), lambda i, j: (i, 0, j)),
        compiler_params=_cp("arbitrary", "arbitrary"),
        name="modulation",
    )(c_all, mod_w, mod_b.reshape(depth, 1, n))


def _proj_m_kernel(xp_ref, x_ref, xn_ref, mod_ref, g_ref, wqk_ref, wv_ref, wo_ref, wg_ref,
                   cw_ref, gb_ref, gm_ref, q_ref, kt_ref, v_ref, o_ref, gt_ref, *, tm, seqlen):
    j = pl.program_id(0)
    mod = mod_ref[0]
    g = g_ref[...]
    u = _normmod(x_ref[...], g, mod, 0).astype(BF16)
    uh = _normmod(jnp.concatenate([xp_ref[...], xn_ref[...]], axis=0), g, mod, 0).astype(BF16)
    wqk = wqk_ref[...]
    z = jnp.dot(u, wqk, preferred_element_type=F32)
    zh = jnp.dot(uh, wqk, preferred_element_type=F32)
    local = lax.broadcasted_iota(jnp.int32, (tm, 1), 0)
    pos = lax.rem(j * tm + local, seqlen)
    zp = jnp.where(local == 0, zh[SUBLANES - 1:SUBLANES], pltpu.roll(z, 1, 0))
    zn = jnp.where(local == tm - 1, zh[SUBLANES:SUBLANES + 1], pltpu.roll(z, tm - 1, 0))
    zp = jnp.where(pos == 0, 0.0, zp)
    zn = jnp.where(pos == seqlen - 1, 0.0, zn)
    cw = cw_ref[...]
    a = _silu(cw[0:1] * zp + cw[1:2] * z + cw[2:3] * zn)
    half = a.shape[1] // 2
    q_ref[...] = a[:, :half].astype(BF16)
    kt_ref[...] = (a[:, half:] * (M_DK ** -0.5)).T.astype(BF16)
    v_ref[...] = jnp.dot(u, wv_ref[...], preferred_element_type=F32).astype(BF16)
    o_ref[...] = jnp.dot(u, wo_ref[...], preferred_element_type=F32).astype(BF16)
    zg = jnp.dot(u, wg_ref[...], preferred_element_type=F32) + gb_ref[...]
    gt_ref[...] = jnp.where(gm_ref[...] > 0.5, _log_sigmoid(zg), zg)


def _proj_m(x, seg, mod, g, wqk, wv, wo, wg, cw, gb, gm):
    n, d = x.shape
    tm = seg.tile(512)
    nb8 = n // SUBLANES
    r8 = tm // SUBLANES
    kern = functools.partial(_proj_m_kernel, tm=tm, seqlen=seg.seqlen)
    nqk = wqk.shape[1]
    consts = (g, wqk, wv, wo, wg, cw, gb, gm)
    return pl.pallas_call(
        kern,
        out_shape=[jax.ShapeDtypeStruct((n, nqk // 2), BF16),
                   jax.ShapeDtypeStruct((nqk // 2, n), BF16),
                   jax.ShapeDtypeStruct((n, wv.shape[1]), BF16),
                   jax.ShapeDtypeStruct((n, wo.shape[1]), BF16),
                   jax.ShapeDtypeStruct((n, LANES), F32)],
        grid=(n // tm,),
        in_specs=[pl.BlockSpec((SUBLANES, d), lambda j: (jnp.maximum(j * r8 - 1, 0), 0)),
                  pl.BlockSpec((tm, d), lambda j: (j, 0)),
                  pl.BlockSpec((SUBLANES, d), lambda j: (jnp.minimum((j + 1) * r8, nb8 - 1), 0)),
                  seg.mod_spec(tm, d)] + [_full(a, 1) for a in consts],
        out_specs=[pl.BlockSpec((tm, nqk // 2), lambda j: (j, 0)),
                   pl.BlockSpec((nqk // 2, tm), lambda j: (0, j)),
                   pl.BlockSpec((tm, wv.shape[1]), lambda j: (j, 0)),
                   pl.BlockSpec((tm, wo.shape[1]), lambda j: (j, 0)),
                   pl.BlockSpec((tm, LANES), lambda j: (j, 0))],
        compiler_params=_cp("arbitrary"),
        name="mlstm_proj",
    )(x, x, x, mod, *consts)


MLSTM_PAIR_GROUP = 4


def _mlstm_gates(g_ref, m_prev, reverse, d):
    L = CHUNK
    g = g_ref[...]
    b = _cumsum_time(g, _cumsum_mat(L, reverse))
    gT, bT = g.T, b.T
    c_ig = 2 * M_HEADS * d
    b_rows = bT[c_ig + M_HEADS:c_ig + 2 * M_HEADS, :]
    beta = gT[c_ig:c_ig + M_HEADS, :] - b_rows
    lane = lax.broadcasted_iota(jnp.int32, (1, L), 1)
    cm = beta
    k = 1
    while k < L:
        if reverse:
            cm = jnp.maximum(cm, jnp.where(lane < L - k, pltpu.roll(cm, L - k, 1), NEG))
        else:
            cm = jnp.maximum(cm, jnp.where(lane >= k, pltpu.roll(cm, k, 1), NEG))
        k *= 2
    last = 0 if reverse else L - 1
    mu = jnp.maximum(cm, m_prev)
    mu_last = mu[:, last:last + 1]
    rows = jnp.concatenate([beta, jnp.exp(beta - mu_last)], axis=0)
    cols = jnp.concatenate([mu, jnp.exp(m_prev - mu), jnp.exp(-(b_rows + mu)),
                            jnp.zeros((LANES - 3 * M_HEADS, L), F32)], axis=0).T
    half = lax.broadcasted_iota(jnp.int32, (1, LANES), 1) < LANES // 2
    small = jnp.where(half, jnp.exp(m_prev - mu_last), b_rows[:, last:last + 1] + mu_last)
    return rows, cols, small


def _mlstm_heads(q_ref, kt_ref, v_ref, rows_g, cols, small, h_ref, s_scr, reverse, d, pairs):
    L = CHUNK
    r = lax.broadcasted_iota(jnp.int32, (L, L), 0)
    c = lax.broadcasted_iota(jnp.int32, (L, L), 1)
    causal = (c >= r) if reverse else (c <= r)
    lane = lax.broadcasted_iota(jnp.int32, (1, LANES), 1)
    ones = jnp.ones((L, M_DV), BF16)
    staged = []
    for p in pairs:
        q_p = q_ref[:, p * LANES:(p + 1) * LANES]
        kt_p = kt_ref[p * LANES:(p + 1) * LANES, :]
        for e in range(2):
            h = 2 * p + e
            q_e = jnp.where((lane // M_DK) == e, q_p, jnp.zeros_like(q_p))
            qk = jnp.dot(q_e, kt_p, preferred_element_type=F32)
            s = jnp.where(causal, qk * jnp.exp(rows_g[h:h + 1, :] - cols[:, h:h + 1]), 0.0).astype(BF16)
            ktw = (kt_p[e * M_DK:(e + 1) * M_DK, :].astype(F32)
                   * rows_g[M_HEADS + h:M_HEADS + h + 1, :]).astype(BF16)
            staged.append((p, e, q_e, s, ktw))
    for p, e, q_e, s, ktw in staged:
        h = 2 * p + e
        rows = slice(e * M_DK, (e + 1) * M_DK)
        vp = jnp.concatenate([v_ref[:, h * M_DV:(h + 1) * M_DV], ones], axis=1)
        w_inter = cols[:, M_HEADS + h:M_HEADS + h + 1]
        floor = cols[:, 2 * M_HEADS + h:2 * M_HEADS + h + 1]
        tot = (jnp.dot(s, vp, preferred_element_type=F32)
               + w_inter * jnp.dot(q_e, s_scr[d, p].astype(BF16), preferred_element_type=F32))
        den = jnp.maximum(jnp.abs(tot[:, M_DV:]), floor)
        h_ref[:, h * M_DV:(h + 1) * M_DV] = (tot[:, :M_DV] / den).astype(BF16)
        s_scr[d, p, rows, :] = (small[h:h + 1, 0:1] * s_scr[d, p, rows, :]
                                + jnp.dot(ktw, vp, preferred_element_type=F32))


def _mlstm_scan_kernel(qf, kf, vf, gf, gf_next, qb, kb, vb, gb, gb_next, s0_ref, m0_ref,
                       hf_ref, hb_ref, st_ref, mt_ref, s_scr, m_scr, rows_scr, cols_scr, small_scr):
    n = pl.program_id(1)
    H = M_HEADS
    g_cur, g_next = (gf, gb), (gf_next, gb_next)

    @pl.when(n == 0)
    def _():
        s_scr[...] = s0_ref[0]
        for d in range(2):
            rows_scr[d], cols_scr[d], small_scr[d] = _mlstm_gates(
                g_cur[d], m0_ref[0, d * H:(d + 1) * H, 0:1], d == 1, d)

    cur = [(rows_scr[d], cols_scr[d], small_scr[d]) for d in range(2)]
    m_new = [cur[d][2][:, LANES // 2:LANES // 2 + 1] for d in range(2)]
    m_scr[...] = jnp.broadcast_to(jnp.concatenate(m_new, axis=0), m_scr.shape)
    for d in range(2):
        rows_scr[d], cols_scr[d], small_scr[d] = _mlstm_gates(g_next[d], m_new[d], d == 1, d)
    for p0 in range(0, M_HEADS // 2, MLSTM_PAIR_GROUP):
        pairs = range(p0, p0 + MLSTM_PAIR_GROUP)
        _mlstm_heads(qf, kf, vf, *cur[0], hf_ref, s_scr, False, 0, pairs)
        _mlstm_heads(qb, kb, vb, *cur[1], hb_ref, s_scr, True, 1, pairs)

    @pl.when(n == pl.num_programs(1) - 1)
    def _():
        st_ref[0] = s_scr[...]
        mt_ref[0] = m_scr[...]


def _mlstm_scan(q, kt, v, gates, s0, m0, seg):
    L = CHUNK
    assert seg.seqlen % L == 0
    n = seg.seqlen // L
    fwd = lambda b, i: (b * n + i, 0)
    bwd = lambda b, i: (b * n + (n - 1 - i), 0)
    fwd_next = lambda b, i: (b * n + jnp.minimum(i + 1, n - 1), 0)
    bwd_next = lambda b, i: (b * n + jnp.maximum(n - 2 - i, 0), 0)
    fwd_t = lambda b, i: (0, b * n + i)
    bwd_t = lambda b, i: (0, b * n + (n - 1 - i))
    dq, dv = q.shape[1], v.shape[1]
    st = lambda b, i: (b, 0, 0, 0, 0)
    mt = lambda b, i: (b, 0, 0)
    return pl.pallas_call(
        _mlstm_scan_kernel,
        out_shape=[jax.ShapeDtypeStruct((seg.n, dv), BF16), jax.ShapeDtypeStruct((seg.n, dv), BF16),
                   jax.ShapeDtypeStruct(s0.shape, F32), jax.ShapeDtypeStruct(m0.shape, F32)],
        grid=(seg.batch, n),
        in_specs=[pl.BlockSpec((L, dq), fwd), pl.BlockSpec((dq, L), fwd_t), pl.BlockSpec((L, dv), fwd),
                  pl.BlockSpec((L, LANES), fwd), pl.BlockSpec((L, LANES), fwd_next),
                  pl.BlockSpec((L, dq), bwd), pl.BlockSpec((dq, L), bwd_t), pl.BlockSpec((L, dv), bwd),
                  pl.BlockSpec((L, LANES), bwd), pl.BlockSpec((L, LANES), bwd_next),
                  pl.BlockSpec((1,) + s0.shape[1:], st), pl.BlockSpec((1,) + m0.shape[1:], mt)],
        out_specs=[pl.BlockSpec((L, dv), fwd), pl.BlockSpec((L, dv), bwd),
                   pl.BlockSpec((1,) + s0.shape[1:], st), pl.BlockSpec((1,) + m0.shape[1:], mt)],
        scratch_shapes=[pltpu.VMEM(s0.shape[1:], F32), pltpu.VMEM(m0.shape[1:], F32),
                        pltpu.VMEM((2, 2 * M_HEADS, L), F32), pltpu.VMEM((2, L, LANES), F32),
                        pltpu.VMEM((2, M_HEADS, LANES), F32)],
        compiler_params=_cp("arbitrary", "arbitrary"),
        name="mlstm_scan",
    )(q, kt, v, gates, gates, q, kt, v, gates, gates, s0, m0)


def _head_out(hf, hb, gate_pre, gh, w, n_heads):
    h = hf.astype(F32) + hb.astype(F32)
    dv = h.shape[1] // n_heads
    parts = [_rms(h[:, i * dv:(i + 1) * dv]) for i in range(n_heads)]
    hn = jnp.concatenate(parts, axis=1) * gh * _sigmoid(gate_pre.astype(F32))
    return jnp.dot(hn.astype(BF16), w, preferred_element_type=F32)


def _mix_out_kernel(hf_ref, hb_ref, o_ref, x_ref, mod_ref, gh_ref, w_ref, g1_ref, out_ref, *, n_heads):
    mod = mod_ref[0]
    y = _head_out(hf_ref[...], hb_ref[...], o_ref[...], gh_ref[...], w_ref[...], n_heads)
    out_ref[...] = x_ref[...] + mod[2:3] * (_rms(y) * g1_ref[...])


def _mix_out(hf, hb, o, x, seg, mod, gh, w, g1, n_heads, in_place=True):
    n, d = x.shape
    tm = seg.tile(512)
    rmap = lambda j: (j, 0)
    consts = (gh, w, g1)
    return pl.pallas_call(
        functools.partial(_mix_out_kernel, n_heads=n_heads),
        out_shape=jax.ShapeDtypeStruct(x.shape, F32),
        grid=(n // tm,),
        in_specs=[pl.BlockSpec((tm, hf.shape[1]), rmap), pl.BlockSpec((tm, hf.shape[1]), rmap),
                  pl.BlockSpec((tm, o.shape[1]), rmap), pl.BlockSpec((tm, d), rmap),
                  seg.mod_spec(tm, d)] + [_full(a, 1) for a in consts],
        out_specs=pl.BlockSpec((tm, d), rmap),
        input_output_aliases={3: 0} if in_place else {},
        compiler_params=_cp("arbitrary"),
        name="mix_out",
    )(hf, hb, o, x, mod, *consts)


def _swiglu_acc(u_scr, wgu_ref, wd_ref, acc_scr, th):
    hd = wd_ref.shape[0]
    assert hd % th == 0 and hd % LANES == 0

    def part(off):
        u = u_scr[...]
        gg = jnp.dot(u, wgu_ref[:, pl.ds(off, th)], preferred_element_type=F32)
        up = jnp.dot(u, wgu_ref[:, pl.ds(pl.multiple_of(hd + off, LANES), th)], preferred_element_type=F32)
        hh = (_silu(gg) * up).astype(BF16)
        return jnp.dot(hh, wd_ref[pl.ds(off, th), :], preferred_element_type=F32)

    def chunk(k, carry):
        acc_scr[...] += part(pl.multiple_of(k * th, th))
        return carry

    acc_scr[...] = part(0)
    lax.fori_loop(1, hd // th, chunk, 0)


def _ffn_kernel(x_ref, mod_ref, g2_ref, wgu_ref, wd_ref, g3_ref, out_ref, u_scr, acc_scr, *, th):
    u_scr[...] = _normmod(x_ref[...], g2_ref[...], mod_ref[0], 3).astype(BF16)
    _swiglu_acc(u_scr, wgu_ref, wd_ref, acc_scr, th)
    mod = mod_ref[0]
    out_ref[...] = x_ref[...] + mod[5:6] * (_rms(acc_scr[...]) * g3_ref[...])


def _resident(a, index_map):
    return pl.BlockSpec(a, index_map, pipeline_mode=pl.Buffered(1))


def _ffn(x, seg, mod, g2, wgu, wd, g3, th):
    n, d = x.shape
    tm = seg.tile(1024)
    return pl.pallas_call(
        functools.partial(_ffn_kernel, th=th),
        out_shape=jax.ShapeDtypeStruct((n, d), F32),
        grid=(n // tm,),
        in_specs=[pl.BlockSpec((tm, d), lambda i: (i, 0)),
                  seg.mod_spec(tm, d),
                  _full(g2, 1),
                  _resident(wgu.shape, lambda i: (0, 0)),
                  _resident(wd.shape, lambda i: (0, 0)),
                  _full(g3, 1)],
        out_specs=pl.BlockSpec((tm, d), lambda i: (i, 0)),
        scratch_shapes=[pltpu.VMEM((tm, d), BF16), pltpu.VMEM((tm, d), F32)],
        input_output_aliases={0: 0},
        compiler_params=_cp("arbitrary"),
        name="ffn",
    )(x, mod, g2, wgu, wd, g3)


META_E, META_RANK, META_W = 0, 2, 4


def _route_kernel(x_ref, mod_ref, g2_ref, wr_ref, br_ref, meta_ref, cnt_ref, tri_scr, run_scr):
    i = pl.program_id(0)
    tm = x_ref.shape[0]

    @pl.when(i == 0)
    def _():
        r = lax.broadcasted_iota(jnp.int32, (tm, tm), 0)
        c = lax.broadcasted_iota(jnp.int32, (tm, tm), 1)
        tri_scr[...] = jnp.where(c < r, 1.0, 0.0).astype(BF16)
        run_scr[...] = jnp.zeros_like(run_scr)

    u = _normmod(x_ref[...], g2_ref[...], mod_ref[0], 3)
    lane = lax.broadcasted_iota(jnp.int32, (1, LANES), 1)
    u_hi, u_lo = _split2(u)
    w_hi, w_lo = _split2(wr_ref[...])
    logits = (jnp.dot(u_hi, w_hi, preferred_element_type=F32) + jnp.dot(u_lo, w_hi, preferred_element_type=F32)
              + jnp.dot(u_hi, w_lo, preferred_element_type=F32)) + br_ref[...]
    logits = jnp.where(lane < N_EXPERTS, logits, NEG)
    m1 = jnp.max(logits, axis=1, keepdims=True)
    i1 = jnp.min(jnp.where(logits == m1, lane, LANES), axis=1, keepdims=True)
    rest = jnp.where(lane == i1, NEG, logits)
    m2 = jnp.max(rest, axis=1, keepdims=True)
    i2 = jnp.min(jnp.where(rest == m2, lane, LANES), axis=1, keepdims=True)
    e2 = jnp.exp(m2 - m1)
    w1 = 1.0 / (1.0 + e2)
    sel = ((lane == i1) | (lane == i2))
    onehot = jnp.where(sel, 1.0, 0.0)
    before = jnp.dot(tri_scr[...], onehot.astype(BF16), preferred_element_type=F32) + run_scr[...]
    rank1 = jnp.sum(jnp.where(lane == i1, before, 0.0), axis=1, keepdims=True)
    rank2 = jnp.sum(jnp.where(lane == i2, before, 0.0), axis=1, keepdims=True)
    run_scr[...] = run_scr[...] + jnp.sum(onehot, axis=0, keepdims=True)
    vals = (i1.astype(F32), i2.astype(F32), rank1, rank2, w1, e2 * w1)
    meta = jnp.zeros((tm, LANES), F32)
    for col, val in enumerate(vals):
        meta = jnp.where(lane == col, val, meta)
    meta_ref[...] = meta
    cnt_ref[...] = jnp.broadcast_to(run_scr[...], cnt_ref.shape)


def _route(x, seg, mod, g2, wr, br):
    n, d = x.shape
    tm = seg.tile(1024)
    consts = (g2, wr, br)
    return pl.pallas_call(
        _route_kernel,
        out_shape=[jax.ShapeDtypeStruct((n, LANES), F32),
                   jax.ShapeDtypeStruct((SUBLANES, LANES), F32)],
        grid=(n // tm,),
        in_specs=[pl.BlockSpec((tm, d), lambda i: (i, 0)), seg.mod_spec(tm, d)]
                 + [_full(a, 1) for a in consts],
        out_specs=[pl.BlockSpec((tm, LANES), lambda i: (i, 0)),
                   pl.BlockSpec((SUBLANES, LANES), lambda i: (0, 0))],
        scratch_shapes=[pltpu.VMEM((tm, tm), BF16), pltpu.VMEM((1, LANES), F32)],
        compiler_params=_cp("arbitrary"),
        name="route",
    )(x, mod, *consts)


def _drain(src_ref, dst_ref, sem, count):
    def body(i, carry):
        pltpu.make_async_copy(src_ref, dst_ref, sem).wait()
        return carry

    lax.fori_loop(0, count, body, 0, unroll=8)


def _row_tile(buf, t):
    return buf.at[pl.ds(pl.multiple_of(t * SUBLANES, SUBLANES), SUBLANES), :]


def _dispatch_kernel(dst_ref, pad_ref, x_ref, mod_ref, g2_ref, xs_ref, ubuf, sems):
    j = pl.program_id(0)
    tm = x_ref.shape[0]
    slot = lax.rem(j, 2)
    buf = ubuf.at[slot]
    sem = sems.at[slot]
    u = _normmod(x_ref[...], g2_ref[...], mod_ref[0], 3)
    for s in range(SUBLANES):
        ubuf[slot, pl.ds(s, tm, stride=SUBLANES), :] = u[:, s * LANES:(s + 1) * LANES]

    def issue(t, carry):
        pltpu.make_async_copy(_row_tile(buf, t), xs_ref.at[dst_ref[0, 0, t]], sem).start(priority=0)
        pltpu.make_async_copy(_row_tile(buf, t), xs_ref.at[dst_ref[0, 0, tm + t]], sem).start(priority=1)
        return carry

    lax.fori_loop(0, tm, issue, 0, unroll=8)
    npad = pad_ref.shape[2]
    first = _row_tile(buf, 0)

    @pl.when(j == 0)
    def _():
        def issue_pad(p, carry):
            pltpu.make_async_copy(first, xs_ref.at[pad_ref[0, 0, p]], sem).start()
            return carry

        lax.fori_loop(0, npad, issue_pad, 0, unroll=8)
        _drain(first, xs_ref.at[0], sem, npad)

    @pl.when(j > 0)
    def _():
        _drain(_row_tile(ubuf.at[1 - slot], 0), xs_ref.at[0], sems.at[1 - slot], 2 * tm)

    @pl.when(j == pl.num_programs(0) - 1)
    def _():
        _drain(first, xs_ref.at[0], sem, 2 * tm)


def _dispatch(x, seg, mod, g2, dest, pad_dst, n_rows):
    n, d = x.shape
    tm = seg.tile(512)
    steps = n // tm
    dst = dest.reshape(steps, tm, 2).transpose(0, 2, 1).reshape(steps, 1, 2 * tm)
    npad = pad_dst.shape[0]
    return pl.pallas_call(
        _dispatch_kernel,
        out_shape=jax.ShapeDtypeStruct((n_rows, SUBLANES, d // SUBLANES), F32),
        grid=(steps,),
        in_specs=[pl.BlockSpec((1, 1, 2 * tm), lambda j: (j, 0, 0), memory_space=pltpu.SMEM),
                  pl.BlockSpec((1, 1, npad), lambda j: (0, 0, 0), memory_space=pltpu.SMEM),
                  pl.BlockSpec((tm, d), lambda j: (j, 0)), seg.mod_spec(tm, d), _full(g2, 1)],
        out_specs=pl.BlockSpec(memory_space=pl.ANY),
        scratch_shapes=[pltpu.VMEM((2, tm * SUBLANES, d // SUBLANES), F32), pltpu.SemaphoreType.DMA((2,))],
        compiler_params=_cp("arbitrary"),
        name="dispatch",
    )(dst, pad_dst.reshape(1, 1, npad), x, mod, g2)


def _experts_kernel(te_ref, used_ref, xs_ref, wgu_ref, wd_ref, y_ref, u_scr, acc_scr, *, th):
    i = pl.program_id(0)

    @pl.when(i < used_ref[0])
    def _():
        tr = u_scr.shape[0]
        for s in range(SUBLANES):
            u_scr[:, s * LANES:(s + 1) * LANES] = xs_ref[pl.ds(s, tr, stride=SUBLANES), :].astype(BF16)
        _swiglu_acc(u_scr, wgu_ref.at[0], wd_ref.at[0], acc_scr, th)
        y = acc_scr[...]
        for s in range(SUBLANES):
            y_ref[pl.ds(s, tr, stride=SUBLANES), :] = y[:, s * LANES:(s + 1) * LANES]


def _experts(xs3, tile_expert, used, wgu, wd, th, tr):
    n_rows = xs3.shape[0]
    ne, hd, d = wd.shape
    tiles = tile_expert.shape[0]
    assert tiles * tr <= n_rows
    row = lambda i, te, us: (jnp.minimum(i, us[0] - 1), 0)
    grid_spec = pltpu.PrefetchScalarGridSpec(
        num_scalar_prefetch=2,
        grid=(tiles,),
        in_specs=[pl.BlockSpec((tr * SUBLANES, LANES), row),
                  _resident((1,) + wgu.shape[1:], lambda i, te, us: (te[i], 0, 0)),
                  _resident((1,) + wd.shape[1:], lambda i, te, us: (te[i], 0, 0))],
        out_specs=pl.BlockSpec((tr * SUBLANES, LANES), row),
        scratch_shapes=[pltpu.VMEM((tr, d), BF16), pltpu.VMEM((tr, d), F32)],
    )
    y = pl.pallas_call(
        functools.partial(_experts_kernel, th=th),
        out_shape=jax.ShapeDtypeStruct((tiles * tr * SUBLANES, LANES), F32),
        grid_spec=grid_spec,
        compiler_params=_cp("arbitrary"),
        name="experts",
    )(tile_expert, used, xs3.reshape(n_rows * SUBLANES, LANES), wgu, wd)
    return y.reshape(tiles * tr, SUBLANES, LANES)


def _combine_kernel(src_ref, nxt_ref, y_ref, meta_ref, x_ref, mod_ref, g3_ref, out_ref, ybuf, sems):
    j = pl.program_id(0)
    tm = x_ref.shape[0]
    slot = lax.rem(j, 2)

    def gather(idx_ref, sl):
        def issue(t, carry):
            pltpu.make_async_copy(y_ref.at[idx_ref[0, 0, t]], _row_tile(ybuf.at[sl, 0], t),
                                  sems.at[sl]).start(priority=0)
            pltpu.make_async_copy(y_ref.at[idx_ref[0, 0, tm + t]], _row_tile(ybuf.at[sl, 1], t),
                                  sems.at[sl]).start(priority=1)
            return carry

        lax.fori_loop(0, tm, issue, 0, unroll=8)

    @pl.when(j == 0)
    def _():
        gather(src_ref, 0)

    @pl.when(j + 1 < pl.num_programs(0))
    def _():
        gather(nxt_ref, 1 - slot)

    _drain(y_ref.at[0], _row_tile(ybuf.at[slot, 0], 0), sems.at[slot], 2 * tm)
    meta = meta_ref[...]
    w1 = meta[:, META_W:META_W + 1]
    w2 = meta[:, META_W + 1:META_W + 2]
    ys = [w1 * ybuf[slot, 0, pl.ds(s, tm, stride=SUBLANES), :]
          + w2 * ybuf[slot, 1, pl.ds(s, tm, stride=SUBLANES), :] for s in range(SUBLANES)]
    ss = ys[0] * ys[0]
    for y in ys[1:]:
        ss = ss + y * y
    d = x_ref.shape[1]
    r = lax.rsqrt(jnp.sum(ss, axis=1, keepdims=True) / d + EPS)
    mod = mod_ref[0]
    for s, y in enumerate(ys):
        sl = slice(s * LANES, (s + 1) * LANES)
        out_ref[:, sl] = x_ref[:, sl] + mod[5:6, sl] * (y * r * g3_ref[:, sl])


def _combine(y3, dest, meta, x, seg, mod, g3):
    n, d = x.shape
    tm = seg.tile(512)
    steps = n // tm
    src = dest.reshape(steps, tm, 2).transpose(0, 2, 1).reshape(steps, 1, 2 * tm)
    return pl.pallas_call(
        _combine_kernel,
        out_shape=jax.ShapeDtypeStruct((n, d), F32),
        grid=(steps,),
        in_specs=[pl.BlockSpec((1, 1, 2 * tm), lambda j: (j, 0, 0), memory_space=pltpu.SMEM),
                  pl.BlockSpec((1, 1, 2 * tm), lambda j: (jnp.minimum(j + 1, steps - 1), 0, 0),
                               memory_space=pltpu.SMEM),
                  pl.BlockSpec(memory_space=pl.ANY),
                  pl.BlockSpec((tm, LANES), lambda j: (j, 0)),
                  pl.BlockSpec((tm, d), lambda j: (j, 0)),
                  seg.mod_spec(tm, d), _full(g3, 1)],
        out_specs=pl.BlockSpec((tm, d), lambda j: (j, 0)),
        scratch_shapes=[pltpu.VMEM((2, 2, tm * SUBLANES, d // SUBLANES), F32), pltpu.SemaphoreType.DMA((2,))],
        input_output_aliases={4: 0},
        compiler_params=_cp("arbitrary"),
        name="combine",
    )(src, src, y3, meta, x, mod, g3)


def _moe_routed(x, seg, mod, g2, wr, br, wgu, wd, g3, th):
    n, d = x.shape
    assert d == SUBLANES * LANES
    tr = max(256, min(1024, pl.next_power_of_2((2 * n) // N_EXPERTS) // 2))
    meta, cnt = _route(x, seg, mod, g2, wr, br)
    counts = cnt[0, :N_EXPERTS].astype(jnp.int32)
    sizes = ((counts + tr - 1) // tr) * tr
    ends = jnp.cumsum(sizes)
    offs = ends - sizes
    mi = meta[:, :META_W].astype(jnp.int32)
    eidx = mi[:, META_E:META_E + 2]
    dest = jnp.sum(jnp.where(eidx[:, :, None] == jnp.arange(N_EXPERTS)[None, None, :],
                             offs[None, None, :], 0), axis=2) + mi[:, META_RANK:META_RANK + 2]
    dest = dest.astype(jnp.int32)
    tiles = (2 * n) // tr + N_EXPERTS
    n_rows = tiles * tr
    used = (ends[-1] // tr).astype(jnp.int32).reshape(1)
    tile_start = jnp.arange(tiles, dtype=jnp.int32) * tr
    tile_expert = jnp.sum(tile_start[:, None] >= ends[None, :], axis=1).astype(jnp.int32)
    last_expert = jnp.sum(jnp.maximum(ends[-1] - tr, 0) >= ends).astype(jnp.int32)
    tile_expert = jnp.where(tile_start < ends[-1], tile_expert, last_expert)
    pad = jnp.arange(N_EXPERTS * tr, dtype=jnp.int32).reshape(N_EXPERTS, tr)
    pad_row = offs[:, None] + counts[:, None] + jnp.arange(tr, dtype=jnp.int32)[None, :]
    pad_dst = jnp.where(pad_row < ends[:, None], pad_row, n_rows + pad).reshape(-1).astype(jnp.int32)
    xs3 = _dispatch(x, seg, mod, g2, dest, pad_dst, n_rows + N_EXPERTS * tr)
    y3 = _experts(xs3, tile_expert, used, wgu, wd, th, tr)
    return _combine(y3, dest, meta, x, seg, mod, g3)


def _proj_h_body(u, w_refs, out_refs):
    for w_ref, o_ref in zip(w_refs, out_refs):
        o_ref[...] = jnp.dot(u, w_ref[...], preferred_element_type=F32).astype(BF16)


def _proj_h_lat_kernel(x_ref, mod_ref, g_ref, *refs, rows_per_col):
    n = len(refs) // 2
    w_refs, out_refs = refs[:n], refs[n:]
    x = jnp.swapaxes(x_ref[...], 0, 1)
    x = x.reshape(SUBLANES * rows_per_col, x.shape[2])
    u = _normmod(x, g_ref[...], mod_ref[0], 0).astype(BF16)
    _proj_h_body(u, w_refs, out_refs)


def _proj_h_ctx_kernel(x_ref, mod_ref, g_ref, *refs):
    n = len(refs) // 2
    u = _normmod(x_ref[...], g_ref[...], mod_ref[0], 0).astype(BF16)
    _proj_h_body(u, refs[:n], refs[n:])


def _proj_h_lat(x, seg, mod, g, ws):
    n, d = x.shape
    grid_rows = seg.seqlen // GRID_W
    x3 = x.reshape(n // GRID_W, GRID_W, d)
    tcol = SUBLANES * grid_rows
    ncb = GRID_W // SUBLANES
    return pl.pallas_call(
        functools.partial(_proj_h_lat_kernel, rows_per_col=grid_rows),
        out_shape=[jax.ShapeDtypeStruct((n, w.shape[1]), BF16) for w in ws],
        grid=(seg.batch, ncb),
        in_specs=[pl.BlockSpec((grid_rows, SUBLANES, d), lambda b, c: (b, c, 0)),
                  pl.BlockSpec((1, N_MOD, d), lambda b, c: (b, 0, 0)), _full(g, 2)]
                 + [_full(w, 2) for w in ws],
        out_specs=[pl.BlockSpec((tcol, w.shape[1]), lambda b, c: (b * ncb + c, 0)) for w in ws],
        compiler_params=_cp("arbitrary", "arbitrary"),
        name="hgrn_proj_lat",
    )(x3, mod, g, *ws)


def _proj_h_ctx(x, seg, mod, g, ws):
    n, d = x.shape
    tm = seg.tile(512)
    return pl.pallas_call(
        _proj_h_ctx_kernel,
        out_shape=[jax.ShapeDtypeStruct((n, w.shape[1]), BF16) for w in ws],
        grid=(n // tm,),
        in_specs=[pl.BlockSpec((tm, d), lambda j: (j, 0)), seg.mod_spec(tm, d), _full(g, 1)]
                 + [_full(w, 1) for w in ws],
        out_specs=[pl.BlockSpec((tm, w.shape[1]), lambda j: (j, 0)) for w in ws],
        compiler_params=_cp("arbitrary"),
        name="hgrn_proj_ctx",
    )(x, mod, g, *ws)


HGRN_HEAD_GROUP = 4


def _hgrn_prologue(f_ref, bias_ref, lb_ref, reverse, d):
    lb = lb_ref[d:d + 1, :]
    f = lb + (1.0 - lb) * _sigmoid(f_ref[...].astype(F32) + bias_ref[d:d + 1, :])
    return f, _cumsum_time(jnp.log(f), _cumsum_mat(CHUNK, reverse))


def _hgrn_heads(q_ref, v_ref, pro, h_ref, s_scr, reverse, d, heads):
    L = CHUNK
    f, a_all = pro
    r = lax.broadcasted_iota(jnp.int32, (L, L), 0)
    c = lax.broadcasted_iota(jnp.int32, (L, L), 1)
    causal = (c >= r) if reverse else (c <= r)
    same64 = (r // 64) == (c // 64)
    diag32 = ((r // 32) == (c // 32)) & causal
    t = lax.broadcasted_iota(jnp.int32, (L, 1), 0)
    if reverse:
        late0, late1 = t < 64, (t % 64) < 32
        a0, a1 = (64,), (32, 96)
        a2 = (31, 63, 95, 127)
        last = 0
    else:
        late0, late1 = t >= 64, (t % 64) >= 32
        a0, a1 = (63,), (31, 95)
        a2 = (0, 32, 64, 96)
        last = L - 1
    staged = []
    for h in heads:
        sl = slice(h * H_DK, (h + 1) * H_DK)
        a = a_all[:, sl]
        q = q_ref[:, sl].astype(F32)
        k = 1.0 - f[:, sl]
        anc0 = jnp.broadcast_to(a[a0[0]:a0[0] + 1, :], (L, H_DK))
        anc1 = jnp.where(t < 64, a[a1[0]:a1[0] + 1, :], a[a1[1]:a1[1] + 1, :])
        anc2 = jnp.where(t < 32, a[a2[0]:a2[0] + 1, :],
                         jnp.where(t < 64, a[a2[1]:a2[1] + 1, :],
                                   jnp.where(t < 96, a[a2[2]:a2[2] + 1, :], a[a2[3]:a2[3] + 1, :])))

        def level(anc, late):
            qs = q * jnp.exp(a - anc)
            ks = k * jnp.exp(anc - a)
            if late is not None:
                qs = jnp.where(late, qs, 0.0)
                ks = jnp.where(late, 0.0, ks)
            return lax.dot_general(qs.astype(BF16), ks.astype(BF16), NT_DIMS, preferred_element_type=F32)

        s = jnp.where(diag32, level(anc2, None), jnp.where(same64, level(anc1, late1), level(anc0, late0)))
        qd = (q * jnp.exp(a)).astype(BF16)
        a_last = a[last:last + 1, :]
        kd = (k * jnp.exp(a_last - a)).astype(BF16)
        staged.append((s.astype(BF16), qd, kd, jnp.exp(a_last)))
    for h, (s, qd, kd, dec) in zip(heads, staged):
        v = v_ref[:, h * H_DV:(h + 1) * H_DV]
        st = s_scr[d, h]
        o = (jnp.dot(s, v, preferred_element_type=F32)
             + lax.dot_general(qd, st.astype(BF16), NT_DIMS, preferred_element_type=F32))
        h_ref[:, h * H_DV:(h + 1) * H_DV] = o.astype(BF16)
        s_scr[d, h] = st * dec + lax.dot_general(v, kd, TN_DIMS, preferred_element_type=F32)


def _hgrn_scan_kernel(qf, vf, ff, qb, vb, fb, bias_ref, lb_ref, s0_ref, hf_ref, hb_ref, st_ref, s_scr):
    n = pl.program_id(1)

    @pl.when(n == 0)
    def _():
        s_scr[...] = s0_ref[0]

    pro_f = _hgrn_prologue(ff, bias_ref, lb_ref, False, 0)
    pro_b = _hgrn_prologue(fb, bias_ref, lb_ref, True, 1)
    for h0 in range(0, H_HEADS, HGRN_HEAD_GROUP):
        heads = range(h0, h0 + HGRN_HEAD_GROUP)
        _hgrn_heads(qf, vf, pro_f, hf_ref, s_scr, False, 0, heads)
        _hgrn_heads(qb, vb, pro_b, hb_ref, s_scr, True, 1, heads)

    @pl.when(n == pl.num_programs(1) - 1)
    def _():
        st_ref[0] = s_scr[...]


def _hgrn_scan(q, v, f_f, f_b, bias, lb, s0, seg):
    L = CHUNK
    assert seg.seqlen % L == 0
    n = seg.seqlen // L
    fwd = lambda b, i: (b * n + i, 0)
    bwd = lambda b, i: (b * n + (n - 1 - i), 0)
    rows_, w = q.shape
    st = lambda b, i: (b, 0, 0, 0, 0)
    return pl.pallas_call(
        _hgrn_scan_kernel,
        out_shape=[jax.ShapeDtypeStruct((rows_, w), BF16), jax.ShapeDtypeStruct((rows_, w), BF16),
                   jax.ShapeDtypeStruct(s0.shape, F32)],
        grid=(seg.batch, n),
        in_specs=[pl.BlockSpec((L, w), fwd), pl.BlockSpec((L, w), fwd), pl.BlockSpec((L, w), fwd),
                  pl.BlockSpec((L, w), bwd), pl.BlockSpec((L, w), bwd), pl.BlockSpec((L, w), bwd),
                  _full(bias, 2), _full(lb, 2), pl.BlockSpec((1,) + s0.shape[1:], st)],
        out_specs=[pl.BlockSpec((L, w), fwd), pl.BlockSpec((L, w), bwd),
                   pl.BlockSpec((1,) + s0.shape[1:], st)],
        scratch_shapes=[pltpu.VMEM(s0.shape[1:], F32)],
        compiler_params=_cp("arbitrary", "arbitrary"),
        name="hgrn_scan",
    )(q, v, f_f, q, v, f_b, bias, lb, s0)


def _mix_out_col_kernel(hf_ref, hb_ref, o_ref, x_ref, mod_ref, gh_ref, w_ref, g1_ref, out_ref,
                        *, n_heads, rows_per_col):
    mod = mod_ref[0]
    y = _head_out(hf_ref[...], hb_ref[...], o_ref[...], gh_ref[...], w_ref[...], n_heads)
    upd = mod[2:3] * (_rms(y) * g1_ref[...])
    upd = jnp.swapaxes(upd.reshape(SUBLANES, rows_per_col, upd.shape[1]), 0, 1)
    out_ref[...] = x_ref[...] + upd


def _mix_out_col(hf, hb, o, x, seg, mod, gh, w, g1, n_heads):
    n, d = x.shape
    grid_rows = seg.seqlen // GRID_W
    tcol = SUBLANES * grid_rows
    ncb = GRID_W // SUBLANES
    x3 = x.reshape(n // GRID_W, GRID_W, d)
    hmap = lambda b, c: (b * ncb + c, 0)
    xmap = lambda b, c: (b, c, 0)
    consts = (gh, w, g1)
    out = pl.pallas_call(
        functools.partial(_mix_out_col_kernel, n_heads=n_heads, rows_per_col=grid_rows),
        out_shape=jax.ShapeDtypeStruct(x3.shape, F32),
        grid=(seg.batch, ncb),
        in_specs=[pl.BlockSpec((tcol, hf.shape[1]), hmap), pl.BlockSpec((tcol, hf.shape[1]), hmap),
                  pl.BlockSpec((tcol, o.shape[1]), hmap),
                  pl.BlockSpec((grid_rows, SUBLANES, d), xmap),
                  pl.BlockSpec((1, N_MOD, d), lambda b, c: (b, 0, 0))] + [_full(a, 2) for a in consts],
        out_specs=pl.BlockSpec((grid_rows, SUBLANES, d), xmap),
        input_output_aliases={3: 0},
        compiler_params=_cp("arbitrary", "arbitrary"),
        name="mix_out_col",
    )(hf, hb, o, x3, mod, *consts)
    return out.reshape(n, d)


def kernel(x, c, ctx, c_ctx, mod_w, mod_b, norm_g, m_w_in, m_b_gate, m_w_conv, m_g_head, m_w_out,
           h_w_in, h_b_f, h_lb_raw, h_g_head, h_w_out, f_w_gu, f_w_down,
           e_w_router, e_b_router, e_w_gu, e_w_down):
    batch, seq, d = x.shape
    ctx_len = ctx.shape[1]
    depth = mod_w.shape[0]
    assert seq % GRID_W == 0 and seq // GRID_W == CHUNK and ctx_len % CHUNK == 0
    assert batch < MOD_ROWS and d % LANES == 0
    lat_seg = _Seg(batch, seq, False)
    ctx_seg = _Seg(batch, ctx_len, True)

    c_all = jnp.zeros((MOD_ROWS, d), F32).at[:batch].set(c).at[batch].set(c_ctx)
    mod_all = _modulation(c_all, mod_w, mod_b).reshape(depth, MOD_ROWS, N_MOD, d)

    lb_all = jax.nn.softmax(h_lb_raw.astype(F32), axis=0)
    lb_all = jnp.cumsum(lb_all, axis=0) - lb_all[0]

    xl = x.reshape(batch * seq, d)
    xc = ctx.reshape(batch * ctx_len, d)
    row = lambda v: v.reshape(1, -1).astype(F32)

    for i in range(depth):
        j = i // 2
        with_ctx = i < depth - 1
        mod = mod_all[i]
        g0, g1, g2, g3 = (row(norm_g[i, s]) for s in range(4))
        if i % 2 == 0:
            qk_w = 2 * M_HEADS * M_DK
            v_w = M_HEADS * M_DV
            w_in = m_w_in[j]
            wqk = w_in[:, :qk_w].astype(BF16)
            wv = w_in[:, qk_w:qk_w + v_w].astype(BF16)
            wo = w_in[:, qk_w + v_w:qk_w + 2 * v_w].astype(BF16)
            ng = 4 * M_HEADS
            wg = jnp.zeros((d, LANES), F32).at[:, :ng].set(w_in[:, qk_w + 2 * v_w:]).astype(BF16)
            gb = jnp.zeros((1, LANES), F32).at[0, :ng].set(m_b_gate[j].reshape(-1))
            col = jnp.arange(LANES)
            gm = (((col // M_HEADS) % 2 == 1) & (col < ng)).astype(F32).reshape(1, LANES)
            pw = (mod, g0, wqk, wv, wo, wg, m_w_conv[j].astype(F32), gb, gm)
            qc, kc, vc, oc, gc = _proj_m(xc, ctx_seg, *pw)
            ql, kl, vl, ol, gl = _proj_m(xl, lat_seg, *pw)
            s0 = jnp.zeros((batch, 2, M_HEADS // 2, LANES, 2 * M_DV), F32)
            m0 = jnp.zeros((batch, 2 * M_HEADS, LANES), F32)
            hcf, hcb, s1, m1 = _mlstm_scan(qc, kc, vc, gc, s0, m0, ctx_seg)
            hlf, hlb, _, _ = _mlstm_scan(ql, kl, vl, gl, s1, m1, lat_seg)
            gh = row(m_g_head[j])
            wout = m_w_out[j].astype(BF16)
            xl = _mix_out(hlf, hlb, ol, xl, lat_seg, mod, gh, wout, g1, M_HEADS, in_place=i > 0)
            if with_ctx:
                xc = _mix_out(hcf, hcb, oc, xc, ctx_seg, mod, gh, wout, g1, M_HEADS, in_place=i > 0)
        else:
            kw = H_HEADS * H_DK
            w_in = h_w_in[j].astype(BF16)
            ws = [w_in[:, s * kw:(s + 1) * kw] for s in range(5)]
            qc, vc, ffc, fbc, gc = _proj_h_ctx(xc, ctx_seg, mod, g0, ws)
            ql, vl, ffl, fbl, gl = _proj_h_lat(xl, lat_seg, mod, g0, ws)
            bias = h_b_f[j].astype(F32)
            lb = lb_all[j]
            s0 = jnp.zeros((batch, 2, H_HEADS, H_DV, H_DK), F32)
            hcf, hcb, s1 = _hgrn_scan(qc, vc, ffc, fbc, bias, lb, s0, ctx_seg)
            hlf, hlb, _ = _hgrn_scan(ql, vl, ffl, fbl, bias, lb, s1, lat_seg)
            gh = row(h_g_head[j])
            wout = h_w_out[j].astype(BF16)
            xl = _mix_out_col(hlf, hlb, gl, xl, lat_seg, mod, gh, wout, g1, H_HEADS)
            if with_ctx:
                xc = _mix_out(hcf, hcb, gc, xc, ctx_seg, mod, gh, wout, g1, H_HEADS)
        streams = [(xl, lat_seg)] + ([(xc, ctx_seg)] if with_ctx else [])
        outs = []
        if i % 2 == 0:
            wgu, wd = f_w_gu[j].astype(BF16), f_w_down[j].astype(BF16)
            for xs, seg in streams:
                outs.append(_ffn(xs, seg, mod, g2, wgu, wd, g3, th=256))
        else:
            wr = jnp.zeros((d, LANES), F32).at[:, :N_EXPERTS].set(e_w_router[j])
            br = jnp.zeros((1, LANES), F32).at[0, :N_EXPERTS].set(e_b_router[j])
            wgu, wd = e_w_gu[j].astype(BF16), e_w_down[j].astype(BF16)
            for xs, seg in streams:
                outs.append(_moe_routed(xs, seg, mod, g2, wr, br, wgu, wd, g3, th=256))
        xl = outs[0]
        if with_ctx:
            xc = outs[1]
    return xl.reshape(batch, seq, d)
```

```python
import functools

import jax
import jax.numpy as jnp
from jax import lax
from jax.experimental import pallas as pl
from jax.experimental.pallas import tpu as pltpu

F32 = jnp.float32
BF16 = jnp.bfloat16
EPS = 1e-6
NEG = -1e30

GRID_W = 64
N_MOD = 6
M_HEADS, M_DK, M_DV = 8, 64, 128
H_HEADS, H_DK, H_DV = 8, 128, 128
N_EXPERTS = 8
LANES = 128
SUBLANES = 8
CHUNK = 128
MOD_ROWS = 16
VMEM_LIMIT = 56 * 1024 * 1024

HI = lax.Precision.HIGHEST
NT_DIMS = (((1,), (1,)), ((), ()))
TN_DIMS = (((0,), (0,)), ((), ()))


def _cp(*sem):
    return pltpu.CompilerParams(dimension_semantics=sem, vmem_limit_bytes=VMEM_LIMIT)


def _full(a, nargs):
    zeros = (0,) * a.ndim
    return pl.BlockSpec(a.shape, lambda *_: zeros)


def _sigmoid(x):
    return 0.5 * jnp.tanh(0.5 * x) + 0.5


def _silu(x):
    return x * _sigmoid(x)


def _log_sigmoid(x):
    return jnp.minimum(x, 0.0) - jnp.log(1.0 + jnp.exp(-jnp.abs(x)))


def _rms(x):
    return x * lax.rsqrt(jnp.mean(x * x, axis=-1, keepdims=True) + EPS)


def _normmod(x, g, mod, s):
    return _rms(x) * (g * (1.0 + mod[s + 1:s + 2])) + mod[s:s + 1]


def _split2(x):
    hi = x.astype(BF16)
    lo = (x - hi.astype(F32)).astype(BF16)
    return hi, lo


def _cumsum_mat(n, reverse):
    r = lax.broadcasted_iota(jnp.int32, (n, n), 0)
    c = lax.broadcasted_iota(jnp.int32, (n, n), 1)
    return jnp.where((c >= r) if reverse else (c <= r), 1.0, 0.0).astype(BF16)


def _cumsum_time(x, tri):
    hi, lo = _split2(x)
    return (jnp.dot(tri, hi, preferred_element_type=F32)
            + jnp.dot(tri, lo, preferred_element_type=F32))


class _Seg:
    def __init__(self, batch, seqlen, is_ctx):
        self.batch, self.seqlen, self.is_ctx = batch, seqlen, is_ctx
        self.n = batch * seqlen

    def tile(self, pref):
        tm = min(pref, self.n if self.is_ctx else self.seqlen)
        assert self.n % tm == 0 and (self.seqlen % tm == 0 or tm % self.seqlen == 0)
        return tm

    def mod_spec(self, tm, d):
        if self.is_ctx:
            return pl.BlockSpec((1, N_MOD, d), lambda j: (self.batch, 0, 0))
        return pl.BlockSpec((1, N_MOD, d), lambda j: ((j * tm) // self.seqlen, 0, 0))


def _mod_kernel(c_ref, w_ref, b_ref, o_ref):
    c = c_ref[...]
    o_ref[0] = jnp.dot(_silu(c), w_ref[0], preferred_element_type=F32, precision=HI) + b_ref[0]


def _modulation(c_all, mod_w, mod_b):
    depth, d, n = mod_w.shape
    tn = n // 4
    return pl.pallas_call(
        _mod_kernel,
        out_shape=jax.ShapeDtypeStruct((depth, MOD_ROWS, n), F32),
        grid=(depth, n // tn),
        in_specs=[pl.BlockSpec((MOD_ROWS, d), lambda i, j: (0, 0)),
                  pl.BlockSpec((1, d, tn), lambda i, j: (i, 0, j)),
                  pl.BlockSpec((1, 1, tn), lambda i, j: (i, 0, j))],
        out_specs=pl.BlockSpec((1, MOD_ROWS, tn), lambda i, j: (i, 0, j)),
        compiler_params=_cp("arbitrary", "arbitrary"),
        name="modulation",
    )(c_all, mod_w, mod_b.reshape(depth, 1, n))


def _proj_m_kernel(xp_ref, x_ref, xn_ref, mod_ref, g_ref, wqk_ref, wv_ref, wo_ref, wg_ref,
                   cw_ref, gb_ref, gm_ref, q_ref, kt_ref, v_ref, o_ref, gt_ref, *, tm, seqlen):
    j = pl.program_id(0)
    mod = mod_ref[0]
    g = g_ref[...]
    u = _normmod(x_ref[...], g, mod, 0).astype(BF16)
    uh = _normmod(jnp.concatenate([xp_ref[...], xn_ref[...]], axis=0), g, mod, 0).astype(BF16)
    wqk = wqk_ref[...]
    z = jnp.dot(u, wqk, preferred_element_type=F32)
    zh = jnp.dot(uh, wqk, preferred_element_type=F32)
    local = lax.broadcasted_iota(jnp.int32, (tm, 1), 0)
    pos = lax.rem(j * tm + local, seqlen)
    zp = jnp.where(local == 0, zh[SUBLANES - 1:SUBLANES], pltpu.roll(z, 1, 0))
    zn = jnp.where(local == tm - 1, zh[SUBLANES:SUBLANES + 1], pltpu.roll(z, tm - 1, 0))
    zp = jnp.where(pos == 0, 0.0, zp)
    zn = jnp.where(pos == seqlen - 1, 0.0, zn)
    cw = cw_ref[...]
    a = _silu(cw[0:1] * zp + cw[1:2] * z + cw[2:3] * zn)
    half = a.shape[1] // 2
    q_ref[...] = a[:, :half].astype(BF16)
    kt_ref[...] = (a[:, half:] * (M_DK ** -0.5)).T.astype(BF16)
    v_ref[...] = jnp.dot(u, wv_ref[...], preferred_element_type=F32).astype(BF16)
    o_ref[...] = jnp.dot(u, wo_ref[...], preferred_element_type=F32).astype(BF16)
    zg = jnp.dot(u, wg_ref[...], preferred_element_type=F32) + gb_ref[...]
    gt_ref[...] = jnp.where(gm_ref[...] > 0.5, _log_sigmoid(zg), zg)


def _proj_m(x, seg, mod, g, wqk, wv, wo, wg, cw, gb, gm):
    n, d = x.shape
    tm = seg.tile(512)
    nb8 = n // SUBLANES
    r8 = tm // SUBLANES
    kern = functools.partial(_proj_m_kernel, tm=tm, seqlen=seg.seqlen)
    nqk = wqk.shape[1]
    consts = (g, wqk, wv, wo, wg, cw, gb, gm)
    return pl.pallas_call(
        kern,
        out_shape=[jax.ShapeDtypeStruct((n, nqk // 2), BF16),
                   jax.ShapeDtypeStruct((nqk // 2, n), BF16),
                   jax.ShapeDtypeStruct((n, wv.shape[1]), BF16),
                   jax.ShapeDtypeStruct((n, wo.shape[1]), BF16),
                   jax.ShapeDtypeStruct((n, LANES), F32)],
        grid=(n // tm,),
        in_specs=[pl.BlockSpec((SUBLANES, d), lambda j: (jnp.maximum(j * r8 - 1, 0), 0)),
                  pl.BlockSpec((tm, d), lambda j: (j, 0)),
                  pl.BlockSpec((SUBLANES, d), lambda j: (jnp.minimum((j + 1) * r8, nb8 - 1), 0)),
                  seg.mod_spec(tm, d)] + [_full(a, 1) for a in consts],
        out_specs=[pl.BlockSpec((tm, nqk // 2), lambda j: (j, 0)),
                   pl.BlockSpec((nqk // 2, tm), lambda j: (0, j)),
                   pl.BlockSpec((tm, wv.shape[1]), lambda j: (j, 0)),
                   pl.BlockSpec((tm, wo.shape[1]), lambda j: (j, 0)),
                   pl.BlockSpec((tm, LANES), lambda j: (j, 0))],
        compiler_params=_cp("arbitrary"),
        name="mlstm_proj",
    )(x, x, x, mod, *consts)


MLSTM_PAIR_GROUP = 4


def _mlstm_gates(g_ref, m_prev, reverse, d):
    L = CHUNK
    g = g_ref[...]
    b = _cumsum_time(g, _cumsum_mat(L, reverse))
    gT, bT = g.T, b.T
    c_ig = 2 * M_HEADS * d
    b_rows = bT[c_ig + M_HEADS:c_ig + 2 * M_HEADS, :]
    beta = gT[c_ig:c_ig + M_HEADS, :] - b_rows
    lane = lax.broadcasted_iota(jnp.int32, (1, L), 1)
    cm = beta
    k = 1
    while k < L:
        if reverse:
            cm = jnp.maximum(cm, jnp.where(lane < L - k, pltpu.roll(cm, L - k, 1), NEG))
        else:
            cm = jnp.maximum(cm, jnp.where(lane >= k, pltpu.roll(cm, k, 1), NEG))
        k *= 2
    last = 0 if reverse else L - 1
    mu = jnp.maximum(cm, m_prev)
    mu_last = mu[:, last:last + 1]
    rows = jnp.concatenate([beta, jnp.exp(beta - mu_last)], axis=0)
    cols = jnp.concatenate([mu, jnp.exp(m_prev - mu), jnp.exp(-(b_rows + mu)),
                            jnp.zeros((LANES - 3 * M_HEADS, L), F32)], axis=0).T
    half = lax.broadcasted_iota(jnp.int32, (1, LANES), 1) < LANES // 2
    small = jnp.where(half, jnp.exp(m_prev - mu_last), b_rows[:, last:last + 1] + mu_last)
    return rows, cols, small


def _mlstm_heads(q_ref, kt_ref, v_ref, rows_g, cols, small, h_ref, s_scr, reverse, d, pairs):
    L = CHUNK
    r = lax.broadcasted_iota(jnp.int32, (L, L), 0)
    c = lax.broadcasted_iota(jnp.int32, (L, L), 1)
    causal = (c >= r) if reverse else (c <= r)
    lane = lax.broadcasted_iota(jnp.int32, (1, LANES), 1)
    ones = jnp.ones((L, M_DV), BF16)
    staged = []
    for p in pairs:
        q_p = q_ref[:, p * LANES:(p + 1) * LANES]
        kt_p = kt_ref[p * LANES:(p + 1) * LANES, :]
        for e in range(2):
            h = 2 * p + e
            q_e = jnp.where((lane // M_DK) == e, q_p, jnp.zeros_like(q_p))
            qk = jnp.dot(q_e, kt_p, preferred_element_type=F32)
            s = jnp.where(causal, qk * jnp.exp(rows_g[h:h + 1, :] - cols[:, h:h + 1]), 0.0).astype(BF16)
            ktw = (kt_p[e * M_DK:(e + 1) * M_DK, :].astype(F32)
                   * rows_g[M_HEADS + h:M_HEADS + h + 1, :]).astype(BF16)
            staged.append((p, e, q_e, s, ktw))
    for p, e, q_e, s, ktw in staged:
        h = 2 * p + e
        rows = slice(e * M_DK, (e + 1) * M_DK)
        vp = jnp.concatenate([v_ref[:, h * M_DV:(h + 1) * M_DV], ones], axis=1)
        w_inter = cols[:, M_HEADS + h:M_HEADS + h + 1]
        floor = cols[:, 2 * M_HEADS + h:2 * M_HEADS + h + 1]
        tot = (jnp.dot(s, vp, preferred_element_type=F32)
               + w_inter * jnp.dot(q_e, s_scr[d, p].astype(BF16), preferred_element_type=F32))
        den = jnp.maximum(jnp.abs(tot[:, M_DV:]), floor)
        h_ref[:, h * M_DV:(h + 1) * M_DV] = (tot[:, :M_DV] / den).astype(BF16)
        s_scr[d, p, rows, :] = (small[h:h + 1, 0:1] * s_scr[d, p, rows, :]
                                + jnp.dot(ktw, vp, preferred_element_type=F32))


def _mlstm_scan_kernel(qf, kf, vf, gf, gf_next, qb, kb, vb, gb, gb_next, s0_ref, m0_ref,
                       hf_ref, hb_ref, st_ref, mt_ref, s_scr, m_scr, rows_scr, cols_scr, small_scr):
    n = pl.program_id(1)
    H = M_HEADS
    g_cur, g_next = (gf, gb), (gf_next, gb_next)

    @pl.when(n == 0)
    def _():
        s_scr[...] = s0_ref[0]
        for d in range(2):
            rows_scr[d], cols_scr[d], small_scr[d] = _mlstm_gates(
                g_cur[d], m0_ref[0, d * H:(d + 1) * H, 0:1], d == 1, d)

    cur = [(rows_scr[d], cols_scr[d], small_scr[d]) for d in range(2)]
    m_new = [cur[d][2][:, LANES // 2:LANES // 2 + 1] for d in range(2)]
    m_scr[...] = jnp.broadcast_to(jnp.concatenate(m_new, axis=0), m_scr.shape)
    for d in range(2):
        rows_scr[d], cols_scr[d], small_scr[d] = _mlstm_gates(g_next[d], m_new[d], d == 1, d)
    for p0 in range(0, M_HEADS // 2, MLSTM_PAIR_GROUP):
        pairs = range(p0, p0 + MLSTM_PAIR_GROUP)
        _mlstm_heads(qf, kf, vf, *cur[0], hf_ref, s_scr, False, 0, pairs)
        _mlstm_heads(qb, kb, vb, *cur[1], hb_ref, s_scr, True, 1, pairs)

    @pl.when(n == pl.num_programs(1) - 1)
    def _():
        st_ref[0] = s_scr[...]
        mt_ref[0] = m_scr[...]


def _mlstm_scan(q, kt, v, gates, s0, m0, seg):
    L = CHUNK
    assert seg.seqlen % L == 0
    n = seg.seqlen // L
    fwd = lambda b, i: (b * n + i, 0)
    bwd = lambda b, i: (b * n + (n - 1 - i), 0)
    fwd_next = lambda b, i: (b * n + jnp.minimum(i + 1, n - 1), 0)
    bwd_next = lambda b, i: (b * n + jnp.maximum(n - 2 - i, 0), 0)
    fwd_t = lambda b, i: (0, b * n + i)
    bwd_t = lambda b, i: (0, b * n + (n - 1 - i))
    dq, dv = q.shape[1], v.shape[1]
    st = lambda b, i: (b, 0, 0, 0, 0)
    mt = lambda b, i: (b, 0, 0)
    return pl.pallas_call(
        _mlstm_scan_kernel,
        out_shape=[jax.ShapeDtypeStruct((seg.n, dv), BF16), jax.ShapeDtypeStruct((seg.n, dv), BF16),
                   jax.ShapeDtypeStruct(s0.shape, F32), jax.ShapeDtypeStruct(m0.shape, F32)],
        grid=(seg.batch, n),
        in_specs=[pl.BlockSpec((L, dq), fwd), pl.BlockSpec((dq, L), fwd_t), pl.BlockSpec((L, dv), fwd),
                  pl.BlockSpec((L, LANES), fwd), pl.BlockSpec((L, LANES), fwd_next),
                  pl.BlockSpec((L, dq), bwd), pl.BlockSpec((dq, L), bwd_t), pl.BlockSpec((L, dv), bwd),
                  pl.BlockSpec((L, LANES), bwd), pl.BlockSpec((L, LANES), bwd_next),
                  pl.BlockSpec((1,) + s0.shape[1:], st), pl.BlockSpec((1,) + m0.shape[1:], mt)],
        out_specs=[pl.BlockSpec((L, dv), fwd), pl.BlockSpec((L, dv), bwd),
                   pl.BlockSpec((1,) + s0.shape[1:], st), pl.BlockSpec((1,) + m0.shape[1:], mt)],
        scratch_shapes=[pltpu.VMEM(s0.shape[1:], F32), pltpu.VMEM(m0.shape[1:], F32),
                        pltpu.VMEM((2, 2 * M_HEADS, L), F32), pltpu.VMEM((2, L, LANES), F32),
                        pltpu.VMEM((2, M_HEADS, LANES), F32)],
        compiler_params=_cp("arbitrary", "arbitrary"),
        name="mlstm_scan",
    )(q, kt, v, gates, gates, q, kt, v, gates, gates, s0, m0)


def _head_out(hf, hb, gate_pre, gh, w, n_heads):
    h = hf.astype(F32) + hb.astype(F32)
    dv = h.shape[1] // n_heads
    parts = [_rms(h[:, i * dv:(i + 1) * dv]) for i in range(n_heads)]
    hn = jnp.concatenate(parts, axis=1) * gh * _sigmoid(gate_pre.astype(F32))
    return jnp.dot(hn.astype(BF16), w, preferred_element_type=F32)


def _mix_out_kernel(hf_ref, hb_ref, o_ref, x_ref, mod_ref, gh_ref, w_ref, g1_ref, out_ref, *, n_heads):
    mod = mod_ref[0]
    y = _head_out(hf_ref[...], hb_ref[...], o_ref[...], gh_ref[...], w_ref[...], n_heads)
    out_ref[...] = x_ref[...] + mod[2:3] * (_rms(y) * g1_ref[...])


def _mix_out(hf, hb, o, x, seg, mod, gh, w, g1, n_heads, in_place=True):
    n, d = x.shape
    tm = seg.tile(512)
    rmap = lambda j: (j, 0)
    consts = (gh, w, g1)
    return pl.pallas_call(
        functools.partial(_mix_out_kernel, n_heads=n_heads),
        out_shape=jax.ShapeDtypeStruct(x.shape, F32),
        grid=(n // tm,),
        in_specs=[pl.BlockSpec((tm, hf.shape[1]), rmap), pl.BlockSpec((tm, hf.shape[1]), rmap),
                  pl.BlockSpec((tm, o.shape[1]), rmap), pl.BlockSpec((tm, d), rmap),
                  seg.mod_spec(tm, d)] + [_full(a, 1) for a in consts],
        out_specs=pl.BlockSpec((tm, d), rmap),
        input_output_aliases={3: 0} if in_place else {},
        compiler_params=_cp("arbitrary"),
        name="mix_out",
    )(hf, hb, o, x, mod, *consts)


def _swiglu_acc(u_scr, wgu_ref, wd_ref, acc_scr, th):
    hd = wd_ref.shape[0]
    assert hd % th == 0 and hd % LANES == 0

    def part(off):
        u = u_scr[...]
        gg = jnp.dot(u, wgu_ref[:, pl.ds(off, th)], preferred_element_type=F32)
        up = jnp.dot(u, wgu_ref[:, pl.ds(pl.multiple_of(hd + off, LANES), th)], preferred_element_type=F32)
        hh = (_silu(gg) * up).astype(BF16)
        return jnp.dot(hh, wd_ref[pl.ds(off, th), :], preferred_element_type=F32)

    def chunk(k, carry):
        acc_scr[...] += part(pl.multiple_of(k * th, th))
        return carry

    assert hd // th >= 2
    acc_scr[...] = part(0)
    lax.fori_loop(1, hd // th - 1, chunk, 0)
    return acc_scr[...] + part(hd - th)


def _ffn_kernel(x_ref, mod_ref, g2_ref, wgu_ref, wd_ref, g3_ref, out_ref, u_scr, acc_scr, *, th):
    u_scr[...] = _normmod(x_ref[...], g2_ref[...], mod_ref[0], 3).astype(BF16)
    y = _swiglu_acc(u_scr, wgu_ref, wd_ref, acc_scr, th)
    mod = mod_ref[0]
    out_ref[...] = x_ref[...] + mod[5:6] * (_rms(y) * g3_ref[...])


def _resident(a, index_map):
    return pl.BlockSpec(a, index_map, pipeline_mode=pl.Buffered(1))


def _ffn(x, seg, mod, g2, wgu, wd, g3, th):
    n, d = x.shape
    tm = seg.tile(1024)
    return pl.pallas_call(
        functools.partial(_ffn_kernel, th=th),
        out_shape=jax.ShapeDtypeStruct((n, d), F32),
        grid=(n // tm,),
        in_specs=[pl.BlockSpec((tm, d), lambda i: (i, 0)),
                  seg.mod_spec(tm, d),
                  _full(g2, 1),
                  _resident(wgu.shape, lambda i: (0, 0)),
                  _resident(wd.shape, lambda i: (0, 0)),
                  _full(g3, 1)],
        out_specs=pl.BlockSpec((tm, d), lambda i: (i, 0)),
        scratch_shapes=[pltpu.VMEM((tm, d), BF16), pltpu.VMEM((tm, d), F32)],
        input_output_aliases={0: 0},
        compiler_params=_cp("arbitrary"),
        name="ffn",
    )(x, mod, g2, wgu, wd, g3)


META_E, META_RANK, META_W = 0, 2, 4


def _route_kernel(x_ref, mod_ref, g2_ref, wr_ref, br_ref, meta_ref, cnt_ref, tri_scr, run_scr):
    i = pl.program_id(0)
    tm = x_ref.shape[0]

    @pl.when(i == 0)
    def _():
        r = lax.broadcasted_iota(jnp.int32, (tm, tm), 0)
        c = lax.broadcasted_iota(jnp.int32, (tm, tm), 1)
        tri_scr[...] = jnp.where(c < r, 1.0, 0.0).astype(BF16)
        run_scr[...] = jnp.zeros_like(run_scr)

    u = _normmod(x_ref[...], g2_ref[...], mod_ref[0], 3)
    lane = lax.broadcasted_iota(jnp.int32, (1, LANES), 1)
    u_hi, u_lo = _split2(u)
    w_hi, w_lo = _split2(wr_ref[...])
    logits = (jnp.dot(u_hi, w_hi, preferred_element_type=F32) + jnp.dot(u_lo, w_hi, preferred_element_type=F32)
              + jnp.dot(u_hi, w_lo, preferred_element_type=F32)) + br_ref[...]
    logits = jnp.where(lane < N_EXPERTS, logits, NEG)
    m1 = jnp.max(logits, axis=1, keepdims=True)
    i1 = jnp.min(jnp.where(logits == m1, lane, LANES), axis=1, keepdims=True)
    rest = jnp.where(lane == i1, NEG, logits)
    m2 = jnp.max(rest, axis=1, keepdims=True)
    i2 = jnp.min(jnp.where(rest == m2, lane, LANES), axis=1, keepdims=True)
    e2 = jnp.exp(m2 - m1)
    w1 = 1.0 / (1.0 + e2)
    sel = ((lane == i1) | (lane == i2))
    onehot = jnp.where(sel, 1.0, 0.0)
    before = jnp.dot(tri_scr[...], onehot.astype(BF16), preferred_element_type=F32) + run_scr[...]
    rank1 = jnp.sum(jnp.where(lane == i1, before, 0.0), axis=1, keepdims=True)
    rank2 = jnp.sum(jnp.where(lane == i2, before, 0.0), axis=1, keepdims=True)
    run_scr[...] = run_scr[...] + jnp.sum(onehot, axis=0, keepdims=True)
    vals = (i1.astype(F32), i2.astype(F32), rank1, rank2, w1, e2 * w1)
    meta = jnp.zeros((tm, LANES), F32)
    for col, val in enumerate(vals):
        meta = jnp.where(lane == col, val, meta)
    meta_ref[...] = meta
    cnt_ref[...] = jnp.broadcast_to(run_scr[...], cnt_ref.shape)


def _route(x, seg, mod, g2, wr, br):
    n, d = x.shape
    tm = seg.tile(1024)
    consts = (g2, wr, br)
    return pl.pallas_call(
        _route_kernel,
        out_shape=[jax.ShapeDtypeStruct((n, LANES), F32),
                   jax.ShapeDtypeStruct((SUBLANES, LANES), F32)],
        grid=(n // tm,),
        in_specs=[pl.BlockSpec((tm, d), lambda i: (i, 0)), seg.mod_spec(tm, d)]
                 + [_full(a, 1) for a in consts],
        out_specs=[pl.BlockSpec((tm, LANES), lambda i: (i, 0)),
                   pl.BlockSpec((SUBLANES, LANES), lambda i: (0, 0))],
        scratch_shapes=[pltpu.VMEM((tm, tm), BF16), pltpu.VMEM((1, LANES), F32)],
        compiler_params=_cp("arbitrary"),
        name="route",
    )(x, mod, *consts)


def _drain(src_ref, dst_ref, sem, count):
    def body(i, carry):
        pltpu.make_async_copy(src_ref, dst_ref, sem).wait()
        return carry

    lax.fori_loop(0, count, body, 0, unroll=8)


def _row_tile(buf, t):
    return buf.at[pl.ds(pl.multiple_of(t * SUBLANES, SUBLANES), SUBLANES), :]


def _dispatch_kernel(dst_ref, pad_ref, x_ref, mod_ref, g2_ref, xs_ref, ubuf, sems):
    j = pl.program_id(0)
    tm = x_ref.shape[0]
    slot = lax.rem(j, 2)
    buf = ubuf.at[slot]
    sem = sems.at[slot]
    u = _normmod(x_ref[...], g2_ref[...], mod_ref[0], 3)
    for s in range(SUBLANES):
        ubuf[slot, pl.ds(s, tm, stride=SUBLANES), :] = u[:, s * LANES:(s + 1) * LANES]

    def issue(t, carry):
        pltpu.make_async_copy(_row_tile(buf, t), xs_ref.at[dst_ref[0, 0, t]], sem).start(priority=0)
        pltpu.make_async_copy(_row_tile(buf, t), xs_ref.at[dst_ref[0, 0, tm + t]], sem).start(priority=1)
        return carry

    lax.fori_loop(0, tm, issue, 0, unroll=8)
    npad = pad_ref.shape[2]
    first = _row_tile(buf, 0)

    @pl.when(j == 0)
    def _():
        def issue_pad(p, carry):
            pltpu.make_async_copy(first, xs_ref.at[pad_ref[0, 0, p]], sem).start()
            return carry

        lax.fori_loop(0, npad, issue_pad, 0, unroll=8)
        _drain(first, xs_ref.at[0], sem, npad)

    @pl.when(j > 0)
    def _():
        _drain(_row_tile(ubuf.at[1 - slot], 0), xs_ref.at[0], sems.at[1 - slot], 2 * tm)

    @pl.when(j == pl.num_programs(0) - 1)
    def _():
        _drain(first, xs_ref.at[0], sem, 2 * tm)


def _dispatch(x, seg, mod, g2, dest, pad_dst, n_rows):
    n, d = x.shape
    tm = seg.tile(512)
    steps = n // tm
    dst = dest.reshape(steps, tm, 2).transpose(0, 2, 1).reshape(steps, 1, 2 * tm)
    npad = pad_dst.shape[0]
    return pl.pallas_call(
        _dispatch_kernel,
        out_shape=jax.ShapeDtypeStruct((n_rows, SUBLANES, d // SUBLANES), F32),
        grid=(steps,),
        in_specs=[pl.BlockSpec((1, 1, 2 * tm), lambda j: (j, 0, 0), memory_space=pltpu.SMEM),
                  pl.BlockSpec((1, 1, npad), lambda j: (0, 0, 0), memory_space=pltpu.SMEM),
                  pl.BlockSpec((tm, d), lambda j: (j, 0)), seg.mod_spec(tm, d), _full(g2, 1)],
        out_specs=pl.BlockSpec(memory_space=pl.ANY),
        scratch_shapes=[pltpu.VMEM((2, tm * SUBLANES, d // SUBLANES), F32), pltpu.SemaphoreType.DMA((2,))],
        compiler_params=_cp("arbitrary"),
        name="dispatch",
    )(dst, pad_dst.reshape(1, 1, npad), x, mod, g2)


def _experts_kernel(te_ref, used_ref, xs_ref, wgu_ref, wd_ref, y_ref, u_scr, acc_scr, *, th):
    i = pl.program_id(0)

    @pl.when(i < used_ref[0])
    def _():
        tr = u_scr.shape[0]
        for s in range(SUBLANES):
            u_scr[:, s * LANES:(s + 1) * LANES] = xs_ref[pl.ds(s, tr, stride=SUBLANES), :].astype(BF16)
        y = _swiglu_acc(u_scr, wgu_ref.at[0], wd_ref.at[0], acc_scr, th)
        for s in range(SUBLANES):
            y_ref[pl.ds(s, tr, stride=SUBLANES), :] = y[:, s * LANES:(s + 1) * LANES]


def _experts(xs3, tile_expert, used, wgu, wd, th, tr):
    n_rows = xs3.shape[0]
    ne, hd, d = wd.shape
    tiles = tile_expert.shape[0]
    assert tiles * tr <= n_rows
    row = lambda i, te, us: (jnp.minimum(i, us[0] - 1), 0)
    grid_spec = pltpu.PrefetchScalarGridSpec(
        num_scalar_prefetch=2,
        grid=(tiles,),
        in_specs=[pl.BlockSpec((tr * SUBLANES, LANES), row),
                  _resident((1,) + wgu.shape[1:], lambda i, te, us: (te[i], 0, 0)),
                  _resident((1,) + wd.shape[1:], lambda i, te, us: (te[i], 0, 0))],
        out_specs=pl.BlockSpec((tr * SUBLANES, LANES), row),
        scratch_shapes=[pltpu.VMEM((tr, d), BF16), pltpu.VMEM((tr, d), F32)],
    )
    y = pl.pallas_call(
        functools.partial(_experts_kernel, th=th),
        out_shape=jax.ShapeDtypeStruct((tiles * tr * SUBLANES, LANES), F32),
        grid_spec=grid_spec,
        compiler_params=_cp("arbitrary"),
        name="experts",
    )(tile_expert, used, xs3.reshape(n_rows * SUBLANES, LANES), wgu, wd)
    return y.reshape(tiles * tr, SUBLANES, LANES)


def _combine_kernel(src_ref, nxt_ref, y_ref, meta_ref, x_ref, mod_ref, g3_ref, out_ref, ybuf, sems):
    j = pl.program_id(0)
    tm = x_ref.shape[0]
    slot = lax.rem(j, 2)

    def gather(idx_ref, sl):
        def issue(t, carry):
            pltpu.make_async_copy(y_ref.at[idx_ref[0, 0, t]], _row_tile(ybuf.at[sl, 0], t),
                                  sems.at[sl]).start(priority=0)
            pltpu.make_async_copy(y_ref.at[idx_ref[0, 0, tm + t]], _row_tile(ybuf.at[sl, 1], t),
                                  sems.at[sl]).start(priority=1)
            return carry

        lax.fori_loop(0, tm, issue, 0, unroll=8)

    @pl.when(j == 0)
    def _():
        gather(src_ref, 0)

    @pl.when(j + 1 < pl.num_programs(0))
    def _():
        gather(nxt_ref, 1 - slot)

    _drain(y_ref.at[0], _row_tile(ybuf.at[slot, 0], 0), sems.at[slot], 2 * tm)
    meta = meta_ref[...]
    w1 = meta[:, META_W:META_W + 1]
    w2 = meta[:, META_W + 1:META_W + 2]
    ys = [w1 * ybuf[slot, 0, pl.ds(s, tm, stride=SUBLANES), :]
          + w2 * ybuf[slot, 1, pl.ds(s, tm, stride=SUBLANES), :] for s in range(SUBLANES)]
    ss = ys[0] * ys[0]
    for y in ys[1:]:
        ss = ss + y * y
    d = x_ref.shape[1]
    r = lax.rsqrt(jnp.sum(ss, axis=1, keepdims=True) / d + EPS)
    mod = mod_ref[0]
    for s, y in enumerate(ys):
        sl = slice(s * LANES, (s + 1) * LANES)
        out_ref[:, sl] = x_ref[:, sl] + mod[5:6, sl] * (y * r * g3_ref[:, sl])


def _combine(y3, dest, meta, x, seg, mod, g3):
    n, d = x.shape
    tm = seg.tile(512)
    steps = n // tm
    src = dest.reshape(steps, tm, 2).transpose(0, 2, 1).reshape(steps, 1, 2 * tm)
    return pl.pallas_call(
        _combine_kernel,
        out_shape=jax.ShapeDtypeStruct((n, d), F32),
        grid=(steps,),
        in_specs=[pl.BlockSpec((1, 1, 2 * tm), lambda j: (j, 0, 0), memory_space=pltpu.SMEM),
                  pl.BlockSpec((1, 1, 2 * tm), lambda j: (jnp.minimum(j + 1, steps - 1), 0, 0),
                               memory_space=pltpu.SMEM),
                  pl.BlockSpec(memory_space=pl.ANY),
                  pl.BlockSpec((tm, LANES), lambda j: (j, 0)),
                  pl.BlockSpec((tm, d), lambda j: (j, 0)),
                  seg.mod_spec(tm, d), _full(g3, 1)],
        out_specs=pl.BlockSpec((tm, d), lambda j: (j, 0)),
        scratch_shapes=[pltpu.VMEM((2, 2, tm * SUBLANES, d // SUBLANES), F32), pltpu.SemaphoreType.DMA((2,))],
        input_output_aliases={4: 0},
        compiler_params=_cp("arbitrary"),
        name="combine",
    )(src, src, y3, meta, x, mod, g3)


def _moe_routed(x, seg, mod, g2, wr, br, wgu, wd, g3, th):
    n, d = x.shape
    assert d == SUBLANES * LANES
    tr = max(256, min(1024, pl.next_power_of_2((2 * n) // N_EXPERTS) // 2))
    meta, cnt = _route(x, seg, mod, g2, wr, br)
    counts = cnt[0, :N_EXPERTS].astype(jnp.int32)
    sizes = ((counts + tr - 1) // tr) * tr
    ends = jnp.cumsum(sizes)
    offs = ends - sizes
    mi = meta[:, :META_W].astype(jnp.int32)
    eidx = mi[:, META_E:META_E + 2]
    dest = jnp.sum(jnp.where(eidx[:, :, None] == jnp.arange(N_EXPERTS)[None, None, :],
                             offs[None, None, :], 0), axis=2) + mi[:, META_RANK:META_RANK + 2]
    dest = dest.astype(jnp.int32)
    tiles = (2 * n) // tr + N_EXPERTS
    n_rows = tiles * tr
    used = (ends[-1] // tr).astype(jnp.int32).reshape(1)
    tile_start = jnp.arange(tiles, dtype=jnp.int32) * tr
    tile_expert = jnp.sum(tile_start[:, None] >= ends[None, :], axis=1).astype(jnp.int32)
    last_expert = jnp.sum(jnp.maximum(ends[-1] - tr, 0) >= ends).astype(jnp.int32)
    tile_expert = jnp.where(tile_start < ends[-1], tile_expert, last_expert)
    pad = jnp.arange(N_EXPERTS * tr, dtype=jnp.int32).reshape(N_EXPERTS, tr)
    pad_row = offs[:, None] + counts[:, None] + jnp.arange(tr, dtype=jnp.int32)[None, :]
    pad_dst = jnp.where(pad_row < ends[:, None], pad_row, n_rows + pad).reshape(-1).astype(jnp.int32)
    xs3 = _dispatch(x, seg, mod, g2, dest, pad_dst, n_rows + N_EXPERTS * tr)
    y3 = _experts(xs3, tile_expert, used, wgu, wd, th, tr)
    return _combine(y3, dest, meta, x, seg, mod, g3)


def _proj_h_body(u, w_refs, out_refs):
    for w_ref, o_ref in zip(w_refs, out_refs):
        o_ref[...] = jnp.dot(u, w_ref[...], preferred_element_type=F32).astype(BF16)


def _proj_h_lat_kernel(x_ref, mod_ref, g_ref, *refs, rows_per_col):
    n = len(refs) // 2
    w_refs, out_refs = refs[:n], refs[n:]
    x = jnp.swapaxes(x_ref[...], 0, 1)
    x = x.reshape(SUBLANES * rows_per_col, x.shape[2])
    u = _normmod(x, g_ref[...], mod_ref[0], 0).astype(BF16)
    _proj_h_body(u, w_refs, out_refs)


def _proj_h_ctx_kernel(x_ref, mod_ref, g_ref, *refs):
    n = len(refs) // 2
    u = _normmod(x_ref[...], g_ref[...], mod_ref[0], 0).astype(BF16)
    _proj_h_body(u, refs[:n], refs[n:])


def _proj_h_lat(x, seg, mod, g, ws):
    n, d = x.shape
    grid_rows = seg.seqlen // GRID_W
    x3 = x.reshape(n // GRID_W, GRID_W, d)
    tcol = SUBLANES * grid_rows
    ncb = GRID_W // SUBLANES
    return pl.pallas_call(
        functools.partial(_proj_h_lat_kernel, rows_per_col=grid_rows),
        out_shape=[jax.ShapeDtypeStruct((n, w.shape[1]), BF16) for w in ws],
        grid=(seg.batch, ncb),
        in_specs=[pl.BlockSpec((grid_rows, SUBLANES, d), lambda b, c: (b, c, 0)),
                  pl.BlockSpec((1, N_MOD, d), lambda b, c: (b, 0, 0)), _full(g, 2)]
                 + [_full(w, 2) for w in ws],
        out_specs=[pl.BlockSpec((tcol, w.shape[1]), lambda b, c: (b * ncb + c, 0)) for w in ws],
        compiler_params=_cp("arbitrary", "arbitrary"),
        name="hgrn_proj_lat",
    )(x3, mod, g, *ws)


def _proj_h_ctx(x, seg, mod, g, ws):
    n, d = x.shape
    tm = seg.tile(512)
    return pl.pallas_call(
        _proj_h_ctx_kernel,
        out_shape=[jax.ShapeDtypeStruct((n, w.shape[1]), BF16) for w in ws],
        grid=(n // tm,),
        in_specs=[pl.BlockSpec((tm, d), lambda j: (j, 0)), seg.mod_spec(tm, d), _full(g, 1)]
                 + [_full(w, 1) for w in ws],
        out_specs=[pl.BlockSpec((tm, w.shape[1]), lambda j: (j, 0)) for w in ws],
        compiler_params=_cp("arbitrary"),
        name="hgrn_proj_ctx",
    )(x, mod, g, *ws)


HGRN_HEAD_GROUP = 4


def _hgrn_prologue(f_ref, bias_ref, lb_ref, reverse, d):
    lb = lb_ref[d:d + 1, :]
    f = lb + (1.0 - lb) * _sigmoid(f_ref[...].astype(F32) + bias_ref[d:d + 1, :])
    return f, _cumsum_time(jnp.log(f), _cumsum_mat(CHUNK, reverse))


def _hgrn_heads(q_ref, v_ref, pro, h_ref, s_scr, reverse, d, heads):
    L = CHUNK
    f, a_all = pro
    r = lax.broadcasted_iota(jnp.int32, (L, L), 0)
    c = lax.broadcasted_iota(jnp.int32, (L, L), 1)
    causal = (c >= r) if reverse else (c <= r)
    same64 = (r // 64) == (c // 64)
    diag32 = ((r // 32) == (c // 32)) & causal
    t = lax.broadcasted_iota(jnp.int32, (L, 1), 0)
    if reverse:
        late0, late1 = t < 64, (t % 64) < 32
        a0, a1 = (64,), (32, 96)
        a2 = (31, 63, 95, 127)
        last = 0
    else:
        late0, late1 = t >= 64, (t % 64) >= 32
        a0, a1 = (63,), (31, 95)
        a2 = (0, 32, 64, 96)
        last = L - 1
    staged = []
    for h in heads:
        sl = slice(h * H_DK, (h + 1) * H_DK)
        a = a_all[:, sl]
        q = q_ref[:, sl].astype(F32)
        k = 1.0 - f[:, sl]
        anc0 = jnp.broadcast_to(a[a0[0]:a0[0] + 1, :], (L, H_DK))
        anc1 = jnp.where(t < 64, a[a1[0]:a1[0] + 1, :], a[a1[1]:a1[1] + 1, :])
        anc2 = jnp.where(t < 32, a[a2[0]:a2[0] + 1, :],
                         jnp.where(t < 64, a[a2[1]:a2[1] + 1, :],
                                   jnp.where(t < 96, a[a2[2]:a2[2] + 1, :], a[a2[3]:a2[3] + 1, :])))

        def level(anc, late):
            qs = q * jnp.exp(a - anc)
            ks = k * jnp.exp(anc - a)
            if late is not None:
                qs = jnp.where(late, qs, 0.0)
                ks = jnp.where(late, 0.0, ks)
            return lax.dot_general(qs.astype(BF16), ks.astype(BF16), NT_DIMS, preferred_element_type=F32)

        s = jnp.where(diag32, level(anc2, None), jnp.where(same64, level(anc1, late1), level(anc0, late0)))
        qd = (q * jnp.exp(a)).astype(BF16)
        a_last = a[last:last + 1, :]
        kd = (k * jnp.exp(a_last - a)).astype(BF16)
        staged.append((s.astype(BF16), qd, kd, jnp.exp(a_last)))
    for h, (s, qd, kd, dec) in zip(heads, staged):
        v = v_ref[:, h * H_DV:(h + 1) * H_DV]
        st = s_scr[d, h]
        o = (jnp.dot(s, v, preferred_element_type=F32)
             + lax.dot_general(qd, st.astype(BF16), NT_DIMS, preferred_element_type=F32))
        h_ref[:, h * H_DV:(h + 1) * H_DV] = o.astype(BF16)
        s_scr[d, h] = st * dec + lax.dot_general(v, kd, TN_DIMS, preferred_element_type=F32)


def _hgrn_scan_kernel(qf, vf, ff, qb, vb, fb, bias_ref, lb_ref, s0_ref, hf_ref, hb_ref, st_ref, s_scr):
    n = pl.program_id(1)

    @pl.when(n == 0)
    def _():
        s_scr[...] = s0_ref[0]

    pro_f = _hgrn_prologue(ff, bias_ref, lb_ref, False, 0)
    pro_b = _hgrn_prologue(fb, bias_ref, lb_ref, True, 1)
    for h0 in range(0, H_HEADS, HGRN_HEAD_GROUP):
        heads = range(h0, h0 + HGRN_HEAD_GROUP)
        _hgrn_heads(qf, vf, pro_f, hf_ref, s_scr, False, 0, heads)
        _hgrn_heads(qb, vb, pro_b, hb_ref, s_scr, True, 1, heads)

    @pl.when(n == pl.num_programs(1) - 1)
    def _():
        st_ref[0] = s_scr[...]


def _hgrn_scan(q, v, f_f, f_b, bias, lb, s0, seg):
    L = CHUNK
    assert seg.seqlen % L == 0
    n = seg.seqlen // L
    fwd = lambda b, i: (b * n + i, 0)
    bwd = lambda b, i: (b * n + (n - 1 - i), 0)
    rows_, w = q.shape
    st = lambda b, i: (b, 0, 0, 0, 0)
    return pl.pallas_call(
        _hgrn_scan_kernel,
        out_shape=[jax.ShapeDtypeStruct((rows_, w), BF16), jax.ShapeDtypeStruct((rows_, w), BF16),
                   jax.ShapeDtypeStruct(s0.shape, F32)],
        grid=(seg.batch, n),
        in_specs=[pl.BlockSpec((L, w), fwd), pl.BlockSpec((L, w), fwd), pl.BlockSpec((L, w), fwd),
                  pl.BlockSpec((L, w), bwd), pl.BlockSpec((L, w), bwd), pl.BlockSpec((L, w), bwd),
                  _full(bias, 2), _full(lb, 2), pl.BlockSpec((1,) + s0.shape[1:], st)],
        out_specs=[pl.BlockSpec((L, w), fwd), pl.BlockSpec((L, w), bwd),
                   pl.BlockSpec((1,) + s0.shape[1:], st)],
        scratch_shapes=[pltpu.VMEM(s0.shape[1:], F32)],
        compiler_params=_cp("arbitrary", "arbitrary"),
        name="hgrn_scan",
    )(q, v, f_f, q, v, f_b, bias, lb, s0)


def _mix_out_col_kernel(hf_ref, hb_ref, o_ref, x_ref, mod_ref, gh_ref, w_ref, g1_ref, out_ref,
                        *, n_heads, rows_per_col):
    mod = mod_ref[0]
    y = _head_out(hf_ref[...], hb_ref[...], o_ref[...], gh_ref[...], w_ref[...], n_heads)
    upd = mod[2:3] * (_rms(y) * g1_ref[...])
    upd = jnp.swapaxes(upd.reshape(SUBLANES, rows_per_col, upd.shape[1]), 0, 1)
    out_ref[...] = x_ref[...] + upd


def _mix_out_col(hf, hb, o, x, seg, mod, gh, w, g1, n_heads):
    n, d = x.shape
    grid_rows = seg.seqlen // GRID_W
    tcol = SUBLANES * grid_rows
    ncb = GRID_W // SUBLANES
    x3 = x.reshape(n // GRID_W, GRID_W, d)
    hmap = lambda b, c: (b * ncb + c, 0)
    xmap = lambda b, c: (b, c, 0)
    consts = (gh, w, g1)
    out = pl.pallas_call(
        functools.partial(_mix_out_col_kernel, n_heads=n_heads, rows_per_col=grid_rows),
        out_shape=jax.ShapeDtypeStruct(x3.shape, F32),
        grid=(seg.batch, ncb),
        in_specs=[pl.BlockSpec((tcol, hf.shape[1]), hmap), pl.BlockSpec((tcol, hf.shape[1]), hmap),
                  pl.BlockSpec((tcol, o.shape[1]), hmap),
                  pl.BlockSpec((grid_rows, SUBLANES, d), xmap),
                  pl.BlockSpec((1, N_MOD, d), lambda b, c: (b, 0, 0))] + [_full(a, 2) for a in consts],
        out_specs=pl.BlockSpec((grid_rows, SUBLANES, d), xmap),
        input_output_aliases={3: 0},
        compiler_params=_cp("arbitrary", "arbitrary"),
        name="mix_out_col",
    )(hf, hb, o, x3, mod, *consts)
    return out.reshape(n, d)


def kernel(x, c, ctx, c_ctx, mod_w, mod_b, norm_g, m_w_in, m_b_gate, m_w_conv, m_g_head, m_w_out,
           h_w_in, h_b_f, h_lb_raw, h_g_head, h_w_out, f_w_gu, f_w_down,
           e_w_router, e_b_router, e_w_gu, e_w_down):
    batch, seq, d = x.shape
    ctx_len = ctx.shape[1]
    depth = mod_w.shape[0]
    assert seq % GRID_W == 0 and seq // GRID_W == CHUNK and ctx_len % CHUNK == 0
    assert batch < MOD_ROWS and d % LANES == 0
    lat_seg = _Seg(batch, seq, False)
    ctx_seg = _Seg(batch, ctx_len, True)

    c_all = jnp.zeros((MOD_ROWS, d), F32).at[:batch].set(c).at[batch].set(c_ctx)
    mod_all = _modulation(c_all, mod_w, mod_b).reshape(depth, MOD_ROWS, N_MOD, d)

    lb_all = jax.nn.softmax(h_lb_raw.astype(F32), axis=0)
    lb_all = jnp.cumsum(lb_all, axis=0) - lb_all[0]

    xl = x.reshape(batch * seq, d)
    xc = ctx.reshape(batch * ctx_len, d)
    row = lambda v: v.reshape(1, -1).astype(F32)

    for i in range(depth):
        j = i // 2
        with_ctx = i < depth - 1
        mod = mod_all[i]
        g0, g1, g2, g3 = (row(norm_g[i, s]) for s in range(4))
        if i % 2 == 0:
            qk_w = 2 * M_HEADS * M_DK
            v_w = M_HEADS * M_DV
            w_in = m_w_in[j]
            wqk = w_in[:, :qk_w].astype(BF16)
            wv = w_in[:, qk_w:qk_w + v_w].astype(BF16)
            wo = w_in[:, qk_w + v_w:qk_w + 2 * v_w].astype(BF16)
            ng = 4 * M_HEADS
            wg = jnp.zeros((d, LANES), F32).at[:, :ng].set(w_in[:, qk_w + 2 * v_w:]).astype(BF16)
            gb = jnp.zeros((1, LANES), F32).at[0, :ng].set(m_b_gate[j].reshape(-1))
            col = jnp.arange(LANES)
            gm = (((col // M_HEADS) % 2 == 1) & (col < ng)).astype(F32).reshape(1, LANES)
            pw = (mod, g0, wqk, wv, wo, wg, m_w_conv[j].astype(F32), gb, gm)
            qc, kc, vc, oc, gc = _proj_m(xc, ctx_seg, *pw)
            ql, kl, vl, ol, gl = _proj_m(xl, lat_seg, *pw)
            s0 = jnp.zeros((batch, 2, M_HEADS // 2, LANES, 2 * M_DV), F32)
            m0 = jnp.zeros((batch, 2 * M_HEADS, LANES), F32)
            hcf, hcb, s1, m1 = _mlstm_scan(qc, kc, vc, gc, s0, m0, ctx_seg)
            hlf, hlb, _, _ = _mlstm_scan(ql, kl, vl, gl, s1, m1, lat_seg)
            gh = row(m_g_head[j])
            wout = m_w_out[j].astype(BF16)
            xl = _mix_out(hlf, hlb, ol, xl, lat_seg, mod, gh, wout, g1, M_HEADS, in_place=i > 0)
            if with_ctx:
                xc = _mix_out(hcf, hcb, oc, xc, ctx_seg, mod, gh, wout, g1, M_HEADS, in_place=i > 0)
        else:
            kw = H_HEADS * H_DK
            w_in = h_w_in[j].astype(BF16)
            ws = [w_in[:, s * kw:(s + 1) * kw] for s in range(5)]
            qc, vc, ffc, fbc, gc = _proj_h_ctx(xc, ctx_seg, mod, g0, ws)
            ql, vl, ffl, fbl, gl = _proj_h_lat(xl, lat_seg, mod, g0, ws)
            bias = h_b_f[j].astype(F32)
            lb = lb_all[j]
            s0 = jnp.zeros((batch, 2, H_HEADS, H_DV, H_DK), F32)
            hcf, hcb, s1 = _hgrn_scan(qc, vc, ffc, fbc, bias, lb, s0, ctx_seg)
            hlf, hlb, _ = _hgrn_scan(ql, vl, ffl, fbl, bias, lb, s1, lat_seg)
            gh = row(h_g_head[j])
            wout = h_w_out[j].astype(BF16)
            xl = _mix_out_col(hlf, hlb, gl, xl, lat_seg, mod, gh, wout, g1, H_HEADS)
            if with_ctx:
                xc = _mix_out(hcf, hcb, gc, xc, ctx_seg, mod, gh, wout, g1, H_HEADS)
        streams = [(xl, lat_seg)] + ([(xc, ctx_seg)] if with_ctx else [])
        outs = []
        if i % 2 == 0:
            wgu, wd = f_w_gu[j].astype(BF16), f_w_down[j].astype(BF16)
            for xs, seg in streams:
                outs.append(_ffn(xs, seg, mod, g2, wgu, wd, g3, th=256))
        else:
            wr = jnp.zeros((d, LANES), F32).at[:, :N_EXPERTS].set(e_w_router[j])
            br = jnp.zeros((1, LANES), F32).at[0, :N_EXPERTS].set(e_b_router[j])
            wgu, wd = e_w_gu[j].astype(BF16), e_w_down[j].astype(BF16)
            for xs, seg in streams:
                outs.append(_moe_routed(xs, seg, mod, g2, wr, br, wgu, wd, g3, th=256))
        xl = outs[0]
        if with_ctx:
            xc = outs[1]
    return xl.reshape(batch, seq, d)
```

```python
import functools

import jax
import jax.numpy as jnp
from jax import lax
from jax.experimental import pallas as pl
from jax.experimental.pallas import tpu as pltpu

F32 = jnp.float32
BF16 = jnp.bfloat16
EPS = 1e-6
NEG = -1e30

GRID_W = 64
N_MOD = 6
M_HEADS, M_DK, M_DV = 8, 64, 128
H_HEADS, H_DK, H_DV = 8, 128, 128
N_EXPERTS = 8
LANES = 128
SUBLANES = 8
CHUNK = 128
MOD_ROWS = 16
TILE_ROWS = 512
TILE_ROWS_WIDE = 1024
HIDDEN_CHUNK = 256
VMEM_LIMIT = 56 * 1024 * 1024

HI = lax.Precision.HIGHEST
NT_DIMS = (((1,), (1,)), ((), ()))
TN_DIMS = (((0,), (0,)), ((), ()))


def _cp(*sem):
    return pltpu.CompilerParams(dimension_semantics=sem, vmem_limit_bytes=VMEM_LIMIT)


def _full(a, nargs):
    zeros = (0,) * a.ndim
    return pl.BlockSpec(a.shape, lambda *_: zeros)


def _sigmoid(x):
    return 0.5 * jnp.tanh(0.5 * x) + 0.5


def _silu(x):
    return x * _sigmoid(x)


def _log_sigmoid(x):
    return jnp.minimum(x, 0.0) - jnp.log(1.0 + jnp.exp(-jnp.abs(x)))


def _rms(x):
    return x * lax.rsqrt(jnp.mean(x * x, axis=-1, keepdims=True) + EPS)


def _normmod(x, g, mod, s):
    return _rms(x) * (g * (1.0 + mod[s + 1:s + 2])) + mod[s:s + 1]


def _split2(x):
    hi = x.astype(BF16)
    lo = (x - hi.astype(F32)).astype(BF16)
    return hi, lo


def _cumsum_mat(n, reverse):
    r = lax.broadcasted_iota(jnp.int32, (n, n), 0)
    c = lax.broadcasted_iota(jnp.int32, (n, n), 1)
    return jnp.where((c >= r) if reverse else (c <= r), 1.0, 0.0).astype(BF16)


def _cumsum_time(x, tri):
    hi, lo = _split2(x)
    return (jnp.dot(tri, hi, preferred_element_type=F32)
            + jnp.dot(tri, lo, preferred_element_type=F32))


class _Seg:
    def __init__(self, batch, seqlen, is_ctx):
        self.batch, self.seqlen, self.is_ctx = batch, seqlen, is_ctx
        self.n = batch * seqlen

    def tile(self, pref):
        tm = min(pref, self.n if self.is_ctx else self.seqlen)
        assert self.n % tm == 0 and (self.seqlen % tm == 0 or tm % self.seqlen == 0)
        return tm

    def mod_spec(self, tm, d):
        if self.is_ctx:
            return pl.BlockSpec((1, N_MOD, d), lambda j: (self.batch, 0, 0))
        return pl.BlockSpec((1, N_MOD, d), lambda j: ((j * tm) // self.seqlen, 0, 0))


def _mod_kernel(c_ref, w_ref, b_ref, o_ref):
    c = c_ref[...]
    o_ref[0] = jnp.dot(_silu(c), w_ref[0], preferred_element_type=F32, precision=HI) + b_ref[0]


def _modulation(c_all, mod_w, mod_b):
    depth, d, n = mod_w.shape
    tn = n // 4
    return pl.pallas_call(
        _mod_kernel,
        out_shape=jax.ShapeDtypeStruct((depth, MOD_ROWS, n), F32),
        grid=(depth, n // tn),
        in_specs=[pl.BlockSpec((MOD_ROWS, d), lambda i, j: (0, 0)),
                  pl.BlockSpec((1, d, tn), lambda i, j: (i, 0, j)),
                  pl.BlockSpec((1, 1, tn), lambda i, j: (i, 0, j))],
        out_specs=pl.BlockSpec((1, MOD_ROWS, tn), lambda i, j: (i, 0, j)),
        compiler_params=_cp("arbitrary", "arbitrary"),
        name="modulation",
    )(c_all, mod_w, mod_b.reshape(depth, 1, n))


def _proj_m_kernel(xp_ref, x_ref, xn_ref, mod_ref, g_ref, wqk_ref, wv_ref, wo_ref, wg_ref,
                   cw_ref, gb_ref, gm_ref, q_ref, kt_ref, v_ref, o_ref, gt_ref, *, tm, seqlen):
    j = pl.program_id(0)
    mod = mod_ref[0]
    g = g_ref[...]
    u = _normmod(x_ref[...], g, mod, 0).astype(BF16)
    uh = _normmod(jnp.concatenate([xp_ref[...], xn_ref[...]], axis=0), g, mod, 0).astype(BF16)
    wqk = wqk_ref[...]
    z = jnp.dot(u, wqk, preferred_element_type=F32)
    zh = jnp.dot(uh, wqk, preferred_element_type=F32)
    local = lax.broadcasted_iota(jnp.int32, (tm, 1), 0)
    pos = lax.rem(j * tm + local, seqlen)
    zp = jnp.where(local == 0, zh[SUBLANES - 1:SUBLANES], pltpu.roll(z, 1, 0))
    zn = jnp.where(local == tm - 1, zh[SUBLANES:SUBLANES + 1], pltpu.roll(z, tm - 1, 0))
    zp = jnp.where(pos == 0, 0.0, zp)
    zn = jnp.where(pos == seqlen - 1, 0.0, zn)
    cw = cw_ref[...]
    a = _silu(cw[0:1] * zp + cw[1:2] * z + cw[2:3] * zn)
    half = a.shape[1] // 2
    q_ref[...] = a[:, :half].astype(BF16)
    kt_ref[...] = (a[:, half:] * (M_DK ** -0.5)).T.astype(BF16)
    v_ref[...] = jnp.dot(u, wv_ref[...], preferred_element_type=F32).astype(BF16)
    o_ref[...] = jnp.dot(u, wo_ref[...], preferred_element_type=F32).astype(BF16)
    zg = jnp.dot(u, wg_ref[...], preferred_element_type=F32) + gb_ref[...]
    gt_ref[...] = jnp.where(gm_ref[...] > 0.5, _log_sigmoid(zg), zg)


def _proj_m(x, seg, mod, g, wqk, wv, wo, wg, cw, gb, gm):
    n, d = x.shape
    tm = seg.tile(TILE_ROWS)
    nb8 = n // SUBLANES
    r8 = tm // SUBLANES
    kern = functools.partial(_proj_m_kernel, tm=tm, seqlen=seg.seqlen)
    nqk = wqk.shape[1]
    consts = (g, wqk, wv, wo, wg, cw, gb, gm)
    return pl.pallas_call(
        kern,
        out_shape=[jax.ShapeDtypeStruct((n, nqk // 2), BF16),
                   jax.ShapeDtypeStruct((nqk // 2, n), BF16),
                   jax.ShapeDtypeStruct((n, wv.shape[1]), BF16),
                   jax.ShapeDtypeStruct((n, wo.shape[1]), BF16),
                   jax.ShapeDtypeStruct((n, LANES), F32)],
        grid=(n // tm,),
        in_specs=[pl.BlockSpec((SUBLANES, d), lambda j: (jnp.maximum(j * r8 - 1, 0), 0)),
                  pl.BlockSpec((tm, d), lambda j: (j, 0)),
                  pl.BlockSpec((SUBLANES, d), lambda j: (jnp.minimum((j + 1) * r8, nb8 - 1), 0)),
                  seg.mod_spec(tm, d)] + [_full(a, 1) for a in consts],
        out_specs=[pl.BlockSpec((tm, nqk // 2), lambda j: (j, 0)),
                   pl.BlockSpec((nqk // 2, tm), lambda j: (0, j)),
                   pl.BlockSpec((tm, wv.shape[1]), lambda j: (j, 0)),
                   pl.BlockSpec((tm, wo.shape[1]), lambda j: (j, 0)),
                   pl.BlockSpec((tm, LANES), lambda j: (j, 0))],
        compiler_params=_cp("arbitrary"),
        name="mlstm_proj",
    )(x, x, x, mod, *consts)


MLSTM_PAIR_GROUP = 4


def _mlstm_gates(g_ref, m_prev, reverse, d):
    L = CHUNK
    g = g_ref[...]
    b = _cumsum_time(g, _cumsum_mat(L, reverse))
    gT, bT = g.T, b.T
    c_ig = 2 * M_HEADS * d
    b_rows = bT[c_ig + M_HEADS:c_ig + 2 * M_HEADS, :]
    beta = gT[c_ig:c_ig + M_HEADS, :] - b_rows
    lane = lax.broadcasted_iota(jnp.int32, (1, L), 1)
    cm = beta
    k = 1
    while k < L:
        if reverse:
            cm = jnp.maximum(cm, jnp.where(lane < L - k, pltpu.roll(cm, L - k, 1), NEG))
        else:
            cm = jnp.maximum(cm, jnp.where(lane >= k, pltpu.roll(cm, k, 1), NEG))
        k *= 2
    last = 0 if reverse else L - 1
    mu = jnp.maximum(cm, m_prev)
    mu_last = mu[:, last:last + 1]
    rows = jnp.concatenate([beta, jnp.exp(beta - mu_last)], axis=0)
    cols = jnp.concatenate([mu, jnp.exp(m_prev - mu), jnp.exp(-(b_rows + mu)),
                            jnp.zeros((LANES - 3 * M_HEADS, L), F32)], axis=0).T
    half = lax.broadcasted_iota(jnp.int32, (1, LANES), 1) < LANES // 2
    small = jnp.where(half, jnp.exp(m_prev - mu_last), b_rows[:, last:last + 1] + mu_last)
    return rows, cols, small


def _mlstm_heads(q_ref, kt_ref, v_ref, rows_g, cols, small, h_ref, s_scr, reverse, d, pairs):
    L = CHUNK
    r = lax.broadcasted_iota(jnp.int32, (L, L), 0)
    c = lax.broadcasted_iota(jnp.int32, (L, L), 1)
    causal = (c >= r) if reverse else (c <= r)
    lane = lax.broadcasted_iota(jnp.int32, (1, LANES), 1)
    ones = jnp.ones((L, M_DV), BF16)
    staged = []
    for p in pairs:
        q_p = q_ref[:, p * LANES:(p + 1) * LANES]
        kt_p = kt_ref[p * LANES:(p + 1) * LANES, :]
        for e in range(2):
            h = 2 * p + e
            q_e = jnp.where((lane // M_DK) == e, q_p, jnp.zeros_like(q_p))
            qk = jnp.dot(q_e, kt_p, preferred_element_type=F32)
            s = jnp.where(causal, qk * jnp.exp(rows_g[h:h + 1, :] - cols[:, h:h + 1]), 0.0).astype(BF16)
            ktw = (kt_p[e * M_DK:(e + 1) * M_DK, :].astype(F32)
                   * rows_g[M_HEADS + h:M_HEADS + h + 1, :]).astype(BF16)
            staged.append((p, e, q_e, s, ktw))
    for p, e, q_e, s, ktw in staged:
        h = 2 * p + e
        rows = slice(e * M_DK, (e + 1) * M_DK)
        vp = jnp.concatenate([v_ref[:, h * M_DV:(h + 1) * M_DV], ones], axis=1)
        w_inter = cols[:, M_HEADS + h:M_HEADS + h + 1]
        floor = cols[:, 2 * M_HEADS + h:2 * M_HEADS + h + 1]
        tot = (jnp.dot(s, vp, preferred_element_type=F32)
               + w_inter * jnp.dot(q_e, s_scr[d, p].astype(BF16), preferred_element_type=F32))
        den = jnp.maximum(jnp.abs(tot[:, M_DV:]), floor)
        h_ref[:, h * M_DV:(h + 1) * M_DV] = (tot[:, :M_DV] / den).astype(BF16)
        s_scr[d, p, rows, :] = (small[h:h + 1, 0:1] * s_scr[d, p, rows, :]
                                + jnp.dot(ktw, vp, preferred_element_type=F32))


def _mlstm_scan_kernel(qf, kf, vf, gf, gf_next, qb, kb, vb, gb, gb_next, s0_ref, m0_ref,
                       hf_ref, hb_ref, st_ref, mt_ref, s_scr, m_scr, rows_scr, cols_scr, small_scr):
    n = pl.program_id(1)
    H = M_HEADS
    g_cur, g_next = (gf, gb), (gf_next, gb_next)

    @pl.when(n == 0)
    def _():
        s_scr[...] = s0_ref[0]
        for d in range(2):
            rows_scr[d], cols_scr[d], small_scr[d] = _mlstm_gates(
                g_cur[d], m0_ref[0, d * H:(d + 1) * H, 0:1], d == 1, d)

    cur = [(rows_scr[d], cols_scr[d], small_scr[d]) for d in range(2)]
    m_new = [cur[d][2][:, LANES // 2:LANES // 2 + 1] for d in range(2)]
    m_scr[...] = jnp.broadcast_to(jnp.concatenate(m_new, axis=0), m_scr.shape)
    for d in range(2):
        rows_scr[d], cols_scr[d], small_scr[d] = _mlstm_gates(g_next[d], m_new[d], d == 1, d)
    for p0 in range(0, M_HEADS // 2, MLSTM_PAIR_GROUP):
        pairs = range(p0, p0 + MLSTM_PAIR_GROUP)
        _mlstm_heads(qf, kf, vf, *cur[0], hf_ref, s_scr, False, 0, pairs)
        _mlstm_heads(qb, kb, vb, *cur[1], hb_ref, s_scr, True, 1, pairs)

    @pl.when(n == pl.num_programs(1) - 1)
    def _():
        st_ref[0] = s_scr[...]
        mt_ref[0] = m_scr[...]


def _mlstm_scan(q, kt, v, gates, s0, m0, seg):
    L = CHUNK
    assert seg.seqlen % L == 0
    n = seg.seqlen // L
    fwd = lambda b, i: (b * n + i, 0)
    bwd = lambda b, i: (b * n + (n - 1 - i), 0)
    fwd_next = lambda b, i: (b * n + jnp.minimum(i + 1, n - 1), 0)
    bwd_next = lambda b, i: (b * n + jnp.maximum(n - 2 - i, 0), 0)
    fwd_t = lambda b, i: (0, b * n + i)
    bwd_t = lambda b, i: (0, b * n + (n - 1 - i))
    dq, dv = q.shape[1], v.shape[1]
    st = lambda b, i: (b, 0, 0, 0, 0)
    mt = lambda b, i: (b, 0, 0)
    return pl.pallas_call(
        _mlstm_scan_kernel,
        out_shape=[jax.ShapeDtypeStruct((seg.n, dv), BF16), jax.ShapeDtypeStruct((seg.n, dv), BF16),
                   jax.ShapeDtypeStruct(s0.shape, F32), jax.ShapeDtypeStruct(m0.shape, F32)],
        grid=(seg.batch, n),
        in_specs=[pl.BlockSpec((L, dq), fwd), pl.BlockSpec((dq, L), fwd_t), pl.BlockSpec((L, dv), fwd),
                  pl.BlockSpec((L, LANES), fwd), pl.BlockSpec((L, LANES), fwd_next),
                  pl.BlockSpec((L, dq), bwd), pl.BlockSpec((dq, L), bwd_t), pl.BlockSpec((L, dv), bwd),
                  pl.BlockSpec((L, LANES), bwd), pl.BlockSpec((L, LANES), bwd_next),
                  pl.BlockSpec((1,) + s0.shape[1:], st), pl.BlockSpec((1,) + m0.shape[1:], mt)],
        out_specs=[pl.BlockSpec((L, dv), fwd), pl.BlockSpec((L, dv), bwd),
                   pl.BlockSpec((1,) + s0.shape[1:], st), pl.BlockSpec((1,) + m0.shape[1:], mt)],
        scratch_shapes=[pltpu.VMEM(s0.shape[1:], F32), pltpu.VMEM(m0.shape[1:], F32),
                        pltpu.VMEM((2, 2 * M_HEADS, L), F32), pltpu.VMEM((2, L, LANES), F32),
                        pltpu.VMEM((2, M_HEADS, LANES), F32)],
        compiler_params=_cp("arbitrary", "arbitrary"),
        name="mlstm_scan",
    )(q, kt, v, gates, gates, q, kt, v, gates, gates, s0, m0)


def _head_out(hf, hb, gate_pre, gh, w, n_heads):
    h = hf.astype(F32) + hb.astype(F32)
    dv = h.shape[1] // n_heads
    parts = [_rms(h[:, i * dv:(i + 1) * dv]) for i in range(n_heads)]
    hn = jnp.concatenate(parts, axis=1) * gh * _sigmoid(gate_pre.astype(F32))
    return jnp.dot(hn.astype(BF16), w, preferred_element_type=F32)


def _mix_out_kernel(hf_ref, hb_ref, o_ref, x_ref, mod_ref, gh_ref, w_ref, g1_ref, out_ref, *, n_heads):
    mod = mod_ref[0]
    y = _head_out(hf_ref[...], hb_ref[...], o_ref[...], gh_ref[...], w_ref[...], n_heads)
    out_ref[...] = x_ref[...] + mod[2:3] * (_rms(y) * g1_ref[...])


def _mix_out(hf, hb, o, x, seg, mod, gh, w, g1, n_heads, in_place=True):
    n, d = x.shape
    tm = seg.tile(TILE_ROWS)
    rmap = lambda j: (j, 0)
    consts = (gh, w, g1)
    return pl.pallas_call(
        functools.partial(_mix_out_kernel, n_heads=n_heads),
        out_shape=jax.ShapeDtypeStruct(x.shape, F32),
        grid=(n // tm,),
        in_specs=[pl.BlockSpec((tm, hf.shape[1]), rmap), pl.BlockSpec((tm, hf.shape[1]), rmap),
                  pl.BlockSpec((tm, o.shape[1]), rmap), pl.BlockSpec((tm, d), rmap),
                  seg.mod_spec(tm, d)] + [_full(a, 1) for a in consts],
        out_specs=pl.BlockSpec((tm, d), rmap),
        input_output_aliases={3: 0} if in_place else {},
        compiler_params=_cp("arbitrary"),
        name="mix_out",
    )(hf, hb, o, x, mod, *consts)


def _swiglu_acc(u_scr, wgu_ref, wd_ref, acc_scr, th):
    hd = wd_ref.shape[0]
    assert hd % th == 0 and hd % LANES == 0

    def part(off):
        u = u_scr[...]
        gg = jnp.dot(u, wgu_ref[:, pl.ds(off, th)], preferred_element_type=F32)
        up = jnp.dot(u, wgu_ref[:, pl.ds(pl.multiple_of(hd + off, LANES), th)], preferred_element_type=F32)
        hh = (_silu(gg) * up).astype(BF16)
        return jnp.dot(hh, wd_ref[pl.ds(off, th), :], preferred_element_type=F32)

    def chunk(k, carry):
        acc_scr[...] += part(pl.multiple_of(k * th, th))
        return carry

    assert hd // th >= 2
    acc_scr[...] = part(0)
    lax.fori_loop(1, hd // th - 1, chunk, 0)
    return acc_scr[...] + part(hd - th)


def _ffn_kernel(x_ref, mod_ref, g2_ref, wgu_ref, wd_ref, g3_ref, out_ref, u_scr, acc_scr, *, th):
    u_scr[...] = _normmod(x_ref[...], g2_ref[...], mod_ref[0], 3).astype(BF16)
    y = _swiglu_acc(u_scr, wgu_ref, wd_ref, acc_scr, th)
    mod = mod_ref[0]
    out_ref[...] = x_ref[...] + mod[5:6] * (_rms(y) * g3_ref[...])


def _resident(a, index_map):
    return pl.BlockSpec(a, index_map, pipeline_mode=pl.Buffered(1))


def _ffn(x, seg, mod, g2, wgu, wd, g3, th):
    n, d = x.shape
    tm = seg.tile(TILE_ROWS_WIDE)
    return pl.pallas_call(
        functools.partial(_ffn_kernel, th=th),
        out_shape=jax.ShapeDtypeStruct((n, d), F32),
        grid=(n // tm,),
        in_specs=[pl.BlockSpec((tm, d), lambda i: (i, 0)),
                  seg.mod_spec(tm, d),
                  _full(g2, 1),
                  _resident(wgu.shape, lambda i: (0, 0)),
                  _resident(wd.shape, lambda i: (0, 0)),
                  _full(g3, 1)],
        out_specs=pl.BlockSpec((tm, d), lambda i: (i, 0)),
        scratch_shapes=[pltpu.VMEM((tm, d), BF16), pltpu.VMEM((tm, d), F32)],
        input_output_aliases={0: 0},
        compiler_params=_cp("arbitrary"),
        name="ffn",
    )(x, mod, g2, wgu, wd, g3)


META_E, META_RANK, META_W = 0, 2, 4


def _route_kernel(x_ref, mod_ref, g2_ref, wr_ref, br_ref, meta_ref, cnt_ref, tri_scr, run_scr):
    i = pl.program_id(0)
    tm = x_ref.shape[0]

    @pl.when(i == 0)
    def _():
        r = lax.broadcasted_iota(jnp.int32, (tm, tm), 0)
        c = lax.broadcasted_iota(jnp.int32, (tm, tm), 1)
        tri_scr[...] = jnp.where(c < r, 1.0, 0.0).astype(BF16)
        run_scr[...] = jnp.zeros_like(run_scr)

    u = _normmod(x_ref[...], g2_ref[...], mod_ref[0], 3)
    lane = lax.broadcasted_iota(jnp.int32, (1, LANES), 1)
    u_hi, u_lo = _split2(u)
    w_hi, w_lo = _split2(wr_ref[...])
    logits = (jnp.dot(u_hi, w_hi, preferred_element_type=F32) + jnp.dot(u_lo, w_hi, preferred_element_type=F32)
              + jnp.dot(u_hi, w_lo, preferred_element_type=F32)) + br_ref[...]
    logits = jnp.where(lane < N_EXPERTS, logits, NEG)
    m1 = jnp.max(logits, axis=1, keepdims=True)
    i1 = jnp.min(jnp.where(logits == m1, lane, LANES), axis=1, keepdims=True)
    rest = jnp.where(lane == i1, NEG, logits)
    m2 = jnp.max(rest, axis=1, keepdims=True)
    i2 = jnp.min(jnp.where(rest == m2, lane, LANES), axis=1, keepdims=True)
    e2 = jnp.exp(m2 - m1)
    w1 = 1.0 / (1.0 + e2)
    sel = ((lane == i1) | (lane == i2))
    onehot = jnp.where(sel, 1.0, 0.0)
    before = jnp.dot(tri_scr[...], onehot.astype(BF16), preferred_element_type=F32) + run_scr[...]
    rank1 = jnp.sum(jnp.where(lane == i1, before, 0.0), axis=1, keepdims=True)
    rank2 = jnp.sum(jnp.where(lane == i2, before, 0.0), axis=1, keepdims=True)
    run_scr[...] = run_scr[...] + jnp.sum(onehot, axis=0, keepdims=True)
    vals = (i1.astype(F32), i2.astype(F32), rank1, rank2, w1, e2 * w1)
    meta = jnp.zeros((tm, LANES), F32)
    for col, val in enumerate(vals):
        meta = jnp.where(lane == col, val, meta)
    meta_ref[...] = meta
    cnt_ref[...] = jnp.broadcast_to(run_scr[...], cnt_ref.shape)


def _route(x, seg, mod, g2, wr, br):
    n, d = x.shape
    tm = seg.tile(TILE_ROWS_WIDE)
    consts = (g2, wr, br)
    return pl.pallas_call(
        _route_kernel,
        out_shape=[jax.ShapeDtypeStruct((n, LANES), F32),
                   jax.ShapeDtypeStruct((SUBLANES, LANES), F32)],
        grid=(n // tm,),
        in_specs=[pl.BlockSpec((tm, d), lambda i: (i, 0)), seg.mod_spec(tm, d)]
                 + [_full(a, 1) for a in consts],
        out_specs=[pl.BlockSpec((tm, LANES), lambda i: (i, 0)),
                   pl.BlockSpec((SUBLANES, LANES), lambda i: (0, 0))],
        scratch_shapes=[pltpu.VMEM((tm, tm), BF16), pltpu.VMEM((1, LANES), F32)],
        compiler_params=_cp("arbitrary"),
        name="route",
    )(x, mod, *consts)


def _drain(src_ref, dst_ref, sem, count):
    def body(i, carry):
        pltpu.make_async_copy(src_ref, dst_ref, sem).wait()
        return carry

    lax.fori_loop(0, count, body, 0, unroll=8)


def _row_tile(buf, t):
    return buf.at[pl.ds(pl.multiple_of(t * SUBLANES, SUBLANES), SUBLANES), :]


def _dispatch_kernel(dst_ref, pad_ref, x_ref, mod_ref, g2_ref, xs_ref, ubuf, sems):
    j = pl.program_id(0)
    tm = x_ref.shape[0]
    slot = lax.rem(j, 2)
    buf = ubuf.at[slot]
    sem = sems.at[slot]
    u = _normmod(x_ref[...], g2_ref[...], mod_ref[0], 3)
    for s in range(SUBLANES):
        ubuf[slot, pl.ds(s, tm, stride=SUBLANES), :] = u[:, s * LANES:(s + 1) * LANES]

    def issue(t, carry):
        pltpu.make_async_copy(_row_tile(buf, t), xs_ref.at[dst_ref[0, 0, t]], sem).start(priority=0)
        pltpu.make_async_copy(_row_tile(buf, t), xs_ref.at[dst_ref[0, 0, tm + t]], sem).start(priority=1)
        return carry

    lax.fori_loop(0, tm, issue, 0, unroll=8)
    npad = pad_ref.shape[2]
    first = _row_tile(buf, 0)

    @pl.when(j == 0)
    def _():
        def issue_pad(p, carry):
            pltpu.make_async_copy(first, xs_ref.at[pad_ref[0, 0, p]], sem).start()
            return carry

        lax.fori_loop(0, npad, issue_pad, 0, unroll=8)
        _drain(first, xs_ref.at[0], sem, npad)

    @pl.when(j > 0)
    def _():
        _drain(_row_tile(ubuf.at[1 - slot], 0), xs_ref.at[0], sems.at[1 - slot], 2 * tm)

    @pl.when(j == pl.num_programs(0) - 1)
    def _():
        _drain(first, xs_ref.at[0], sem, 2 * tm)


def _dispatch(x, seg, mod, g2, dest, pad_dst, n_rows):
    n, d = x.shape
    tm = seg.tile(TILE_ROWS)
    steps = n // tm
    dst = dest.reshape(steps, tm, 2).transpose(0, 2, 1).reshape(steps, 1, 2 * tm)
    npad = pad_dst.shape[0]
    return pl.pallas_call(
        _dispatch_kernel,
        out_shape=jax.ShapeDtypeStruct((n_rows, SUBLANES, d // SUBLANES), F32),
        grid=(steps,),
        in_specs=[pl.BlockSpec((1, 1, 2 * tm), lambda j: (j, 0, 0), memory_space=pltpu.SMEM),
                  pl.BlockSpec((1, 1, npad), lambda j: (0, 0, 0), memory_space=pltpu.SMEM),
                  pl.BlockSpec((tm, d), lambda j: (j, 0)), seg.mod_spec(tm, d), _full(g2, 1)],
        out_specs=pl.BlockSpec(memory_space=pl.ANY),
        scratch_shapes=[pltpu.VMEM((2, tm * SUBLANES, d // SUBLANES), F32), pltpu.SemaphoreType.DMA((2,))],
        compiler_params=_cp("arbitrary"),
        name="dispatch",
    )(dst, pad_dst.reshape(1, 1, npad), x, mod, g2)


def _experts_kernel(te_ref, used_ref, xs_ref, wgu_ref, wd_ref, y_ref, u_scr, acc_scr, *, th):
    i = pl.program_id(0)

    @pl.when(i < used_ref[0])
    def _():
        tr = u_scr.shape[0]
        for s in range(SUBLANES):
            u_scr[:, s * LANES:(s + 1) * LANES] = xs_ref[pl.ds(s, tr, stride=SUBLANES), :].astype(BF16)
        y = _swiglu_acc(u_scr, wgu_ref.at[0], wd_ref.at[0], acc_scr, th)
        for s in range(SUBLANES):
            y_ref[pl.ds(s, tr, stride=SUBLANES), :] = y[:, s * LANES:(s + 1) * LANES]


def _experts(xs3, tile_expert, used, wgu, wd, th, tr):
    n_rows = xs3.shape[0]
    ne, hd, d = wd.shape
    tiles = tile_expert.shape[0]
    assert tiles * tr <= n_rows
    row = lambda i, te, us: (jnp.minimum(i, us[0] - 1), 0)
    grid_spec = pltpu.PrefetchScalarGridSpec(
        num_scalar_prefetch=2,
        grid=(tiles,),
        in_specs=[pl.BlockSpec((tr * SUBLANES, LANES), row),
                  _resident((1,) + wgu.shape[1:], lambda i, te, us: (te[i], 0, 0)),
                  _resident((1,) + wd.shape[1:], lambda i, te, us: (te[i], 0, 0))],
        out_specs=pl.BlockSpec((tr * SUBLANES, LANES), row),
        scratch_shapes=[pltpu.VMEM((tr, d), BF16), pltpu.VMEM((tr, d), F32)],
    )
    y = pl.pallas_call(
        functools.partial(_experts_kernel, th=th),
        out_shape=jax.ShapeDtypeStruct((tiles * tr * SUBLANES, LANES), F32),
        grid_spec=grid_spec,
        compiler_params=_cp("arbitrary"),
        name="experts",
    )(tile_expert, used, xs3.reshape(n_rows * SUBLANES, LANES), wgu, wd)
    return y.reshape(tiles * tr, SUBLANES, LANES)


def _combine_kernel(src_ref, nxt_ref, y_ref, meta_ref, x_ref, mod_ref, g3_ref, out_ref, ybuf, sems):
    j = pl.program_id(0)
    tm = x_ref.shape[0]
    slot = lax.rem(j, 2)

    def gather(idx_ref, sl):
        def issue(t, carry):
            pltpu.make_async_copy(y_ref.at[idx_ref[0, 0, t]], _row_tile(ybuf.at[sl, 0], t),
                                  sems.at[sl]).start(priority=0)
            pltpu.make_async_copy(y_ref.at[idx_ref[0, 0, tm + t]], _row_tile(ybuf.at[sl, 1], t),
                                  sems.at[sl]).start(priority=1)
            return carry

        lax.fori_loop(0, tm, issue, 0, unroll=8)

    @pl.when(j == 0)
    def _():
        gather(src_ref, 0)

    @pl.when(j + 1 < pl.num_programs(0))
    def _():
        gather(nxt_ref, 1 - slot)

    _drain(y_ref.at[0], _row_tile(ybuf.at[slot, 0], 0), sems.at[slot], 2 * tm)
    meta = meta_ref[...]
    w1 = meta[:, META_W:META_W + 1]
    w2 = meta[:, META_W + 1:META_W + 2]
    ys = [w1 * ybuf[slot, 0, pl.ds(s, tm, stride=SUBLANES), :]
          + w2 * ybuf[slot, 1, pl.ds(s, tm, stride=SUBLANES), :] for s in range(SUBLANES)]
    ss = ys[0] * ys[0]
    for y in ys[1:]:
        ss = ss + y * y
    d = x_ref.shape[1]
    r = lax.rsqrt(jnp.sum(ss, axis=1, keepdims=True) / d + EPS)
    mod = mod_ref[0]
    for s, y in enumerate(ys):
        sl = slice(s * LANES, (s + 1) * LANES)
        out_ref[:, sl] = x_ref[:, sl] + mod[5:6, sl] * (y * r * g3_ref[:, sl])


def _combine(y3, dest, meta, x, seg, mod, g3):
    n, d = x.shape
    tm = seg.tile(TILE_ROWS)
    steps = n // tm
    src = dest.reshape(steps, tm, 2).transpose(0, 2, 1).reshape(steps, 1, 2 * tm)
    return pl.pallas_call(
        _combine_kernel,
        out_shape=jax.ShapeDtypeStruct((n, d), F32),
        grid=(steps,),
        in_specs=[pl.BlockSpec((1, 1, 2 * tm), lambda j: (j, 0, 0), memory_space=pltpu.SMEM),
                  pl.BlockSpec((1, 1, 2 * tm), lambda j: (jnp.minimum(j + 1, steps - 1), 0, 0),
                               memory_space=pltpu.SMEM),
                  pl.BlockSpec(memory_space=pl.ANY),
                  pl.BlockSpec((tm, LANES), lambda j: (j, 0)),
                  pl.BlockSpec((tm, d), lambda j: (j, 0)),
                  seg.mod_spec(tm, d), _full(g3, 1)],
        out_specs=pl.BlockSpec((tm, d), lambda j: (j, 0)),
        scratch_shapes=[pltpu.VMEM((2, 2, tm * SUBLANES, d // SUBLANES), F32), pltpu.SemaphoreType.DMA((2,))],
        input_output_aliases={4: 0},
        compiler_params=_cp("arbitrary"),
        name="combine",
    )(src, src, y3, meta, x, mod, g3)


def _moe_routed(x, seg, mod, g2, wr, br, wgu, wd, g3, th):
    n, d = x.shape
    assert d == SUBLANES * LANES
    tr = max(256, min(1024, pl.next_power_of_2((2 * n) // N_EXPERTS) // 2))
    meta, cnt = _route(x, seg, mod, g2, wr, br)
    counts = cnt[0, :N_EXPERTS].astype(jnp.int32)
    sizes = ((counts + tr - 1) // tr) * tr
    ends = jnp.cumsum(sizes)
    offs = ends - sizes
    mi = meta[:, :META_W].astype(jnp.int32)
    eidx = mi[:, META_E:META_E + 2]
    dest = jnp.sum(jnp.where(eidx[:, :, None] == jnp.arange(N_EXPERTS)[None, None, :],
                             offs[None, None, :], 0), axis=2) + mi[:, META_RANK:META_RANK + 2]
    dest = dest.astype(jnp.int32)
    tiles = (2 * n) // tr + N_EXPERTS
    n_rows = tiles * tr
    used = (ends[-1] // tr).astype(jnp.int32).reshape(1)
    tile_start = jnp.arange(tiles, dtype=jnp.int32) * tr
    tile_expert = jnp.sum(tile_start[:, None] >= ends[None, :], axis=1).astype(jnp.int32)
    last_expert = jnp.sum(jnp.maximum(ends[-1] - tr, 0) >= ends).astype(jnp.int32)
    tile_expert = jnp.where(tile_start < ends[-1], tile_expert, last_expert)
    pad = jnp.arange(N_EXPERTS * tr, dtype=jnp.int32).reshape(N_EXPERTS, tr)
    pad_row = offs[:, None] + counts[:, None] + jnp.arange(tr, dtype=jnp.int32)[None, :]
    pad_dst = jnp.where(pad_row < ends[:, None], pad_row, n_rows + pad).reshape(-1).astype(jnp.int32)
    xs3 = _dispatch(x, seg, mod, g2, dest, pad_dst, n_rows + N_EXPERTS * tr)
    y3 = _experts(xs3, tile_expert, used, wgu, wd, th, tr)
    return _combine(y3, dest, meta, x, seg, mod, g3)


def _proj_h_body(u, w_refs, out_refs):
    for w_ref, o_ref in zip(w_refs, out_refs):
        o_ref[...] = jnp.dot(u, w_ref[...], preferred_element_type=F32).astype(BF16)


def _proj_h_lat_kernel(x_ref, mod_ref, g_ref, *refs, rows_per_col):
    n = len(refs) // 2
    w_refs, out_refs = refs[:n], refs[n:]
    x = jnp.swapaxes(x_ref[...], 0, 1)
    x = x.reshape(SUBLANES * rows_per_col, x.shape[2])
    u = _normmod(x, g_ref[...], mod_ref[0], 0).astype(BF16)
    _proj_h_body(u, w_refs, out_refs)


def _proj_h_ctx_kernel(x_ref, mod_ref, g_ref, *refs):
    n = len(refs) // 2
    u = _normmod(x_ref[...], g_ref[...], mod_ref[0], 0).astype(BF16)
    _proj_h_body(u, refs[:n], refs[n:])


def _proj_h_lat(x, seg, mod, g, ws):
    n, d = x.shape
    grid_rows = seg.seqlen // GRID_W
    x3 = x.reshape(n // GRID_W, GRID_W, d)
    tcol = SUBLANES * grid_rows
    ncb = GRID_W // SUBLANES
    return pl.pallas_call(
        functools.partial(_proj_h_lat_kernel, rows_per_col=grid_rows),
        out_shape=[jax.ShapeDtypeStruct((n, w.shape[1]), BF16) for w in ws],
        grid=(seg.batch, ncb),
        in_specs=[pl.BlockSpec((grid_rows, SUBLANES, d), lambda b, c: (b, c, 0)),
                  pl.BlockSpec((1, N_MOD, d), lambda b, c: (b, 0, 0)), _full(g, 2)]
                 + [_full(w, 2) for w in ws],
        out_specs=[pl.BlockSpec((tcol, w.shape[1]), lambda b, c: (b * ncb + c, 0)) for w in ws],
        compiler_params=_cp("arbitrary", "arbitrary"),
        name="hgrn_proj_lat",
    )(x3, mod, g, *ws)


def _proj_h_ctx(x, seg, mod, g, ws):
    n, d = x.shape
    tm = seg.tile(TILE_ROWS)
    return pl.pallas_call(
        _proj_h_ctx_kernel,
        out_shape=[jax.ShapeDtypeStruct((n, w.shape[1]), BF16) for w in ws],
        grid=(n // tm,),
        in_specs=[pl.BlockSpec((tm, d), lambda j: (j, 0)), seg.mod_spec(tm, d), _full(g, 1)]
                 + [_full(w, 1) for w in ws],
        out_specs=[pl.BlockSpec((tm, w.shape[1]), lambda j: (j, 0)) for w in ws],
        compiler_params=_cp("arbitrary"),
        name="hgrn_proj_ctx",
    )(x, mod, g, *ws)


HGRN_HEAD_GROUP = 4


def _hgrn_prologue(f_ref, bias_ref, lb_ref, reverse, d):
    lb = lb_ref[d:d + 1, :]
    f = lb + (1.0 - lb) * _sigmoid(f_ref[...].astype(F32) + bias_ref[d:d + 1, :])
    return f, _cumsum_time(jnp.log(f), _cumsum_mat(CHUNK, reverse))


def _hgrn_heads(q_ref, v_ref, pro, h_ref, s_scr, reverse, d, heads):
    L = CHUNK
    f, a_all = pro
    r = lax.broadcasted_iota(jnp.int32, (L, L), 0)
    c = lax.broadcasted_iota(jnp.int32, (L, L), 1)
    causal = (c >= r) if reverse else (c <= r)
    same64 = (r // 64) == (c // 64)
    diag32 = ((r // 32) == (c // 32)) & causal
    t = lax.broadcasted_iota(jnp.int32, (L, 1), 0)
    if reverse:
        late0, late1 = t < 64, (t % 64) < 32
        a0, a1 = (64,), (32, 96)
        a2 = (31, 63, 95, 127)
        last = 0
    else:
        late0, late1 = t >= 64, (t % 64) >= 32
        a0, a1 = (63,), (31, 95)
        a2 = (0, 32, 64, 96)
        last = L - 1
    staged = []
    for h in heads:
        sl = slice(h * H_DK, (h + 1) * H_DK)
        a = a_all[:, sl]
        q = q_ref[:, sl].astype(F32)
        k = 1.0 - f[:, sl]
        anc0 = jnp.broadcast_to(a[a0[0]:a0[0] + 1, :], (L, H_DK))
        anc1 = jnp.where(t < 64, a[a1[0]:a1[0] + 1, :], a[a1[1]:a1[1] + 1, :])
        anc2 = jnp.where(t < 32, a[a2[0]:a2[0] + 1, :],
                         jnp.where(t < 64, a[a2[1]:a2[1] + 1, :],
                                   jnp.where(t < 96, a[a2[2]:a2[2] + 1, :], a[a2[3]:a2[3] + 1, :])))

        def level(anc, late):
            qs = q * jnp.exp(a - anc)
            ks = k * jnp.exp(anc - a)
            if late is not None:
                qs = jnp.where(late, qs, 0.0)
                ks = jnp.where(late, 0.0, ks)
            return lax.dot_general(qs.astype(BF16), ks.astype(BF16), NT_DIMS, preferred_element_type=F32)

        s = jnp.where(diag32, level(anc2, None), jnp.where(same64, level(anc1, late1), level(anc0, late0)))
        qd = (q * jnp.exp(a)).astype(BF16)
        a_last = a[last:last + 1, :]
        kd = (k * jnp.exp(a_last - a)).astype(BF16)
        staged.append((s.astype(BF16), qd, kd, jnp.exp(a_last)))
    for h, (s, qd, kd, dec) in zip(heads, staged):
        v = v_ref[:, h * H_DV:(h + 1) * H_DV]
        st = s_scr[d, h]
        o = (jnp.dot(s, v, preferred_element_type=F32)
             + lax.dot_general(qd, st.astype(BF16), NT_DIMS, preferred_element_type=F32))
        h_ref[:, h * H_DV:(h + 1) * H_DV] = o.astype(BF16)
        s_scr[d, h] = st * dec + lax.dot_general(v, kd, TN_DIMS, preferred_element_type=F32)


def _hgrn_scan_kernel(qf, vf, ff, qb, vb, fb, bias_ref, lb_ref, s0_ref, hf_ref, hb_ref, st_ref, s_scr):
    n = pl.program_id(1)

    @pl.when(n == 0)
    def _():
        s_scr[...] = s0_ref[0]

    pro_f = _hgrn_prologue(ff, bias_ref, lb_ref, False, 0)
    pro_b = _hgrn_prologue(fb, bias_ref, lb_ref, True, 1)
    for h0 in range(0, H_HEADS, HGRN_HEAD_GROUP):
        heads = range(h0, h0 + HGRN_HEAD_GROUP)
        _hgrn_heads(qf, vf, pro_f, hf_ref, s_scr, False, 0, heads)
        _hgrn_heads(qb, vb, pro_b, hb_ref, s_scr, True, 1, heads)

    @pl.when(n == pl.num_programs(1) - 1)
    def _():
        st_ref[0] = s_scr[...]


def _hgrn_scan(q, v, f_f, f_b, bias, lb, s0, seg):
    L = CHUNK
    assert seg.seqlen % L == 0
    n = seg.seqlen // L
    fwd = lambda b, i: (b * n + i, 0)
    bwd = lambda b, i: (b * n + (n - 1 - i), 0)
    rows_, w = q.shape
    st = lambda b, i: (b, 0, 0, 0, 0)
    return pl.pallas_call(
        _hgrn_scan_kernel,
        out_shape=[jax.ShapeDtypeStruct((rows_, w), BF16), jax.ShapeDtypeStruct((rows_, w), BF16),
                   jax.ShapeDtypeStruct(s0.shape, F32)],
        grid=(seg.batch, n),
        in_specs=[pl.BlockSpec((L, w), fwd), pl.BlockSpec((L, w), fwd), pl.BlockSpec((L, w), fwd),
                  pl.BlockSpec((L, w), bwd), pl.BlockSpec((L, w), bwd), pl.BlockSpec((L, w), bwd),
                  _full(bias, 2), _full(lb, 2), pl.BlockSpec((1,) + s0.shape[1:], st)],
        out_specs=[pl.BlockSpec((L, w), fwd), pl.BlockSpec((L, w), bwd),
                   pl.BlockSpec((1,) + s0.shape[1:], st)],
        scratch_shapes=[pltpu.VMEM(s0.shape[1:], F32)],
        compiler_params=_cp("arbitrary", "arbitrary"),
        name="hgrn_scan",
    )(q, v, f_f, q, v, f_b, bias, lb, s0)


def _mix_out_col_kernel(hf_ref, hb_ref, o_ref, x_ref, mod_ref, gh_ref, w_ref, g1_ref, out_ref,
                        *, n_heads, rows_per_col):
    mod = mod_ref[0]
    y = _head_out(hf_ref[...], hb_ref[...], o_ref[...], gh_ref[...], w_ref[...], n_heads)
    upd = mod[2:3] * (_rms(y) * g1_ref[...])
    upd = jnp.swapaxes(upd.reshape(SUBLANES, rows_per_col, upd.shape[1]), 0, 1)
    out_ref[...] = x_ref[...] + upd


def _mix_out_col(hf, hb, o, x, seg, mod, gh, w, g1, n_heads):
    n, d = x.shape
    grid_rows = seg.seqlen // GRID_W
    tcol = SUBLANES * grid_rows
    ncb = GRID_W // SUBLANES
    x3 = x.reshape(n // GRID_W, GRID_W, d)
    hmap = lambda b, c: (b * ncb + c, 0)
    xmap = lambda b, c: (b, c, 0)
    consts = (gh, w, g1)
    out = pl.pallas_call(
        functools.partial(_mix_out_col_kernel, n_heads=n_heads, rows_per_col=grid_rows),
        out_shape=jax.ShapeDtypeStruct(x3.shape, F32),
        grid=(seg.batch, ncb),
        in_specs=[pl.BlockSpec((tcol, hf.shape[1]), hmap), pl.BlockSpec((tcol, hf.shape[1]), hmap),
                  pl.BlockSpec((tcol, o.shape[1]), hmap),
                  pl.BlockSpec((grid_rows, SUBLANES, d), xmap),
                  pl.BlockSpec((1, N_MOD, d), lambda b, c: (b, 0, 0))] + [_full(a, 2) for a in consts],
        out_specs=pl.BlockSpec((grid_rows, SUBLANES, d), xmap),
        input_output_aliases={3: 0},
        compiler_params=_cp("arbitrary", "arbitrary"),
        name="mix_out_col",
    )(hf, hb, o, x3, mod, *consts)
    return out.reshape(n, d)


def kernel(x, c, ctx, c_ctx, mod_w, mod_b, norm_g, m_w_in, m_b_gate, m_w_conv, m_g_head, m_w_out,
           h_w_in, h_b_f, h_lb_raw, h_g_head, h_w_out, f_w_gu, f_w_down,
           e_w_router, e_b_router, e_w_gu, e_w_down):
    batch, seq, d = x.shape
    ctx_len = ctx.shape[1]
    depth = mod_w.shape[0]
    assert seq % GRID_W == 0 and seq // GRID_W == CHUNK and ctx_len % CHUNK == 0
    assert batch < MOD_ROWS and d % LANES == 0
    lat_seg = _Seg(batch, seq, False)
    ctx_seg = _Seg(batch, ctx_len, True)

    c_all = jnp.zeros((MOD_ROWS, d), F32).at[:batch].set(c).at[batch].set(c_ctx)
    mod_all = _modulation(c_all, mod_w, mod_b).reshape(depth, MOD_ROWS, N_MOD, d)

    lb_all = jax.nn.softmax(h_lb_raw.astype(F32), axis=0)
    lb_all = jnp.cumsum(lb_all, axis=0) - lb_all[0]

    xl = x.reshape(batch * seq, d)
    xc = ctx.reshape(batch * ctx_len, d)
    row = lambda v: v.reshape(1, -1).astype(F32)

    for i in range(depth):
        j = i // 2
        with_ctx = i < depth - 1
        mod = mod_all[i]
        g0, g1, g2, g3 = (row(norm_g[i, s]) for s in range(4))
        if i % 2 == 0:
            qk_w = 2 * M_HEADS * M_DK
            v_w = M_HEADS * M_DV
            w_in = m_w_in[j]
            wqk = w_in[:, :qk_w].astype(BF16)
            wv = w_in[:, qk_w:qk_w + v_w].astype(BF16)
            wo = w_in[:, qk_w + v_w:qk_w + 2 * v_w].astype(BF16)
            ng = 4 * M_HEADS
            wg = jnp.zeros((d, LANES), F32).at[:, :ng].set(w_in[:, qk_w + 2 * v_w:]).astype(BF16)
            gb = jnp.zeros((1, LANES), F32).at[0, :ng].set(m_b_gate[j].reshape(-1))
            col = jnp.arange(LANES)
            gm = (((col // M_HEADS) % 2 == 1) & (col < ng)).astype(F32).reshape(1, LANES)
            pw = (mod, g0, wqk, wv, wo, wg, m_w_conv[j].astype(F32), gb, gm)
            qc, kc, vc, oc, gc = _proj_m(xc, ctx_seg, *pw)
            ql, kl, vl, ol, gl = _proj_m(xl, lat_seg, *pw)
            s0 = jnp.zeros((batch, 2, M_HEADS // 2, LANES, 2 * M_DV), F32)
            m0 = jnp.zeros((batch, 2 * M_HEADS, LANES), F32)
            hcf, hcb, s1, m1 = _mlstm_scan(qc, kc, vc, gc, s0, m0, ctx_seg)
            hlf, hlb, _, _ = _mlstm_scan(ql, kl, vl, gl, s1, m1, lat_seg)
            gh = row(m_g_head[j])
            wout = m_w_out[j].astype(BF16)
            xl = _mix_out(hlf, hlb, ol, xl, lat_seg, mod, gh, wout, g1, M_HEADS, in_place=i > 0)
            if with_ctx:
                xc = _mix_out(hcf, hcb, oc, xc, ctx_seg, mod, gh, wout, g1, M_HEADS, in_place=i > 0)
        else:
            kw = H_HEADS * H_DK
            w_in = h_w_in[j].astype(BF16)
            ws = [w_in[:, s * kw:(s + 1) * kw] for s in range(5)]
            qc, vc, ffc, fbc, gc = _proj_h_ctx(xc, ctx_seg, mod, g0, ws)
            ql, vl, ffl, fbl, gl = _proj_h_lat(xl, lat_seg, mod, g0, ws)
            bias = h_b_f[j].astype(F32)
            lb = lb_all[j]
            s0 = jnp.zeros((batch, 2, H_HEADS, H_DV, H_DK), F32)
            hcf, hcb, s1 = _hgrn_scan(qc, vc, ffc, fbc, bias, lb, s0, ctx_seg)
            hlf, hlb, _ = _hgrn_scan(ql, vl, ffl, fbl, bias, lb, s1, lat_seg)
            gh = row(h_g_head[j])
            wout = h_w_out[j].astype(BF16)
            xl = _mix_out_col(hlf, hlb, gl, xl, lat_seg, mod, gh, wout, g1, H_HEADS)
            if with_ctx:
                xc = _mix_out(hcf, hcb, gc, xc, ctx_seg, mod, gh, wout, g1, H_HEADS)
        streams = [(xl, lat_seg)] + ([(xc, ctx_seg)] if with_ctx else [])
        outs = []
        if i % 2 == 0:
            wgu, wd = f_w_gu[j].astype(BF16), f_w_down[j].astype(BF16)
            for xs, seg in streams:
                outs.append(_ffn(xs, seg, mod, g2, wgu, wd, g3, th=HIDDEN_CHUNK))
        else:
            wr = jnp.zeros((d, LANES), F32).at[:, :N_EXPERTS].set(e_w_router[j])
            br = jnp.zeros((1, LANES), F32).at[0, :N_EXPERTS].set(e_b_router[j])
            wgu, wd = e_w_gu[j].astype(BF16), e_w_down[j].astype(BF16)
            for xs, seg in streams:
                outs.append(_moe_routed(xs, seg, mod, g2, wr, br, wgu, wd, g3, th=HIDDEN_CHUNK))
        xl = outs[0]
        if with_ctx:
            xc = outs[1]
    return xl.reshape(batch, seq, d)
```
